```python
import jax, jax.numpy as jnp
from jax import lax
import numpy as np

D_MODEL = 1024
BATCH = 4
SEQ = 4096
DEPTH = 1
DEC_BATCH = 128
DEC_SEQ = 8
PAST_LEN = 16384
PAGE_SIZE = 128

HEAD_DIM = 64
ATT_HEADS = 8
KV_HEADS = 2
Q_PER_KV = ATT_HEADS // KV_HEADS
ATT_WIDTH = ATT_HEADS * HEAD_DIM
KV_WIDTH = KV_HEADS * HEAD_DIM
GM_HEADS = 8
GM_WIDTH = GM_HEADS * HEAD_DIM
MIX_WIDTH = ATT_WIDTH + GM_WIDTH
IN_WIDTH = ATT_WIDTH + 2 * KV_WIDTH + 2 * GM_WIDTH
WINDOW = 128
CHUNK = 128
ROPE_THETA = 500000.0
ROT_DIM = HEAD_DIM // 4
D_FF = ((8 * D_MODEL + 3 * 256 - 1) // (3 * 256)) * 256
ALPHA = (2 * DEPTH) ** 0.25
BETA = (8 * DEPTH) ** -0.25
LN_EPS = 1e-5
ATT_SCALE = HEAD_DIM ** -0.5

kernel_name = "hymba_sgu_swa_sink_deepnorm_adaln_step"


def _layer_norm(x, gain=None, bias=None):
    xf = x.astype(jnp.float32)
    mu = jnp.mean(xf, axis=-1, keepdims=True)
    var = jnp.mean(jnp.square(xf - mu), axis=-1, keepdims=True)
    y = (xf - mu) * lax.rsqrt(var + LN_EPS)
    if gain is not None:
        y = y * gain.astype(jnp.float32) + bias.astype(jnp.float32)
    return y.astype(x.dtype)


def _adaln(c, w_ada, b_ada):
    mod = jax.nn.silu(c) @ w_ada + b_ada
    return jnp.split(mod[:, None, :], 6, axis=-1)


def _modulate(x, shift, scale):
    return _layer_norm(x) * (1 + scale) + shift


def _rotary(x, pos):
    half = ROT_DIM // 2
    inv_freq = jnp.power(jnp.float32(ROPE_THETA), -jnp.arange(half, dtype=jnp.float32) * 2.0 / ROT_DIM)
    ang = pos.astype(jnp.float32)[:, None] * inv_freq[None, :]
    cos = jnp.cos(ang)[:, None, :]
    sin = jnp.sin(ang)[:, None, :]
    xf = x.astype(jnp.float32)
    x1 = xf[..., :half]
    x2 = xf[..., half:ROT_DIM]
    out = jnp.concatenate([x1 * cos - x2 * sin, x2 * cos + x1 * sin, xf[..., ROT_DIM:]], axis=-1)
    return out.astype(x.dtype)


def _project(h, w_in, pos, sgu_g, sgu_b):
    n, s, _ = h.shape
    z = h @ w_in
    q, k, v, zu, zv = jnp.split(z, [ATT_WIDTH, ATT_WIDTH + KV_WIDTH, ATT_WIDTH + 2 * KV_WIDTH,
                                    ATT_WIDTH + 2 * KV_WIDTH + GM_WIDTH], axis=-1)
    q = _rotary(q.reshape(n, s, ATT_HEADS, HEAD_DIM), pos)
    k = _rotary(k.reshape(n, s, KV_HEADS, HEAD_DIM), pos)
    v = v.reshape(n, s, KV_HEADS, HEAD_DIM)
    u = jax.nn.gelu(zu)
    gv = _layer_norm(jax.nn.gelu(zv), sgu_g, sgu_b)
    return q, k, v, u, gv


def _sink_weights(scores, mask, sinks):
    scores = jnp.where(mask, scores, -jnp.inf)
    sink = jnp.broadcast_to(sinks.astype(jnp.float32).reshape(KV_HEADS, Q_PER_KV, 1, 1),
                            scores.shape[:-1] + (1,))
    p = jax.nn.softmax(jnp.concatenate([scores, sink], axis=-1), axis=-1)
    return p[..., :-1]


def _attn_prompt(q, k, v, sinks):
    n, s = q.shape[0], q.shape[1]
    nb = s // WINDOW
    qb = q.reshape(n, nb, WINDOW, KV_HEADS, Q_PER_KV, HEAD_DIM)
    kb = k.reshape(n, nb, WINDOW, KV_HEADS, HEAD_DIM)
    vb = v.reshape(n, nb, WINDOW, KV_HEADS, HEAD_DIM)
    pad = ((0, 0), (1, 0), (0, 0), (0, 0), (0, 0))
    kk = jnp.concatenate([jnp.pad(kb, pad)[:, :-1], kb], axis=2)
    vv = jnp.concatenate([jnp.pad(vb, pad)[:, :-1], vb], axis=2)
    scores = jnp.einsum('bnqkgd,bnskd->bnkgqs', qb, kk, preferred_element_type=jnp.float32) * ATT_SCALE
    qi = jnp.arange(WINDOW)[:, None]
    sj = jnp.arange(2 * WINDOW)[None, :]
    rel = qi + WINDOW - sj
    band = (rel >= 0) & (rel < WINDOW)
    valid = (jnp.arange(nb)[:, None, None] > 0) | (sj[None] >= WINDOW)
    mask = (band[None] & valid)[None, :, None, None]
    p = _sink_weights(scores, mask, sinks)
    o = jnp.einsum('bnkgqs,bnskd->bnqkgd', p.astype(vv.dtype), vv)
    return o.reshape(n, s, ATT_WIDTH)


def _attn_sample(q, k, v, k_buf, v_buf, sinks):
    n, t = q.shape[0], q.shape[1]
    wb = k_buf.shape[1]
    kk = jnp.concatenate([k_buf, k], axis=1)
    vv = jnp.concatenate([v_buf, v], axis=1)
    qg = q.reshape(n, t, KV_HEADS, Q_PER_KV, HEAD_DIM)
    scores = jnp.einsum('bqkgd,bskd->bkgqs', qg, kk, preferred_element_type=jnp.float32) * ATT_SCALE
    q_pos = PAST_LEN + jnp.arange(t)
    k_pos = PAST_LEN - wb + jnp.arange(wb + t)
    rel = q_pos[:, None] - k_pos[None, :]
    mask = (rel >= 0) & (rel < WINDOW)
    p = _sink_weights(scores, mask, sinks)
    o = jnp.einsum('bkgqs,bskd->bqkgd', p.astype(vv.dtype), vv)
    return o.reshape(n, t, ATT_WIDTH), kk[:, -wb:], vv[:, -wb:]


def _sgu_prompt(u, gv, w_s, b_s):
    n, s, _ = u.shape
    nc = s // CHUNK
    vc = gv.reshape(n, nc, CHUNK, GM_HEADS, HEAD_DIM)
    sv = jnp.einsum('hts,bnshd->bnthd', jnp.tril(w_s), vc) + b_s.T[:, :, None]
    return u * sv.reshape(n, s, GM_WIDTH)


def _sgu_sample(u, gv, w_s, b_s):
    n, t, _ = u.shape
    vt = gv.reshape(n, t, GM_HEADS, HEAD_DIM)
    w = jnp.tril(w_s)[:, :t, :t]
    sv = jnp.einsum('hts,bshd->bthd', w, vt) + b_s[:, :t].T[:, :, None]
    return u * sv.reshape(n, t, GM_WIDTH)


def _ffn(h, w_gate, w_up, w_down):
    return (jax.nn.silu(h @ w_gate) * (h @ w_up)) @ w_down


def _post(x, f, gate, g, b):
    return _layer_norm(ALPHA * x + gate * f, g, b)


def setup_inputs(seed: int = 0) -> dict:
    key = jax.random.key(seed)
    ks = jax.random.split(key, 24)
    f32 = jnp.float32
    wb = min(WINDOW, PAST_LEN)
    nrm = lambda k, shape: jax.random.normal(k, shape, dtype=f32)
    return {
        "x_prompt": nrm(ks[0], (BATCH, SEQ, D_MODEL)),
        "x_sample": nrm(ks[1], (DEC_BATCH, DEC_SEQ, D_MODEL)),
        "cache_k_win": nrm(ks[2], (DEPTH, DEC_BATCH, wb, KV_HEADS, HEAD_DIM)),
        "cache_v_win": nrm(ks[3], (DEPTH, DEC_BATCH, wb, KV_HEADS, HEAD_DIM)),
        "c_prompt": nrm(ks[4], (BATCH, D_MODEL)),
        "c_sample": nrm(ks[5], (DEC_BATCH, D_MODEL)),
        "w_ada": nrm(ks[6], (DEPTH, D_MODEL, 6 * D_MODEL)) * (0.5 * D_MODEL ** -0.5),
        "b_ada": nrm(ks[7], (DEPTH, 6 * D_MODEL)) * 0.02,
        "w_in": nrm(ks[8], (DEPTH, D_MODEL, IN_WIDTH)) * D_MODEL ** -0.5,
        "attn_sinks": nrm(ks[9], (DEPTH, ATT_HEADS)),
        "sgu_ln_g": 1.0 + 0.02 * nrm(ks[10], (DEPTH, GM_WIDTH)),
        "sgu_ln_b": 0.02 * nrm(ks[11], (DEPTH, GM_WIDTH)),
        "w_s": nrm(ks[12], (DEPTH, GM_HEADS, CHUNK, CHUNK)) * CHUNK ** -0.5,
        "b_s": 1.0 + 0.02 * nrm(ks[13], (DEPTH, GM_HEADS, CHUNK)),
        "w_o": nrm(ks[14], (DEPTH, MIX_WIDTH, D_MODEL)) * (MIX_WIDTH ** -0.5 * BETA),
        "ln1_g": 1.0 + 0.02 * nrm(ks[15], (DEPTH, D_MODEL)),
        "ln1_b": 0.02 * nrm(ks[16], (DEPTH, D_MODEL)),
        "w_gate": nrm(ks[17], (DEPTH, D_MODEL, D_FF)) * D_MODEL ** -0.5,
        "w_up": nrm(ks[18], (DEPTH, D_MODEL, D_FF)) * D_MODEL ** -0.5,
        "w_down": nrm(ks[19], (DEPTH, D_FF, D_MODEL)) * (D_FF ** -0.5 * BETA),
        "ln2_g": 1.0 + 0.02 * nrm(ks[20], (DEPTH, D_MODEL)),
        "ln2_b": 0.02 * nrm(ks[21], (DEPTH, D_MODEL)),
    }


def reference(x_prompt, x_sample, cache_k_win, cache_v_win, c_prompt, c_sample,
              w_ada, b_ada, w_in, attn_sinks, sgu_ln_g, sgu_ln_b, w_s, b_s, w_o,
              ln1_g, ln1_b, w_gate, w_up, w_down, ln2_g, ln2_b):
    yp, ys = x_prompt, x_sample
    pos_p = jnp.arange(yp.shape[1], dtype=jnp.int32)
    pos_s = PAST_LEN + jnp.arange(ys.shape[1], dtype=jnp.int32)
    kwp, vwp, kws, vws, sgv = [], [], [], [], []
    for l in range(DEPTH):
        mp = _adaln(c_prompt, w_ada[l], b_ada[l])
        ms = _adaln(c_sample, w_ada[l], b_ada[l])
        q, k, v, u, gv = _project(_modulate(yp, mp[0], mp[1]), w_in[l], pos_p, sgu_ln_g[l], sgu_ln_b[l])
        mix = jnp.concatenate([_attn_prompt(q, k, v, attn_sinks[l]),
                               _sgu_prompt(u, gv, w_s[l], b_s[l])], axis=-1) @ w_o[l]
        yp = _post(yp, mix, mp[2], ln1_g[l], ln1_b[l])
        kwp.append(k[:, -WINDOW:])
        vwp.append(v[:, -WINDOW:])
        yp = _post(yp, _ffn(_modulate(yp, mp[3], mp[4]), w_gate[l], w_up[l], w_down[l]),
                   mp[5], ln2_g[l], ln2_b[l])
        q, k, v, u, gv = _project(_modulate(ys, ms[0], ms[1]), w_in[l], pos_s, sgu_ln_g[l], sgu_ln_b[l])
        a, k_new, v_new = _attn_sample(q, k, v, cache_k_win[l], cache_v_win[l], attn_sinks[l])
        mix = jnp.concatenate([a, _sgu_sample(u, gv, w_s[l], b_s[l])], axis=-1) @ w_o[l]
        ys = _post(ys, mix, ms[2], ln1_g[l], ln1_b[l])
        kws.append(k_new)
        vws.append(v_new)
        sgv.append(gv)
        ys = _post(ys, _ffn(_modulate(ys, ms[3], ms[4]), w_gate[l], w_up[l], w_down[l]),
                   ms[5], ln2_g[l], ln2_b[l])
    new_k_win_prompt = jnp.stack(kwp, axis=0)
    new_v_win_prompt = jnp.stack(vwp, axis=0)
    new_k_win_sample = jnp.stack(kws, axis=0)
    new_v_win_sample = jnp.stack(vws, axis=0)
    new_sgu_v_sample = jnp.stack(sgv, axis=0)
    return (yp, ys, new_k_win_prompt, new_v_win_prompt, new_k_win_sample, new_v_win_sample, new_sgu_v_sample)
```

```python
import functools

import jax
import jax.numpy as jnp
from jax import lax
from jax.experimental import pallas as pl
from jax.experimental.pallas import tpu as pltpu

D_MODEL = 1024
HEAD_DIM = 64
ATT_HEADS = 8
KV_HEADS = 2
ATT_WIDTH = ATT_HEADS * HEAD_DIM
KV_WIDTH = KV_HEADS * HEAD_DIM
GM_HEADS = 8
GM_WIDTH = GM_HEADS * HEAD_DIM
IN_WIDTH = ATT_WIDTH + 2 * KV_WIDTH + 2 * GM_WIDTH
WINDOW = 128
CHUNK = 128
PAST_LEN = 16384
ROPE_THETA = 500000.0
ROT_DIM = HEAD_DIM // 4
DEPTH = 1
ALPHA = (2 * DEPTH) ** 0.25
LN_EPS = 1e-5
ATT_SCALE = HEAD_DIM ** -0.5

LANES = 128
VMEM_LIMIT_BYTES = 56 * 1024 * 1024

BF16 = jnp.bfloat16
F32 = jnp.float32


def _norm(x):
    mu = jnp.mean(x, axis=-1, keepdims=True)
    xc = x - mu
    var = jnp.mean(xc * xc, axis=-1, keepdims=True)
    return xc * lax.rsqrt(var + LN_EPS)


def _lo_half():
    return lax.broadcasted_iota(jnp.int32, (1, LANES), 1) < HEAD_DIM


def _rotary(x, cos, sin_up, sin_dn):
    half = ROT_DIM // 2
    return (x * cos + pltpu.roll(x, LANES - half, 1) * sin_up
            + pltpu.roll(x, half, 1) * sin_dn)


def _ada_kernel(c_ref, w_ref, b_ref, o_ref):
    c = c_ref[...]
    a = (c * jax.nn.sigmoid(c)).astype(BF16)
    o_ref[...] = jnp.dot(a, w_ref[...].astype(BF16), preferred_element_type=F32) + b_ref[...]


def _ada(c_all, w_ada, b_ada, tn=1024):
    rows = c_all.shape[0]
    n = w_ada.shape[1]
    return pl.pallas_call(
        _ada_kernel,
        grid=(n // tn,),
        in_specs=[
            pl.BlockSpec((rows, D_MODEL), lambda j: (0, 0)),
            pl.BlockSpec((D_MODEL, tn), lambda j: (0, j)),
            pl.BlockSpec((1, tn), lambda j: (0, j)),
        ],
        out_specs=pl.BlockSpec((rows, tn), lambda j: (0, j)),
        out_shape=jax.ShapeDtypeStruct((rows, n), F32),
        compiler_params=pltpu.CompilerParams(dimension_semantics=("arbitrary",)),
        name="ada",
    )(c_all, w_ada, b_ada)


def _prompt_attention(qj, kcat, vcat, sinks_ref, prev_valid):
    lo = _lo_half()
    parts = []
    for c in range(4):
        qc = qj[:, c * LANES:(c + 1) * LANES]
        parts.append(jnp.where(lo, qc, 0.0))
        parts.append(jnp.where(lo, 0.0, qc))
    lhs = jnp.concatenate(parts, axis=0).astype(BF16)
    s = lax.dot_general(lhs, kcat, (((1,), (1,)), ((), ())), preferred_element_type=F32)
    row = lax.broadcasted_iota(jnp.int32, (WINDOW, 2 * WINDOW), 0)
    col = lax.broadcasted_iota(jnp.int32, (WINDOW, 2 * WINDOW), 1)
    mask = (col > row) & (col <= row + WINDOW)
    if prev_valid is not None:
        mask = mask & (col >= jnp.where(prev_valid, 0, WINDOW))
    es, invs = [], []
    for hb in range(8):
        sink = sinks_ref[(hb // 2) + 4 * (hb % 2)]
        sh = jnp.where(mask, s[hb * WINDOW:(hb + 1) * WINDOW], -jnp.inf)
        m = jnp.maximum(jnp.max(sh, axis=-1, keepdims=True), sink)
        e = jnp.exp(sh - m)
        den = jnp.sum(e, axis=-1, keepdims=True) + jnp.exp(sink - m)
        es.append(e.astype(BF16))
        invs.append(1.0 / den)
    p = jnp.concatenate(es, axis=0)
    o = jnp.dot(p, vcat, preferred_element_type=F32)
    cols = []
    for c in range(4):
        o_lo = o[(2 * c) * WINDOW:(2 * c + 1) * WINDOW] * invs[2 * c]
        o_hi = o[(2 * c + 1) * WINDOW:(2 * c + 2) * WINDOW] * invs[2 * c + 1]
        cols.append(jnp.where(lo, o_lo, o_hi))
    return jnp.concatenate(cols, axis=1)


def _prompt_sgu(uj, gvj, wsp_ref, bexp):
    lo = _lo_half()
    svs = []
    for p in range(4):
        g = gvj[:, p * LANES:(p + 1) * LANES]
        rhs = jnp.concatenate([jnp.where(lo, g, 0.0), jnp.where(lo, 0.0, g)], axis=0).astype(BF16)
        svs.append(jnp.dot(wsp_ref[p], rhs, preferred_element_type=F32))
    sv = jnp.concatenate(svs, axis=1)
    return uj * (sv + bexp)


def _mixer_prompt_kernel(sinks_ref, x_ref, mod_ref, cos_ref, sup_ref, sdn_ref, w_in_ref, wsp_ref,
                         bexp_ref, sg_ref, sb_ref, w_o_ref, g1_ref, b1_ref,
                         y_ref, kwin_ref, vwin_ref, kprev_ref, vprev_ref, *, tq):
    i = pl.program_id(1)
    nb = pl.num_programs(1)

    @pl.when(i == 0)
    def _():
        kprev_ref[...] = jnp.zeros_like(kprev_ref)
        vprev_ref[...] = jnp.zeros_like(vprev_ref)

    x = x_ref[...]
    shift, scale, gate = mod_ref[0:1, :], mod_ref[1:2, :], mod_ref[2:3, :]
    h = (_norm(x) * (1.0 + scale) + shift).astype(BF16)
    z = jnp.dot(h, w_in_ref[...], preferred_element_type=F32)

    cos, sup, sdn = cos_ref[...], sup_ref[...], sdn_ref[...]
    q = jnp.concatenate(
        [_rotary(z[:, c * LANES:(c + 1) * LANES], cos, sup, sdn) for c in range(4)], axis=1)
    k = _rotary(z[:, ATT_WIDTH:ATT_WIDTH + KV_WIDTH], cos, sup, sdn)
    v = z[:, ATT_WIDTH + KV_WIDTH:ATT_WIDTH + 2 * KV_WIDTH]
    u = jax.nn.gelu(z[:, ATT_WIDTH + 2 * KV_WIDTH:ATT_WIDTH + 2 * KV_WIDTH + GM_WIDTH])
    gv = _norm(jax.nn.gelu(z[:, ATT_WIDTH + 2 * KV_WIDTH + GM_WIDTH:])) * sg_ref[...] + sb_ref[...]

    kb = k.astype(BF16)
    vb = v.astype(BF16)
    bexp = bexp_ref[...]
    mixes = []
    for j in range(tq // WINDOW):
        sl = slice(j * WINDOW, (j + 1) * WINDOW)
        if j == 0:
            kcat = jnp.concatenate([kprev_ref[...], kb[sl]], axis=0)
            vcat = jnp.concatenate([vprev_ref[...], vb[sl]], axis=0)
            prev_valid = i > 0
        else:
            kcat = kb[(j - 1) * WINDOW:(j + 1) * WINDOW]
            vcat = vb[(j - 1) * WINDOW:(j + 1) * WINDOW]
            prev_valid = None
        att = _prompt_attention(q[sl], kcat, vcat, sinks_ref, prev_valid)
        sgu = _prompt_sgu(u[sl], gv[sl], wsp_ref, bexp)
        mixes.append(jnp.concatenate([att, sgu], axis=1).astype(BF16))
    mixcat = jnp.concatenate(mixes, axis=0)
    mix = jnp.dot(mixcat, w_o_ref[...], preferred_element_type=F32)
    y_ref[...] = _norm(ALPHA * x + gate * mix) * g1_ref[...] + b1_ref[...]

    kprev_ref[...] = kb[tq - WINDOW:]
    vprev_ref[...] = vb[tq - WINDOW:]

    @pl.when(i == nb - 1)
    def _():
        kwin_ref[...] = k[tq - WINDOW:]
        vwin_ref[...] = v[tq - WINDOW:]


def _const_spec(shape):
    nd = len(shape)
    return pl.BlockSpec(shape, lambda *_: (0,) * nd)


def _mixer_prompt(x, mod, sinks, tabs, w_in_p, wsp, bexp, sgu_g, sgu_b, w_o_p, g1, b1, tq=256):
    bsz, seq, _ = x.shape
    cos, sup, sdn = tabs
    tab_spec = pl.BlockSpec((tq, LANES), lambda b, i: (i, 0))
    return pl.pallas_call(
        functools.partial(_mixer_prompt_kernel, tq=tq),
        grid=(bsz, seq // tq),
        in_specs=[
            pl.BlockSpec(memory_space=pltpu.SMEM),
            pl.BlockSpec((None, tq, D_MODEL), lambda b, i: (b, i, 0)),
            pl.BlockSpec((None, 6, D_MODEL), lambda b, i: (b, 0, 0)),
            tab_spec, tab_spec, tab_spec,
            _const_spec(w_in_p.shape),
            _const_spec(wsp.shape),
            _const_spec(bexp.shape),
            _const_spec(sgu_g.shape),
            _const_spec(sgu_b.shape),
            _const_spec(w_o_p.shape),
            _const_spec(g1.shape),
            _const_spec(b1.shape),
        ],
        out_specs=[
            pl.BlockSpec((None, tq, D_MODEL), lambda b, i: (b, i, 0)),
            pl.BlockSpec((None, WINDOW, KV_WIDTH), lambda b, i: (b, 0, 0)),
            pl.BlockSpec((None, WINDOW, KV_WIDTH), lambda b, i: (b, 0, 0)),
        ],
        out_shape=[
            jax.ShapeDtypeStruct((bsz, seq, D_MODEL), F32),
            jax.ShapeDtypeStruct((bsz, WINDOW, KV_WIDTH), F32),
            jax.ShapeDtypeStruct((bsz, WINDOW, KV_WIDTH), F32),
        ],
        scratch_shapes=[pltpu.VMEM((WINDOW, KV_WIDTH), BF16), pltpu.VMEM((WINDOW, KV_WIDTH), BF16)],
        compiler_params=pltpu.CompilerParams(
            dimension_semantics=("arbitrary", "arbitrary"), vmem_limit_bytes=VMEM_LIMIT_BYTES),
        name="mixer_prompt",
    )(sinks, x, mod, cos, sup, sdn, w_in_p, wsp, bexp, sgu_g, sgu_b, w_o_p, g1, b1)


def _mixer_sample_kernel(x_ref, mod_ref, cos_ref, sup_ref, sdn_ref, kc_ref, vc_ref, sink_ref,
                         w_in_ref, wsx_ref, bexp_ref, sg_ref, sb_ref, w_o_ref, g1_ref, b1_ref,
                         y_ref, kwin_ref, vwin_ref, gv_ref, *, bb, t):
    x3 = x_ref[...]
    shift, scale, gate = mod_ref[:, 0:1, :], mod_ref[:, 1:2, :], mod_ref[:, 2:3, :]
    h = (_norm(x3) * (1.0 + scale) + shift).reshape(bb * t, D_MODEL).astype(BF16)
    z = jnp.dot(h, w_in_ref[...], preferred_element_type=F32)

    cos, sup, sdn = cos_ref[...], sup_ref[...], sdn_ref[...]
    qcols = [_rotary(z[:, c * LANES:(c + 1) * LANES], cos, sup, sdn).reshape(bb, t, LANES)
             for c in range(4)]
    k3 = _rotary(z[:, ATT_WIDTH:ATT_WIDTH + KV_WIDTH], cos, sup, sdn).reshape(bb, t, KV_WIDTH)
    v3 = z[:, ATT_WIDTH + KV_WIDTH:ATT_WIDTH + 2 * KV_WIDTH].reshape(bb, t, KV_WIDTH)
    u3 = jax.nn.gelu(z[:, ATT_WIDTH + 2 * KV_WIDTH:ATT_WIDTH + 2 * KV_WIDTH + GM_WIDTH]
                     ).reshape(bb, t, GM_WIDTH)
    gv = _norm(jax.nn.gelu(z[:, ATT_WIDTH + 2 * KV_WIDTH + GM_WIDTH:])) * sg_ref[...] + sb_ref[...]
    gv3 = gv.reshape(bb, t, GM_WIDTH)
    gv_ref[...] = gv3

    kc = kc_ref[...]
    vc = vc_ref[...]
    wb = kc.shape[1]
    kwin_ref[:, 0:wb - t, :] = kc[:, t:, :]
    kwin_ref[:, wb - t:, :] = k3
    vwin_ref[:, 0:wb - t, :] = vc[:, t:, :]
    vwin_ref[:, wb - t:, :] = v3

    lo = _lo_half()[None]
    parts = []
    for c in range(4):
        parts.append(jnp.where(lo, qcols[c], 0.0))
        parts.append(jnp.where(lo, 0.0, qcols[c]))
    lhs = jnp.concatenate(parts, axis=1).astype(BF16)
    kcat = jnp.concatenate([kc, k3], axis=1).astype(BF16)
    vcat = jnp.concatenate([vc, v3], axis=1).astype(BF16)
    s = jnp.einsum("bqd,bkd->bqk", lhs, kcat, preferred_element_type=F32)
    nq, nk = 8 * t, wb + t
    tok = lax.broadcasted_iota(jnp.int32, (nq, nk), 0) % t
    col = lax.broadcasted_iota(jnp.int32, (nq, nk), 1)
    rel = tok + wb - col
    mask = ((rel >= 0) & (rel < WINDOW))[None]
    sink = sink_ref[:, 0:1][None]
    sm = jnp.where(mask, s, -jnp.inf)
    m = jnp.maximum(jnp.max(sm, axis=-1, keepdims=True), sink)
    e = jnp.exp(sm - m)
    inv = 1.0 / (jnp.sum(e, axis=-1, keepdims=True) + jnp.exp(sink - m))
    o = jnp.einsum("bqk,bkd->bqd", e.astype(BF16), vcat, preferred_element_type=F32) * inv
    att = jnp.concatenate(
        [jnp.where(lo, o[:, (2 * c) * t:(2 * c + 1) * t, :], o[:, (2 * c + 1) * t:(2 * c + 2) * t, :])
         for c in range(4)], axis=-1)

    sv = jnp.zeros((bb, t, GM_WIDTH), F32)
    for si in range(t):
        sv = sv + wsx_ref[si][None] * gv3[:, si:si + 1, :]
    sgu = u3 * (sv + bexp_ref[...][None])

    mixcat = jnp.concatenate([att, sgu], axis=-1).reshape(bb * t, D_MODEL).astype(BF16)
    mix = jnp.dot(mixcat, w_o_ref[...], preferred_element_type=F32).reshape(bb, t, D_MODEL)
    y_ref[...] = _norm(ALPHA * x3 + gate * mix) * g1_ref[...] + b1_ref[...]


def _mixer_sample(x, mod, tabs, kc, vc, sink_rows, w_in_p, wsx, bexp_t, sgu_g, sgu_b, w_o_p,
                  g1, b1, bb=32):
    n, t, _ = x.shape
    wb = kc.shape[1]
    cos, sup, sdn = tabs
    row3 = lambda r, c: pl.BlockSpec((bb, r, c), lambda b: (b, 0, 0))
    return pl.pallas_call(
        functools.partial(_mixer_sample_kernel, bb=bb, t=t),
        grid=(n // bb,),
        in_specs=[
            row3(t, D_MODEL), row3(6, D_MODEL),
            _const_spec(cos.shape), _const_spec(sup.shape), _const_spec(sdn.shape),
            row3(wb, KV_WIDTH), row3(wb, KV_WIDTH),
            _const_spec(sink_rows.shape),
            _const_spec(w_in_p.shape), _const_spec(wsx.shape), _const_spec(bexp_t.shape),
            _const_spec(sgu_g.shape), _const_spec(sgu_b.shape), _const_spec(w_o_p.shape),
            _const_spec(g1.shape), _const_spec(b1.shape),
        ],
        out_specs=[row3(t, D_MODEL), row3(wb, KV_WIDTH), row3(wb, KV_WIDTH), row3(t, GM_WIDTH)],
        out_shape=[
            jax.ShapeDtypeStruct((n, t, D_MODEL), F32),
            jax.ShapeDtypeStruct((n, wb, KV_WIDTH), F32),
            jax.ShapeDtypeStruct((n, wb, KV_WIDTH), F32),
            jax.ShapeDtypeStruct((n, t, GM_WIDTH), F32),
        ],
        compiler_params=pltpu.CompilerParams(
            dimension_semantics=("arbitrary",), vmem_limit_bytes=VMEM_LIMIT_BYTES),
        name="mixer_sample",
    )(x, mod, cos, sup, sdn, kc, vc, sink_rows, w_in_p, wsx, bexp_t, sgu_g, sgu_b, w_o_p, g1, b1)


def _ffn_kernel(x_ref, mod_ref, wg_ref, wu_ref, wd_ref, g2_ref, b2_ref, y_ref):
    x = x_ref[...]
    if x.ndim == 3:
        shift, scale, gate = mod_ref[:, 3:4, :], mod_ref[:, 4:5, :], mod_ref[:, 5:6, :]
    else:
        shift, scale, gate = mod_ref[3:4, :], mod_ref[4:5, :], mod_ref[5:6, :]
    h = (_norm(x) * (1.0 + scale) + shift).reshape(-1, D_MODEL).astype(BF16)
    g = jnp.dot(h, wg_ref[...], preferred_element_type=F32)
    up = jnp.dot(h, wu_ref[...], preferred_element_type=F32)
    a = (g * jax.nn.sigmoid(g) * up).astype(BF16)
    f = jnp.dot(a, wd_ref[...], preferred_element_type=F32).reshape(x.shape)
    y_ref[...] = _norm(ALPHA * x + gate * f) * g2_ref[...] + b2_ref[...]


def _ffn(x, mod, wg, wu, wd, g2, b2, *, rows):
    gsz, r, _ = x.shape
    if r >= rows:
        grid = (gsz, r // rows)
        x_spec = pl.BlockSpec((None, rows, D_MODEL), lambda b, i: (b, i, 0))
        m_spec = pl.BlockSpec((None, 6, D_MODEL), lambda b, i: (b, 0, 0))
    else:
        bb = rows // r
        grid = (gsz // bb, 1)
        x_spec = pl.BlockSpec((bb, r, D_MODEL), lambda b, i: (b, 0, 0))
        m_spec = pl.BlockSpec((bb, 6, D_MODEL), lambda b, i: (b, 0, 0))
    return pl.pallas_call(
        _ffn_kernel,
        grid=grid,
        in_specs=[x_spec, m_spec, _const_spec(wg.shape), _const_spec(wu.shape),
                  _const_spec(wd.shape), _const_spec(g2.shape), _const_spec(b2.shape)],
        out_specs=x_spec,
        out_shape=jax.ShapeDtypeStruct(x.shape, F32),
        compiler_params=pltpu.CompilerParams(
            dimension_semantics=("arbitrary", "arbitrary"), vmem_limit_bytes=VMEM_LIMIT_BYTES),
        name="ffn",
    )(x, mod, wg, wu, wd, g2, b2)


def _rotary_tables(pos):
    half = ROT_DIM // 2
    inv_freq = jnp.power(jnp.float32(ROPE_THETA), -jnp.arange(half, dtype=F32) * 2.0 / ROT_DIM)
    ang = pos.astype(F32)[:, None] * inv_freq[None, :]
    cos, sin = jnp.cos(ang), jnp.sin(ang)
    n = pos.shape[0]
    ones = jnp.ones((n, HEAD_DIM - ROT_DIM), F32)
    zeros = jnp.zeros((n, HEAD_DIM - ROT_DIM), F32)
    zh = jnp.zeros((n, half), F32)
    cos_h = jnp.concatenate([cos, cos, ones], axis=1)
    sup_h = jnp.concatenate([-sin, zh, zeros], axis=1)
    sdn_h = jnp.concatenate([zh, sin, zeros], axis=1)
    return tuple(jnp.concatenate([a, a], axis=1) for a in (cos_h, sup_h, sdn_h))


def _q_perm():
    idx = []
    for c in range(4):
        idx += list(range(c * HEAD_DIM, (c + 1) * HEAD_DIM))
        idx += list(range((4 + c) * HEAD_DIM, (5 + c) * HEAD_DIM))
    return jnp.asarray(idx, dtype=jnp.int32)


def kernel(x_prompt, x_sample, cache_k_win, cache_v_win, c_prompt, c_sample, w_ada, b_ada, w_in, attn_sinks, sgu_ln_g, sgu_ln_b, w_s, b_s, w_o, ln1_g, ln1_b, w_gate, w_up, w_down, ln2_g, ln2_b):
    bsz, seq, _ = x_prompt.shape
    nsm, t, _ = x_sample.shape
    wb = cache_k_win.shape[2]
    perm = _q_perm()
    pos_p = jnp.arange(seq, dtype=jnp.int32)
    pos_s = PAST_LEN + jnp.arange(t, dtype=jnp.int32)
    tabs_p = _rotary_tables(pos_p)
    bb = 32
    tabs_s = tuple(jnp.tile(a, (bb, 1)) for a in _rotary_tables(pos_s))

    yp, ys = x_prompt, x_sample
    kwp, vwp, kws, vws, sgv = [], [], [], [], []
    for l in range(DEPTH):
        w_in_p = jnp.concatenate(
            [w_in[l][:, perm] * ATT_SCALE, w_in[l][:, ATT_WIDTH:]], axis=1).astype(BF16)
        w_o_p = jnp.concatenate([w_o[l][perm, :], w_o[l][ATT_WIDTH:, :]], axis=0).astype(BF16)
        wtril = jnp.tril(w_s[l])
        wsp = jnp.concatenate([wtril[0::2], wtril[1::2]], axis=2).astype(BF16)
        bexp = jnp.repeat(b_s[l].T, HEAD_DIM, axis=1)
        wsx = jnp.repeat(jnp.transpose(wtril[:, :t, :t], (2, 1, 0)), HEAD_DIM, axis=-1)
        sinks = attn_sinks[l]
        sink_rows = jnp.broadcast_to(
            jnp.repeat(sinks[jnp.asarray([0, 4, 1, 5, 2, 6, 3, 7])], t)[:, None], (8 * t, LANES))
        sg, sb = sgu_ln_g[l][None, :], sgu_ln_b[l][None, :]
        g1, b1 = ln1_g[l][None, :], ln1_b[l][None, :]
        g2, b2 = ln2_g[l][None, :], ln2_b[l][None, :]
        wg, wu, wd = w_gate[l].astype(BF16), w_up[l].astype(BF16), w_down[l].astype(BF16)

        nrow = bsz + nsm
        pad = (-nrow) % 8
        c_all = jnp.concatenate([c_prompt, c_sample, jnp.zeros((pad, D_MODEL), F32)], axis=0)
        mod = _ada(c_all, w_ada[l], b_ada[l][None, :])
        mod_p = mod[:bsz].reshape(bsz, 6, D_MODEL)
        mod_s = mod[bsz:nrow].reshape(nsm, 6, D_MODEL)

        yp, kw, vw = _mixer_prompt(yp, mod_p, sinks, tabs_p, w_in_p, wsp, bexp, sg, sb, w_o_p, g1, b1)
        yp = _ffn(yp, mod_p, wg, wu, wd, g2, b2, rows=512)
        kwp.append(kw.reshape(bsz, WINDOW, KV_HEADS, HEAD_DIM))
        vwp.append(vw.reshape(bsz, WINDOW, KV_HEADS, HEAD_DIM))

        kc = cache_k_win[l].reshape(nsm, wb, KV_WIDTH)
        vc = cache_v_win[l].reshape(nsm, wb, KV_WIDTH)
        ys, kn, vn, gvs = _mixer_sample(ys, mod_s, tabs_s, kc, vc, sink_rows, w_in_p, wsx,
                                        bexp[:t], sg, sb, w_o_p, g1, b1, bb=bb)
        ys = _ffn(ys, mod_s, wg, wu, wd, g2, b2, rows=512)
        kws.append(kn.reshape(nsm, wb, KV_HEADS, HEAD_DIM))
        vws.append(vn.reshape(nsm, wb, KV_HEADS, HEAD_DIM))
        sgv.append(gvs)
    return (yp, ys, jnp.stack(kwp, axis=0), jnp.stack(vwp, axis=0), jnp.stack(kws, axis=0),
            jnp.stack(vws, axis=0), jnp.stack(sgv, axis=0))
```

```python
import functools

import jax
import jax.numpy as jnp
from jax import lax
from jax.experimental import pallas as pl
from jax.experimental.pallas import tpu as pltpu

D_MODEL = 1024
HEAD_DIM = 64
ATT_HEADS = 8
KV_HEADS = 2
Q_PER_KV = ATT_HEADS // KV_HEADS
ATT_WIDTH = ATT_HEADS * HEAD_DIM
KV_WIDTH = KV_HEADS * HEAD_DIM
GM_HEADS = 8
GM_WIDTH = GM_HEADS * HEAD_DIM
IN_WIDTH = ATT_WIDTH + 2 * KV_WIDTH + 2 * GM_WIDTH
D_FF = 2816
WINDOW = 128
CHUNK = 128
PAST_LEN = 16384
ROPE_THETA = 500000.0
ROT_DIM = HEAD_DIM // 4
DEPTH = 1
ALPHA = (2 * DEPTH) ** 0.25
LN_EPS = 1e-5
ATT_SCALE = HEAD_DIM ** -0.5

LANES = 128
VMEM_LIMIT_BYTES = 56 * 1024 * 1024

BF16 = jnp.bfloat16
F32 = jnp.float32

HEADS_STRAIGHT = tuple(h for h in range(ATT_HEADS) if h % 2 == h // Q_PER_KV)
HEADS_SWAPPED = tuple(h for h in range(ATT_HEADS) if h % 2 != h // Q_PER_KV)


def _norm(x):
    mu = jnp.mean(x, axis=-1, keepdims=True)
    xc = x - mu
    var = jnp.mean(xc * xc, axis=-1, keepdims=True)
    return xc * lax.rsqrt(var + LN_EPS)


def _lo_half():
    return lax.broadcasted_iota(jnp.int32, (1, LANES), 1) < HEAD_DIM


def _rotary_lane_signs():
    half = ROT_DIM // 2
    l = lax.broadcasted_iota(jnp.int32, (1, LANES), 1) % HEAD_DIM
    up = jnp.where(l < half, -1.0, 0.0)
    dn = jnp.where((l >= half) & (l < ROT_DIM), 1.0, 0.0)
    return up, dn


def _rotary(x, cos, sin_up, sin_dn):
    half = ROT_DIM // 2
    return (x * cos + pltpu.roll(x, LANES - half, 1) * sin_up
            + pltpu.roll(x, half, 1) * sin_dn)


def _swap_halves(x):
    return pltpu.roll(x, HEAD_DIM, x.ndim - 1)


def _masked_heads(qcols, heads):
    lo = _lo_half()
    parts = []
    for h in heads:
        qc = qcols[h // 2]
        parts.append(jnp.where(lo, qc, 0.0) if h % 2 == 0 else jnp.where(lo, 0.0, qc))
    return jnp.concatenate(parts, axis=-2).astype(BF16)


def _prep_kernel(win_ref, wo_ref, wg_ref, wu_ref, wd_ref, oin_ref, oo_ref, og_ref, ou_ref, od_ref):
    col = lax.broadcasted_iota(jnp.int32, (1, IN_WIDTH), 1)
    qscale = jnp.where(col < ATT_WIDTH, ATT_SCALE, 1.0)
    oin_ref[...] = (win_ref[...] * qscale).astype(BF16)
    oo_ref[...] = wo_ref[...].astype(BF16)
    og_ref[...] = wg_ref[...].astype(BF16)
    ou_ref[...] = wu_ref[...].astype(BF16)
    od_ref[...] = wd_ref[...].astype(BF16)


def _prep_weights(w_in, w_o, w_gate, w_up, w_down, layer, steps=8):
    def spec(w):
        _, r, c = w.shape
        return pl.BlockSpec((None, r // steps, c), lambda j: (layer, j, 0))

    def ospec(w):
        _, r, c = w.shape
        return pl.BlockSpec((r // steps, c), lambda j: (j, 0))

    ws = (w_in, w_o, w_gate, w_up, w_down)
    return pl.pallas_call(
        _prep_kernel,
        grid=(steps,),
        in_specs=[spec(w) for w in ws],
        out_specs=[ospec(w) for w in ws],
        out_shape=[jax.ShapeDtypeStruct(w.shape[1:], BF16) for w in ws],
        compiler_params=pltpu.CompilerParams(
            dimension_semantics=("arbitrary",), vmem_limit_bytes=VMEM_LIMIT_BYTES),
        name="prep_weights",
    )(*ws)


def _ada_kernel(c_ref, w_ref, b_ref, o_ref):
    c = c_ref[...]
    a = (c * jax.nn.sigmoid(c)).astype(BF16)
    o_ref[...] = jnp.dot(a, w_ref[...].astype(BF16), preferred_element_type=F32) + b_ref[...]


def _ada(c_all, w_ada, b_ada, layer, tn=1024):
    rows = c_all.shape[0]
    n = w_ada.shape[2]
    return pl.pallas_call(
        _ada_kernel,
        grid=(n // tn,),
        in_specs=[
            pl.BlockSpec((rows, D_MODEL), lambda j: (0, 0)),
            pl.BlockSpec((None, D_MODEL, tn), lambda j: (layer, 0, j)),
            pl.BlockSpec((1, tn), lambda j: (layer, j)),
        ],
        out_specs=pl.BlockSpec((rows, tn), lambda j: (0, j)),
        out_shape=jax.ShapeDtypeStruct((rows, n), F32),
        compiler_params=pltpu.CompilerParams(dimension_semantics=("arbitrary",)),
        name="ada",
    )(c_all, w_ada, b_ada)


def _softmax_pv(s, mask, sinks_ref, heads, vmat):
    es, invs = [], []
    for n, h in enumerate(heads):
        sink = sinks_ref[h]
        sh = jnp.where(mask, s[n * WINDOW:(n + 1) * WINDOW], -jnp.inf)
        m = jnp.maximum(jnp.max(sh, axis=-1, keepdims=True), sink)
        e = jnp.exp(sh - m)
        den = jnp.sum(e, axis=-1, keepdims=True) + jnp.exp(sink - m)
        es.append(e.astype(BF16))
        invs.append(1.0 / den)
    o = jnp.dot(jnp.concatenate(es, axis=0), vmat, preferred_element_type=F32)
    return {h: o[n * WINDOW:(n + 1) * WINDOW] * invs[n] for n, h in enumerate(heads)}


def _prompt_attention(qcols, kcat, kcat_sw, vcat, vcat_sw, sinks_ref, first_key):
    row = lax.broadcasted_iota(jnp.int32, (WINDOW, 2 * WINDOW), 0)
    col = lax.broadcasted_iota(jnp.int32, (WINDOW, 2 * WINDOW), 1)
    mask = (col > row) & (col <= row + WINDOW) & (col >= first_key)
    dn = (((1,), (1,)), ((), ()))
    s_a = lax.dot_general(_masked_heads(qcols, HEADS_STRAIGHT), kcat, dn, preferred_element_type=F32)
    s_b = lax.dot_general(_masked_heads(qcols, HEADS_SWAPPED), kcat_sw, dn, preferred_element_type=F32)
    o = _softmax_pv(s_a, mask, sinks_ref, HEADS_STRAIGHT, vcat)
    o.update(_softmax_pv(s_b, mask, sinks_ref, HEADS_SWAPPED, vcat_sw))
    lo = _lo_half()
    return jnp.concatenate([jnp.where(lo, o[2 * c], o[2 * c + 1]) for c in range(4)], axis=1)


def _prompt_sgu(uj, gvj, wsp_ref, bexp):
    lo = _lo_half()
    svs = []
    for p in range(4):
        g = gvj[:, p * LANES:(p + 1) * LANES]
        rhs = jnp.concatenate([jnp.where(lo, g, 0.0), jnp.where(lo, 0.0, g)], axis=0).astype(BF16)
        svs.append(jnp.dot(wsp_ref[p], rhs, preferred_element_type=F32))
    sv = jnp.concatenate(svs, axis=1)
    return uj * (sv + bexp)


def _mixer_prompt_kernel(sinks_ref, x_ref, mod_ref, cr_ref, sr_ref, cb_ref, sb_ref, w_in_ref, wsp_ref,
                         bexp_ref, sg_ref, sb2_ref, w_o_ref, g1_ref, b1_ref,
                         y_ref, kwin_ref, vwin_ref, kprev_ref, vprev_ref, kprev_sw_ref, vprev_sw_ref,
                         *, tq):
    i = pl.program_id(1)
    nb = pl.num_programs(1)

    @pl.when(i == 0)
    def _():
        for r in (kprev_ref, vprev_ref, kprev_sw_ref, vprev_sw_ref):
            r[...] = jnp.zeros_like(r)

    x = x_ref[...]
    shift, scale, gate = mod_ref[0:1, :], mod_ref[1:2, :], mod_ref[2:3, :]
    h = (_norm(x) * (1.0 + scale) + shift).astype(BF16)
    z = jnp.dot(h, w_in_ref[...], preferred_element_type=F32)

    cb, sb = cb_ref[pl.ds(i, 1), :], sb_ref[pl.ds(i, 1), :]
    cr, sr = cr_ref[...], sr_ref[...]
    cos = cb * cr - sb * sr
    sin = sb * cr + cb * sr
    up, dn = _rotary_lane_signs()
    sup, sdn = sin * up, sin * dn

    q = [_rotary(z[:, c * LANES:(c + 1) * LANES], cos, sup, sdn) for c in range(4)]
    k = _rotary(z[:, ATT_WIDTH:ATT_WIDTH + KV_WIDTH], cos, sup, sdn)
    v = z[:, ATT_WIDTH + KV_WIDTH:ATT_WIDTH + 2 * KV_WIDTH]
    u = jax.nn.gelu(z[:, ATT_WIDTH + 2 * KV_WIDTH:ATT_WIDTH + 2 * KV_WIDTH + GM_WIDTH])
    gv = _norm(jax.nn.gelu(z[:, ATT_WIDTH + 2 * KV_WIDTH + GM_WIDTH:])) * sg_ref[...] + sb2_ref[...]

    kb, vb = k.astype(BF16), v.astype(BF16)
    kb_sw, vb_sw = _swap_halves(k).astype(BF16), _swap_halves(v).astype(BF16)
    bexp = bexp_ref[...]
    mixes = []
    for j in range(tq // WINDOW):
        sl = slice(j * WINDOW, (j + 1) * WINDOW)
        if j == 0:
            cats = [jnp.concatenate([p[...], c[sl]], axis=0) for p, c in
                    ((kprev_ref, kb), (kprev_sw_ref, kb_sw), (vprev_ref, vb), (vprev_sw_ref, vb_sw))]
            first_key = jnp.where(i > 0, 0, WINDOW)
        else:
            sl2 = slice((j - 1) * WINDOW, (j + 1) * WINDOW)
            cats = [kb[sl2], kb_sw[sl2], vb[sl2], vb_sw[sl2]]
            first_key = 0
        att = _prompt_attention([qc[sl] for qc in q], *cats, sinks_ref, first_key)
        sgu = _prompt_sgu(u[sl], gv[sl], wsp_ref, bexp)
        mixes.append(jnp.concatenate([att, sgu], axis=1).astype(BF16))
    mixcat = jnp.concatenate(mixes, axis=0)
    mix = jnp.dot(mixcat, w_o_ref[...], preferred_element_type=F32)
    y_ref[...] = _norm(ALPHA * x + gate * mix) * g1_ref[...] + b1_ref[...]

    kprev_ref[...] = kb[tq - WINDOW:]
    vprev_ref[...] = vb[tq - WINDOW:]
    kprev_sw_ref[...] = kb_sw[tq - WINDOW:]
    vprev_sw_ref[...] = vb_sw[tq - WINDOW:]

    @pl.when(i == nb - 1)
    def _():
        kwin_ref[...] = k[tq - WINDOW:]
        vwin_ref[...] = v[tq - WINDOW:]


def _const_spec(shape):
    nd = len(shape)
    return pl.BlockSpec(shape, lambda *_: (0,) * nd)


def _mixer_prompt(x, mod, sinks, tabs, w_in_b, wsp, bexp, sgu_g, sgu_b, w_o_b, g1, b1, tq):
    bsz, seq, _ = x.shape
    carry = pltpu.VMEM((WINDOW, KV_WIDTH), BF16)
    return pl.pallas_call(
        functools.partial(_mixer_prompt_kernel, tq=tq),
        grid=(bsz, seq // tq),
        in_specs=[
            pl.BlockSpec(memory_space=pltpu.SMEM),
            pl.BlockSpec((None, tq, D_MODEL), lambda b, i: (b, i, 0)),
            pl.BlockSpec((None, 6, D_MODEL), lambda b, i: (b, 0, 0)),
            *[_const_spec(t.shape) for t in tabs],
            _const_spec(w_in_b.shape),
            _const_spec(wsp.shape),
            _const_spec(bexp.shape),
            _const_spec(sgu_g.shape),
            _const_spec(sgu_b.shape),
            _const_spec(w_o_b.shape),
            _const_spec(g1.shape),
            _const_spec(b1.shape),
        ],
        out_specs=[
            pl.BlockSpec((None, tq, D_MODEL), lambda b, i: (b, i, 0)),
            pl.BlockSpec((None, WINDOW, KV_WIDTH), lambda b, i: (b, 0, 0)),
            pl.BlockSpec((None, WINDOW, KV_WIDTH), lambda b, i: (b, 0, 0)),
        ],
        out_shape=[
            jax.ShapeDtypeStruct((bsz, seq, D_MODEL), F32),
            jax.ShapeDtypeStruct((bsz, WINDOW, KV_WIDTH), F32),
            jax.ShapeDtypeStruct((bsz, WINDOW, KV_WIDTH), F32),
        ],
        scratch_shapes=[carry, carry, carry, carry],
        compiler_params=pltpu.CompilerParams(
            dimension_semantics=("arbitrary", "arbitrary"), vmem_limit_bytes=VMEM_LIMIT_BYTES),
        name="mixer_prompt",
    )(sinks, x, mod, *tabs, w_in_b, wsp, bexp, sgu_g, sgu_b, w_o_b, g1, b1)


def _mixer_sample_kernel(x_ref, mod_ref, cos_ref, sin_ref, kc_ref, vc_ref, sinka_ref, sinkb_ref,
                         w_in_ref, wsx_ref, bexp_ref, sg_ref, sb_ref, w_o_ref, g1_ref, b1_ref,
                         y_ref, kwin_ref, vwin_ref, gv_ref, *, bb, t):
    x3 = x_ref[...]
    shift, scale, gate = mod_ref[:, 0:1, :], mod_ref[:, 1:2, :], mod_ref[:, 2:3, :]
    h = (_norm(x3) * (1.0 + scale) + shift).reshape(bb * t, D_MODEL).astype(BF16)
    z = jnp.dot(h, w_in_ref[...], preferred_element_type=F32)

    cos, sin = cos_ref[...], sin_ref[...]
    up, dn = _rotary_lane_signs()
    sup, sdn = sin * up, sin * dn
    qcols = [_rotary(z[:, c * LANES:(c + 1) * LANES], cos, sup, sdn).reshape(bb, t, LANES)
             for c in range(4)]
    k2 = _rotary(z[:, ATT_WIDTH:ATT_WIDTH + KV_WIDTH], cos, sup, sdn)
    v2 = z[:, ATT_WIDTH + KV_WIDTH:ATT_WIDTH + 2 * KV_WIDTH]
    k3, v3 = k2.reshape(bb, t, KV_WIDTH), v2.reshape(bb, t, KV_WIDTH)
    u3 = jax.nn.gelu(z[:, ATT_WIDTH + 2 * KV_WIDTH:ATT_WIDTH + 2 * KV_WIDTH + GM_WIDTH]
                     ).reshape(bb, t, GM_WIDTH)
    gv = _norm(jax.nn.gelu(z[:, ATT_WIDTH + 2 * KV_WIDTH + GM_WIDTH:])) * sg_ref[...] + sb_ref[...]
    gv3 = gv.reshape(bb, t, GM_WIDTH)
    gv_ref[...] = gv3

    kc = kc_ref[...]
    vc = vc_ref[...]
    wb = kc.shape[1]
    kwin_ref[:, 0:wb - t, :] = kc[:, t:, :]
    kwin_ref[:, wb - t:, :] = k3
    vwin_ref[:, 0:wb - t, :] = vc[:, t:, :]
    vwin_ref[:, wb - t:, :] = v3

    kcat = jnp.concatenate([kc, k3], axis=1)
    vcat = jnp.concatenate([vc, v3], axis=1)
    nq, nk = Q_PER_KV * t, wb + t
    tok = lax.broadcasted_iota(jnp.int32, (nq, nk), 0) % t
    col = lax.broadcasted_iota(jnp.int32, (nq, nk), 1)
    rel = tok + wb - col
    mask = ((rel >= 0) & (rel < WINDOW))[None]
    o = {}
    for heads, sink_ref, km, vm in (
            (HEADS_STRAIGHT, sinka_ref, kcat, vcat),
            (HEADS_SWAPPED, sinkb_ref, _swap_halves(kcat), _swap_halves(vcat))):
        s = jnp.einsum("bqd,bkd->bqk", _masked_heads(qcols, heads), km.astype(BF16),
                       preferred_element_type=F32)
        sink = sink_ref[:, 0:1][None]
        sm = jnp.where(mask, s, -jnp.inf)
        m = jnp.maximum(jnp.max(sm, axis=-1, keepdims=True), sink)
        e = jnp.exp(sm - m)
        inv = 1.0 / (jnp.sum(e, axis=-1, keepdims=True) + jnp.exp(sink - m))
        oh = jnp.einsum("bqk,bkd->bqd", e.astype(BF16), vm.astype(BF16),
                        preferred_element_type=F32) * inv
        for n, hd in enumerate(heads):
            o[hd] = oh[:, n * t:(n + 1) * t, :]
    lo = _lo_half()[None]
    att = jnp.concatenate([jnp.where(lo, o[2 * c], o[2 * c + 1]) for c in range(4)], axis=-1)

    sv = jnp.zeros((bb, t, GM_WIDTH), F32)
    for si in range(t):
        sv = sv + wsx_ref[si][None] * gv3[:, si:si + 1, :]
    sgu = u3 * (sv + bexp_ref[...][None])

    mixcat = jnp.concatenate([att, sgu], axis=-1).reshape(bb * t, D_MODEL).astype(BF16)
    mix = jnp.dot(mixcat, w_o_ref[...], preferred_element_type=F32).reshape(bb, t, D_MODEL)
    y_ref[...] = _norm(ALPHA * x3 + gate * mix) * g1_ref[...] + b1_ref[...]


def _mixer_sample(x, mod, tabs, kc, vc, sink_rows, w_in_b, wsx, bexp_t, sgu_g, sgu_b, w_o_b,
                  g1, b1, bb):
    n, t, _ = x.shape
    wb = kc.shape[1]
    row3 = lambda r, c: pl.BlockSpec((bb, r, c), lambda b: (b, 0, 0))
    consts = (*tabs, )
    return pl.pallas_call(
        functools.partial(_mixer_sample_kernel, bb=bb, t=t),
        grid=(n // bb,),
        in_specs=[
            row3(t, D_MODEL), row3(6, D_MODEL),
            *[_const_spec(a.shape) for a in consts],
            row3(wb, KV_WIDTH), row3(wb, KV_WIDTH),
            *[_const_spec(a.shape) for a in sink_rows],
            _const_spec(w_in_b.shape), _const_spec(wsx.shape), _const_spec(bexp_t.shape),
            _const_spec(sgu_g.shape), _const_spec(sgu_b.shape), _const_spec(w_o_b.shape),
            _const_spec(g1.shape), _const_spec(b1.shape),
        ],
        out_specs=[row3(t, D_MODEL), row3(wb, KV_WIDTH), row3(wb, KV_WIDTH), row3(t, GM_WIDTH)],
        out_shape=[
            jax.ShapeDtypeStruct((n, t, D_MODEL), F32),
            jax.ShapeDtypeStruct((n, wb, KV_WIDTH), F32),
            jax.ShapeDtypeStruct((n, wb, KV_WIDTH), F32),
            jax.ShapeDtypeStruct((n, t, GM_WIDTH), F32),
        ],
        compiler_params=pltpu.CompilerParams(
            dimension_semantics=("arbitrary",), vmem_limit_bytes=VMEM_LIMIT_BYTES),
        name="mixer_sample",
    )(x, mod, *consts, kc, vc, *sink_rows, w_in_b, wsx, bexp_t, sgu_g, sgu_b, w_o_b, g1, b1)


def _ffn_kernel(x_ref, mod_ref, wg_ref, wu_ref, wd_ref, g2_ref, b2_ref, y_ref):
    x = x_ref[...]
    if x.ndim == 3:
        shift, scale, gate = mod_ref[:, 3:4, :], mod_ref[:, 4:5, :], mod_ref[:, 5:6, :]
    else:
        shift, scale, gate = mod_ref[3:4, :], mod_ref[4:5, :], mod_ref[5:6, :]
    h = (_norm(x) * (1.0 + scale) + shift).reshape(-1, D_MODEL).astype(BF16)
    g = jnp.dot(h, wg_ref[...], preferred_element_type=F32)
    up = jnp.dot(h, wu_ref[...], preferred_element_type=F32)
    a = (g * jax.nn.sigmoid(g) * up).astype(BF16)
    f = jnp.dot(a, wd_ref[...], preferred_element_type=F32).reshape(x.shape)
    y_ref[...] = _norm(ALPHA * x + gate * f) * g2_ref[...] + b2_ref[...]


def _ffn(x, mod, wg, wu, wd, g2, b2, *, rows):
    gsz, r, _ = x.shape
    if r >= rows:
        grid = (gsz, r // rows)
        x_spec = pl.BlockSpec((None, rows, D_MODEL), lambda b, i: (b, i, 0))
        m_spec = pl.BlockSpec((None, 6, D_MODEL), lambda b, i: (b, 0, 0))
    else:
        bb = rows // r
        grid = (gsz // bb, 1)
        x_spec = pl.BlockSpec((bb, r, D_MODEL), lambda b, i: (b, 0, 0))
        m_spec = pl.BlockSpec((bb, 6, D_MODEL), lambda b, i: (b, 0, 0))
    return pl.pallas_call(
        _ffn_kernel,
        grid=grid,
        in_specs=[x_spec, m_spec, _const_spec(wg.shape), _const_spec(wu.shape),
                  _const_spec(wd.shape), _const_spec(g2.shape), _const_spec(b2.shape)],
        out_specs=x_spec,
        out_shape=jax.ShapeDtypeStruct(x.shape, F32),
        compiler_params=pltpu.CompilerParams(
            dimension_semantics=("arbitrary", "arbitrary"), vmem_limit_bytes=VMEM_LIMIT_BYTES),
        name="ffn",
    )(x, mod, wg, wu, wd, g2, b2)


def _lane_freqs():
    half = ROT_DIM // 2
    inv_freq = jnp.power(jnp.float32(ROPE_THETA), -jnp.arange(half, dtype=F32) * 2.0 / ROT_DIM)
    per_head = jnp.concatenate([inv_freq, inv_freq, jnp.zeros((HEAD_DIM - ROT_DIM,), F32)])
    return jnp.concatenate([per_head, per_head])


def _angle_tables(pos):
    ang = pos.astype(F32)[:, None] * _lane_freqs()[None, :]
    return jnp.cos(ang), jnp.sin(ang)


def kernel(x_prompt, x_sample, cache_k_win, cache_v_win, c_prompt, c_sample, w_ada, b_ada, w_in, attn_sinks, sgu_ln_g, sgu_ln_b, w_s, b_s, w_o, ln1_g, ln1_b, w_gate, w_up, w_down, ln2_g, ln2_b):
    bsz, seq, _ = x_prompt.shape
    nsm, t, _ = x_sample.shape
    wb = cache_k_win.shape[2]
    tq, bb = 256, 32
    tabs_p = (*_angle_tables(jnp.arange(tq, dtype=jnp.int32)),
              *_angle_tables(jnp.arange(seq // tq, dtype=jnp.int32) * tq))
    tabs_s = tuple(jnp.tile(a, (bb, 1))
                   for a in _angle_tables(PAST_LEN + jnp.arange(t, dtype=jnp.int32)))

    yp, ys = x_prompt, x_sample
    kwp, vwp, kws, vws, sgv = [], [], [], [], []
    for l in range(DEPTH):
        w_in_b, w_o_b, wg, wu, wd = _prep_weights(w_in, w_o, w_gate, w_up, w_down, l)
        wtril = jnp.tril(w_s[l])
        wsp = jnp.concatenate([wtril[0::2], wtril[1::2]], axis=2).astype(BF16)
        bexp = jnp.repeat(b_s[l].T, HEAD_DIM, axis=1)
        wsx = jnp.repeat(jnp.transpose(wtril[:, :t, :t], (2, 1, 0)), HEAD_DIM, axis=-1)
        sinks = attn_sinks[l]
        sink_rows = tuple(
            jnp.broadcast_to(jnp.repeat(sinks[jnp.asarray(hs)], t)[:, None], (Q_PER_KV * t, LANES))
            for hs in (HEADS_STRAIGHT, HEADS_SWAPPED))
        sg, sb = sgu_ln_g[l][None, :], sgu_ln_b[l][None, :]
        g1, b1 = ln1_g[l][None, :], ln1_b[l][None, :]
        g2, b2 = ln2_g[l][None, :], ln2_b[l][None, :]

        nrow = bsz + nsm
        pad = (-nrow) % 8
        c_all = jnp.concatenate([c_sample, c_prompt, jnp.zeros((pad, D_MODEL), F32)], axis=0)
        mod = _ada(c_all, w_ada, b_ada, l)
        mod_s = mod[:nsm].reshape(nsm, 6, D_MODEL)
        mod_p = mod[nsm:nrow].reshape(bsz, 6, D_MODEL)

        yp, kw, vw = _mixer_prompt(yp, mod_p, sinks, tabs_p, w_in_b, wsp, bexp, sg, sb, w_o_b,
                                   g1, b1, tq)
        yp = _ffn(yp, mod_p, wg, wu, wd, g2, b2, rows=512)
        kwp.append(kw.reshape(bsz, WINDOW, KV_HEADS, HEAD_DIM))
        vwp.append(vw.reshape(bsz, WINDOW, KV_HEADS, HEAD_DIM))

        kc = cache_k_win[l].reshape(nsm, wb, KV_WIDTH)
        vc = cache_v_win[l].reshape(nsm, wb, KV_WIDTH)
        ys, kn, vn, gvs = _mixer_sample(ys, mod_s, tabs_s, kc, vc, sink_rows, w_in_b, wsx,
                                        bexp[:t], sg, sb, w_o_b, g1, b1, bb)
        ys = _ffn(ys, mod_s, wg, wu, wd, g2, b2, rows=512)
        kws.append(kn.reshape(nsm, wb, KV_HEADS, HEAD_DIM))
        vws.append(vn.reshape(nsm, wb, KV_HEADS, HEAD_DIM))
        sgv.append(gvs)
    return (yp, ys, jnp.stack(kwp, axis=0), jnp.stack(vwp, axis=0), jnp.stack(kws, axis=0),
            jnp.stack(vws, axis=0), jnp.stack(sgv, axis=0))
```

```python
import functools

import jax
import jax.numpy as jnp
from jax import lax
from jax.experimental import pallas as pl
from jax.experimental.pallas import tpu as pltpu

D_MODEL = 1024
HEAD_DIM = 64
ATT_HEADS = 8
KV_HEADS = 2
Q_PER_KV = ATT_HEADS // KV_HEADS
ATT_WIDTH = ATT_HEADS * HEAD_DIM
KV_WIDTH = KV_HEADS * HEAD_DIM
GM_HEADS = 8
GM_WIDTH = GM_HEADS * HEAD_DIM
IN_WIDTH = ATT_WIDTH + 2 * KV_WIDTH + 2 * GM_WIDTH
D_FF = 2816
WINDOW = 128
CHUNK = 128
PAST_LEN = 16384
ROPE_THETA = 500000.0
ROT_DIM = HEAD_DIM // 4
DEPTH = 1
ALPHA = (2 * DEPTH) ** 0.25
LN_EPS = 1e-5
ATT_SCALE = HEAD_DIM ** -0.5

LANES = 128
VMEM_LIMIT_BYTES = 56 * 1024 * 1024

BF16 = jnp.bfloat16
F32 = jnp.float32

HEADS_STRAIGHT = tuple(h for h in range(ATT_HEADS) if h % 2 == h // Q_PER_KV)
HEADS_SWAPPED = tuple(h for h in range(ATT_HEADS) if h % 2 != h // Q_PER_KV)


def _norm(x):
    mu = jnp.mean(x, axis=-1, keepdims=True)
    xc = x - mu
    var = jnp.mean(xc * xc, axis=-1, keepdims=True)
    return xc * lax.rsqrt(var + LN_EPS)


def _lo_half():
    return lax.broadcasted_iota(jnp.int32, (1, LANES), 1) < HEAD_DIM


def _rotary_lane_signs():
    half = ROT_DIM // 2
    l = lax.broadcasted_iota(jnp.int32, (1, LANES), 1) % HEAD_DIM
    up = jnp.where(l < half, -1.0, 0.0)
    dn = jnp.where((l >= half) & (l < ROT_DIM), 1.0, 0.0)
    return up, dn


def _rotary(x, cos, sin_up, sin_dn):
    half = ROT_DIM // 2
    return (x * cos + pltpu.roll(x, LANES - half, 1) * sin_up
            + pltpu.roll(x, half, 1) * sin_dn)


def _swap_halves(x):
    return pltpu.roll(x, HEAD_DIM, x.ndim - 1)


def _masked_heads(qcols, heads):
    lo = _lo_half()
    parts = []
    for h in heads:
        qc = qcols[h // 2]
        parts.append(jnp.where(lo, qc, 0.0) if h % 2 == 0 else jnp.where(lo, 0.0, qc))
    return jnp.concatenate(parts, axis=-2).astype(BF16)


def _prep_kernel(win_ref, wo_ref, wg_ref, wu_ref, wd_ref, oin_ref, oo_ref, og_ref, ou_ref, od_ref):
    col = lax.broadcasted_iota(jnp.int32, (1, IN_WIDTH), 1)
    qscale = jnp.where(col < ATT_WIDTH, ATT_SCALE, 1.0)
    oin_ref[...] = (win_ref[...] * qscale).astype(BF16)
    oo_ref[...] = wo_ref[...].astype(BF16)
    og_ref[...] = wg_ref[...].astype(BF16)
    ou_ref[...] = wu_ref[...].astype(BF16)
    od_ref[...] = wd_ref[...].astype(BF16)


def _prep_weights(w_in, w_o, w_gate, w_up, w_down, layer, steps=8):
    def spec(w):
        _, r, c = w.shape
        return pl.BlockSpec((None, r // steps, c), lambda j: (layer, j, 0))

    def ospec(w):
        _, r, c = w.shape
        return pl.BlockSpec((r // steps, c), lambda j: (j, 0))

    ws = (w_in, w_o, w_gate, w_up, w_down)
    return pl.pallas_call(
        _prep_kernel,
        grid=(steps,),
        in_specs=[spec(w) for w in ws],
        out_specs=[ospec(w) for w in ws],
        out_shape=[jax.ShapeDtypeStruct(w.shape[1:], BF16) for w in ws],
        compiler_params=pltpu.CompilerParams(
            dimension_semantics=("arbitrary",), vmem_limit_bytes=VMEM_LIMIT_BYTES),
        name="prep_weights",
    )(*ws)


def _ada_kernel(c_ref, w_ref, b_ref, o_ref):
    c = c_ref[...]
    a = (c * jax.nn.sigmoid(c)).astype(BF16)
    o_ref[...] = jnp.dot(a, w_ref[...].astype(BF16), preferred_element_type=F32) + b_ref[...]


def _ada(c_all, w_ada, b_ada, layer, tn=1024):
    rows = c_all.shape[0]
    n = w_ada.shape[2]
    return pl.pallas_call(
        _ada_kernel,
        grid=(n // tn,),
        in_specs=[
            pl.BlockSpec((rows, D_MODEL), lambda j: (0, 0)),
            pl.BlockSpec((None, D_MODEL, tn), lambda j: (layer, 0, j)),
            pl.BlockSpec((1, tn), lambda j: (layer, j)),
        ],
        out_specs=pl.BlockSpec((rows, tn), lambda j: (0, j)),
        out_shape=jax.ShapeDtypeStruct((rows, n), F32),
        compiler_params=pltpu.CompilerParams(dimension_semantics=("arbitrary",)),
        name="ada",
    )(c_all, w_ada, b_ada)


def _softmax_pv(s, mask, sinks_ref, heads, vmat):
    es, invs = [], []
    for n, h in enumerate(heads):
        sink = sinks_ref[h]
        sh = jnp.where(mask, s[n * WINDOW:(n + 1) * WINDOW], -jnp.inf)
        m = jnp.maximum(jnp.max(sh, axis=-1, keepdims=True), sink)
        e = jnp.exp(sh - m)
        den = jnp.sum(e, axis=-1, keepdims=True) + jnp.exp(sink - m)
        es.append(e.astype(BF16))
        invs.append(1.0 / den)
    o = jnp.dot(jnp.concatenate(es, axis=0), vmat, preferred_element_type=F32)
    return {h: o[n * WINDOW:(n + 1) * WINDOW] * invs[n] for n, h in enumerate(heads)}


def _prompt_attention(qcols, kcat, kcat_sw, vcat, vcat_sw, sinks_ref, first_key):
    row = lax.broadcasted_iota(jnp.int32, (WINDOW, 2 * WINDOW), 0)
    col = lax.broadcasted_iota(jnp.int32, (WINDOW, 2 * WINDOW), 1)
    mask = (col > row) & (col <= row + WINDOW) & (col >= first_key)
    dn = (((1,), (1,)), ((), ()))
    s_a = lax.dot_general(_masked_heads(qcols, HEADS_STRAIGHT), kcat, dn, preferred_element_type=F32)
    s_b = lax.dot_general(_masked_heads(qcols, HEADS_SWAPPED), kcat_sw, dn, preferred_element_type=F32)
    o = _softmax_pv(s_a, mask, sinks_ref, HEADS_STRAIGHT, vcat)
    o.update(_softmax_pv(s_b, mask, sinks_ref, HEADS_SWAPPED, vcat_sw))
    lo = _lo_half()
    return jnp.concatenate([jnp.where(lo, o[2 * c], o[2 * c + 1]) for c in range(4)], axis=1)


def _prompt_sgu(uj, gvj, wsp_ref, bexp):
    lo = _lo_half()
    svs = []
    for p in range(4):
        g = gvj[:, p * LANES:(p + 1) * LANES]
        rhs = jnp.concatenate([jnp.where(lo, g, 0.0), jnp.where(lo, 0.0, g)], axis=0).astype(BF16)
        svs.append(jnp.dot(wsp_ref[p], rhs, preferred_element_type=F32))
    sv = jnp.concatenate(svs, axis=1)
    return uj * (sv + bexp)


def _mixer_math(x, shift, scale, gate, i, prev_refs, sinks_ref, cr_ref, sr_ref, cb_ref, sb_ref,
                w_in_ref, wsp_ref, bexp_ref, sg_ref, sb2_ref, w_o_ref, g1_ref, b1_ref, tq):
    h = (_norm(x) * (1.0 + scale) + shift).astype(BF16)
    z = jnp.dot(h, w_in_ref[...], preferred_element_type=F32)

    cb, sb = cb_ref[pl.ds(i, 1), :], sb_ref[pl.ds(i, 1), :]
    cr, sr = cr_ref[...], sr_ref[...]
    cos = cb * cr - sb * sr
    sin = sb * cr + cb * sr
    up, dn = _rotary_lane_signs()
    sup, sdn = sin * up, sin * dn

    q = [_rotary(z[:, c * LANES:(c + 1) * LANES], cos, sup, sdn) for c in range(4)]
    k = _rotary(z[:, ATT_WIDTH:ATT_WIDTH + KV_WIDTH], cos, sup, sdn)
    v = z[:, ATT_WIDTH + KV_WIDTH:ATT_WIDTH + 2 * KV_WIDTH]
    u = jax.nn.gelu(z[:, ATT_WIDTH + 2 * KV_WIDTH:ATT_WIDTH + 2 * KV_WIDTH + GM_WIDTH])
    gv = _norm(jax.nn.gelu(z[:, ATT_WIDTH + 2 * KV_WIDTH + GM_WIDTH:])) * sg_ref[...] + sb2_ref[...]

    forms = (k.astype(BF16), v.astype(BF16), _swap_halves(k).astype(BF16), _swap_halves(v).astype(BF16))
    kb, vb, kb_sw, vb_sw = forms
    bexp = bexp_ref[...]
    mixes = []
    for j in range(tq // WINDOW):
        sl = slice(j * WINDOW, (j + 1) * WINDOW)
        if j == 0:
            kcat, vcat, kcat_sw, vcat_sw = (
                jnp.concatenate([p[...], c[sl]], axis=0) for p, c in zip(prev_refs, forms))
            first_key = jnp.where(i > 0, 0, WINDOW)
        else:
            sl2 = slice((j - 1) * WINDOW, (j + 1) * WINDOW)
            kcat, vcat, kcat_sw, vcat_sw = kb[sl2], vb[sl2], kb_sw[sl2], vb_sw[sl2]
            first_key = 0
        att = _prompt_attention([qc[sl] for qc in q], kcat, kcat_sw, vcat, vcat_sw, sinks_ref,
                                first_key)
        sgu = _prompt_sgu(u[sl], gv[sl], wsp_ref, bexp)
        mixes.append(jnp.concatenate([att, sgu], axis=1).astype(BF16))
    mixcat = jnp.concatenate(mixes, axis=0)
    mix = jnp.dot(mixcat, w_o_ref[...], preferred_element_type=F32)
    y = _norm(ALPHA * x + gate * mix) * g1_ref[...] + b1_ref[...]
    return y, k, v, forms


def _ffn_math(x, shift, scale, gate, wg_ref, wu_ref, wd_ref, g2_ref, b2_ref):
    h = (_norm(x) * (1.0 + scale) + shift).reshape(-1, D_MODEL).astype(BF16)
    g = jnp.dot(h, wg_ref[...], preferred_element_type=F32)
    up = jnp.dot(h, wu_ref[...], preferred_element_type=F32)
    a = (g * jax.nn.sigmoid(g) * up).astype(BF16)
    f = jnp.dot(a, wd_ref[...], preferred_element_type=F32).reshape(x.shape)
    return _norm(ALPHA * x + gate * f) * g2_ref[...] + b2_ref[...]


def _prompt_kernel(sinks_ref, x_ref, mod_ref, cr_ref, sr_ref, cb_ref, sb_ref, w_in_ref, wsp_ref,
                   bexp_ref, sg_ref, sb2_ref, w_o_ref, g1_ref, b1_ref, wg_ref, wu_ref, wd_ref,
                   g2_ref, b2_ref, y_ref, kwin_ref, vwin_ref,
                   y1_ref, kprev_ref, vprev_ref, kprev_sw_ref, vprev_sw_ref, *, tq, nb, nblocks):
    s = pl.program_id(0)
    sm = jnp.minimum(s, nblocks - 1)
    b, i = sm // nb, sm % nb
    bf = jnp.maximum(s - 1, 0) // nb
    prev_refs = (kprev_ref, vprev_ref, kprev_sw_ref, vprev_sw_ref)

    @pl.when(s == 0)
    def _():
        for r in prev_refs:
            r[...] = jnp.zeros_like(r)
        y1_ref[1] = jnp.zeros(y1_ref.shape[1:], F32)

    modf = mod_ref[bf]
    y_ref[...] = _ffn_math(y1_ref[(s + 1) % 2], modf[3:4], modf[4:5], modf[5:6],
                           wg_ref, wu_ref, wd_ref, g2_ref, b2_ref)

    modm = mod_ref[b]
    y1, k, v, forms = _mixer_math(
        x_ref[...], modm[0:1], modm[1:2], modm[2:3], i, prev_refs, sinks_ref, cr_ref, sr_ref,
        cb_ref, sb_ref, w_in_ref, wsp_ref, bexp_ref, sg_ref, sb2_ref, w_o_ref, g1_ref, b1_ref, tq)
    y1_ref[s % 2] = y1
    for r, f in zip(prev_refs, forms):
        r[...] = f[tq - WINDOW:]
    kwin_ref[...] = k[tq - WINDOW:]
    vwin_ref[...] = v[tq - WINDOW:]


def _const_spec(shape):
    nd = len(shape)
    return pl.BlockSpec(shape, lambda *_: (0,) * nd)


def _prompt_layer(x, mod, sinks, tabs, w_in_b, wsp, bexp, sgu_g, sgu_b, w_o_b, g1, b1,
                  wg, wu, wd, g2, b2, tq):
    bsz, seq, _ = x.shape
    nb = seq // tq
    nblocks = bsz * nb
    carry = pltpu.VMEM((WINDOW, KV_WIDTH), BF16)

    def mixer_block(s):
        sm = jnp.minimum(s, nblocks - 1)
        return sm // nb, sm % nb, 0

    def ffn_block(s):
        sf = jnp.maximum(s - 1, 0)
        return sf // nb, sf % nb, 0

    consts = (*tabs, w_in_b, wsp, bexp, sgu_g, sgu_b, w_o_b, g1, b1, wg, wu, wd, g2, b2)
    return pl.pallas_call(
        functools.partial(_prompt_kernel, tq=tq, nb=nb, nblocks=nblocks),
        grid=(nblocks + 1,),
        in_specs=[
            pl.BlockSpec(memory_space=pltpu.SMEM),
            pl.BlockSpec((None, tq, D_MODEL), mixer_block),
            _const_spec(mod.shape),
            *[_const_spec(c.shape) for c in consts],
        ],
        out_specs=[
            pl.BlockSpec((None, tq, D_MODEL), ffn_block),
            pl.BlockSpec((None, WINDOW, KV_WIDTH), lambda s: (jnp.minimum(s, nblocks - 1) // nb, 0, 0)),
            pl.BlockSpec((None, WINDOW, KV_WIDTH), lambda s: (jnp.minimum(s, nblocks - 1) // nb, 0, 0)),
        ],
        out_shape=[
            jax.ShapeDtypeStruct((bsz, seq, D_MODEL), F32),
            jax.ShapeDtypeStruct((bsz, WINDOW, KV_WIDTH), F32),
            jax.ShapeDtypeStruct((bsz, WINDOW, KV_WIDTH), F32),
        ],
        scratch_shapes=[pltpu.VMEM((2, tq, D_MODEL), F32), carry, carry, carry, carry],
        compiler_params=pltpu.CompilerParams(
            dimension_semantics=("arbitrary",), vmem_limit_bytes=VMEM_LIMIT_BYTES),
        name="prompt_layer",
    )(sinks, x, mod, *consts)


def _mixer_sample_kernel(x_ref, mod_ref, cos_ref, sin_ref, kc_ref, vc_ref, sinka_ref, sinkb_ref,
                         w_in_ref, wsx_ref, bexp_ref, sg_ref, sb_ref, w_o_ref, g1_ref, b1_ref,
                         y_ref, kwin_ref, vwin_ref, gv_ref, *, bb, t):
    x3 = x_ref[...]
    shift, scale, gate = mod_ref[:, 0:1, :], mod_ref[:, 1:2, :], mod_ref[:, 2:3, :]
    h = (_norm(x3) * (1.0 + scale) + shift).reshape(bb * t, D_MODEL).astype(BF16)
    z = jnp.dot(h, w_in_ref[...], preferred_element_type=F32)

    cos, sin = cos_ref[...], sin_ref[...]
    up, dn = _rotary_lane_signs()
    sup, sdn = sin * up, sin * dn
    qcols = [_rotary(z[:, c * LANES:(c + 1) * LANES], cos, sup, sdn).reshape(bb, t, LANES)
             for c in range(4)]
    k2 = _rotary(z[:, ATT_WIDTH:ATT_WIDTH + KV_WIDTH], cos, sup, sdn)
    v2 = z[:, ATT_WIDTH + KV_WIDTH:ATT_WIDTH + 2 * KV_WIDTH]
    k3, v3 = k2.reshape(bb, t, KV_WIDTH), v2.reshape(bb, t, KV_WIDTH)
    u3 = jax.nn.gelu(z[:, ATT_WIDTH + 2 * KV_WIDTH:ATT_WIDTH + 2 * KV_WIDTH + GM_WIDTH]
                     ).reshape(bb, t, GM_WIDTH)
    gv = _norm(jax.nn.gelu(z[:, ATT_WIDTH + 2 * KV_WIDTH + GM_WIDTH:])) * sg_ref[...] + sb_ref[...]
    gv3 = gv.reshape(bb, t, GM_WIDTH)
    gv_ref[...] = gv3

    kc = kc_ref[...]
    vc = vc_ref[...]
    wb = kc.shape[1]
    kwin_ref[:, 0:wb - t, :] = kc[:, t:, :]
    kwin_ref[:, wb - t:, :] = k3
    vwin_ref[:, 0:wb - t, :] = vc[:, t:, :]
    vwin_ref[:, wb - t:, :] = v3

    kcat = jnp.concatenate([kc, k3], axis=1)
    vcat = jnp.concatenate([vc, v3], axis=1)
    nq, nk = Q_PER_KV * t, wb + t
    tok = lax.broadcasted_iota(jnp.int32, (nq, nk), 0) % t
    col = lax.broadcasted_iota(jnp.int32, (nq, nk), 1)
    rel = tok + wb - col
    mask = ((rel >= 0) & (rel < WINDOW))[None]
    o = {}
    for heads, sink_ref, km, vm in (
            (HEADS_STRAIGHT, sinka_ref, kcat, vcat),
            (HEADS_SWAPPED, sinkb_ref, _swap_halves(kcat), _swap_halves(vcat))):
        s = jnp.einsum("bqd,bkd->bqk", _masked_heads(qcols, heads), km.astype(BF16),
                       preferred_element_type=F32)
        sink = sink_ref[:, 0:1][None]
        sm = jnp.where(mask, s, -jnp.inf)
        m = jnp.maximum(jnp.max(sm, axis=-1, keepdims=True), sink)
        e = jnp.exp(sm - m)
        inv = 1.0 / (jnp.sum(e, axis=-1, keepdims=True) + jnp.exp(sink - m))
        oh = jnp.einsum("bqk,bkd->bqd", e.astype(BF16), vm.astype(BF16),
                        preferred_element_type=F32) * inv
        for n, hd in enumerate(heads):
            o[hd] = oh[:, n * t:(n + 1) * t, :]
    lo = _lo_half()[None]
    att = jnp.concatenate([jnp.where(lo, o[2 * c], o[2 * c + 1]) for c in range(4)], axis=-1)

    sv = jnp.zeros((bb, t, GM_WIDTH), F32)
    for si in range(t):
        sv = sv + wsx_ref[si][None] * gv3[:, si:si + 1, :]
    sgu = u3 * (sv + bexp_ref[...][None])

    mixcat = jnp.concatenate([att, sgu], axis=-1).reshape(bb * t, D_MODEL).astype(BF16)
    mix = jnp.dot(mixcat, w_o_ref[...], preferred_element_type=F32).reshape(bb, t, D_MODEL)
    y_ref[...] = _norm(ALPHA * x3 + gate * mix) * g1_ref[...] + b1_ref[...]


def _mixer_sample(x, mod, tabs, kc, vc, sink_rows, w_in_b, wsx, bexp_t, sgu_g, sgu_b, w_o_b,
                  g1, b1, bb):
    n, t, _ = x.shape
    wb = kc.shape[1]
    row3 = lambda r, c: pl.BlockSpec((bb, r, c), lambda b: (b, 0, 0))
    consts = (*tabs, )
    return pl.pallas_call(
        functools.partial(_mixer_sample_kernel, bb=bb, t=t),
        grid=(n // bb,),
        in_specs=[
            row3(t, D_MODEL), row3(6, D_MODEL),
            *[_const_spec(a.shape) for a in consts],
            row3(wb, KV_WIDTH), row3(wb, KV_WIDTH),
            *[_const_spec(a.shape) for a in sink_rows],
            _const_spec(w_in_b.shape), _const_spec(wsx.shape), _const_spec(bexp_t.shape),
            _const_spec(sgu_g.shape), _const_spec(sgu_b.shape), _const_spec(w_o_b.shape),
            _const_spec(g1.shape), _const_spec(b1.shape),
        ],
        out_specs=[row3(t, D_MODEL), row3(wb, KV_WIDTH), row3(wb, KV_WIDTH), row3(t, GM_WIDTH)],
        out_shape=[
            jax.ShapeDtypeStruct((n, t, D_MODEL), F32),
            jax.ShapeDtypeStruct((n, wb, KV_WIDTH), F32),
            jax.ShapeDtypeStruct((n, wb, KV_WIDTH), F32),
            jax.ShapeDtypeStruct((n, t, GM_WIDTH), F32),
        ],
        compiler_params=pltpu.CompilerParams(
            dimension_semantics=("arbitrary",), vmem_limit_bytes=VMEM_LIMIT_BYTES),
        name="mixer_sample",
    )(x, mod, *consts, kc, vc, *sink_rows, w_in_b, wsx, bexp_t, sgu_g, sgu_b, w_o_b, g1, b1)


def _ffn_kernel(x_ref, mod_ref, wg_ref, wu_ref, wd_ref, g2_ref, b2_ref, y_ref):
    shift, scale, gate = mod_ref[:, 3:4, :], mod_ref[:, 4:5, :], mod_ref[:, 5:6, :]
    y_ref[...] = _ffn_math(x_ref[...], shift, scale, gate, wg_ref, wu_ref, wd_ref, g2_ref, b2_ref)


def _ffn(x, mod, wg, wu, wd, g2, b2, *, rows):
    gsz, r, _ = x.shape
    if r >= rows:
        grid = (gsz, r // rows)
        x_spec = pl.BlockSpec((None, rows, D_MODEL), lambda b, i: (b, i, 0))
        m_spec = pl.BlockSpec((None, 6, D_MODEL), lambda b, i: (b, 0, 0))
    else:
        bb = rows // r
        grid = (gsz // bb, 1)
        x_spec = pl.BlockSpec((bb, r, D_MODEL), lambda b, i: (b, 0, 0))
        m_spec = pl.BlockSpec((bb, 6, D_MODEL), lambda b, i: (b, 0, 0))
    return pl.pallas_call(
        _ffn_kernel,
        grid=grid,
        in_specs=[x_spec, m_spec, _const_spec(wg.shape), _const_spec(wu.shape),
                  _const_spec(wd.shape), _const_spec(g2.shape), _const_spec(b2.shape)],
        out_specs=x_spec,
        out_shape=jax.ShapeDtypeStruct(x.shape, F32),
        compiler_params=pltpu.CompilerParams(
            dimension_semantics=("arbitrary", "arbitrary"), vmem_limit_bytes=VMEM_LIMIT_BYTES),
        name="ffn",
    )(x, mod, wg, wu, wd, g2, b2)


def _lane_freqs():
    half = ROT_DIM // 2
    inv_freq = jnp.power(jnp.float32(ROPE_THETA), -jnp.arange(half, dtype=F32) * 2.0 / ROT_DIM)
    per_head = jnp.concatenate([inv_freq, inv_freq, jnp.zeros((HEAD_DIM - ROT_DIM,), F32)])
    return jnp.concatenate([per_head, per_head])


def _angle_tables(pos):
    ang = pos.astype(F32)[:, None] * _lane_freqs()[None, :]
    return jnp.cos(ang), jnp.sin(ang)


def kernel(x_prompt, x_sample, cache_k_win, cache_v_win, c_prompt, c_sample, w_ada, b_ada, w_in, attn_sinks, sgu_ln_g, sgu_ln_b, w_s, b_s, w_o, ln1_g, ln1_b, w_gate, w_up, w_down, ln2_g, ln2_b):
    bsz, seq, _ = x_prompt.shape
    nsm, t, _ = x_sample.shape
    wb = cache_k_win.shape[2]
    tq, bb = 256, 32
    tabs_p = (*_angle_tables(jnp.arange(tq, dtype=jnp.int32)),
              *_angle_tables(jnp.arange(seq // tq, dtype=jnp.int32) * tq))
    tabs_s = tuple(jnp.tile(a, (bb, 1))
                   for a in _angle_tables(PAST_LEN + jnp.arange(t, dtype=jnp.int32)))

    yp, ys = x_prompt, x_sample
    kwp, vwp, kws, vws, sgv = [], [], [], [], []
    for l in range(DEPTH):
        w_in_b, w_o_b, wg, wu, wd = _prep_weights(w_in, w_o, w_gate, w_up, w_down, l)
        wtril = jnp.tril(w_s[l])
        wsp = jnp.concatenate([wtril[0::2], wtril[1::2]], axis=2).astype(BF16)
        bexp = jnp.repeat(b_s[l].T, HEAD_DIM, axis=1)
        wsx = jnp.repeat(jnp.transpose(wtril[:, :t, :t], (2, 1, 0)), HEAD_DIM, axis=-1)
        sinks = attn_sinks[l]
        sink_rows = tuple(
            jnp.broadcast_to(jnp.repeat(sinks[jnp.asarray(hs)], t)[:, None], (Q_PER_KV * t, LANES))
            for hs in (HEADS_STRAIGHT, HEADS_SWAPPED))
        sg, sb = sgu_ln_g[l][None, :], sgu_ln_b[l][None, :]
        g1, b1 = ln1_g[l][None, :], ln1_b[l][None, :]
        g2, b2 = ln2_g[l][None, :], ln2_b[l][None, :]

        nrow = bsz + nsm
        pad = (-nrow) % 8
        c_all = jnp.concatenate([c_sample, c_prompt, jnp.zeros((pad, D_MODEL), F32)], axis=0)
        mod = _ada(c_all, w_ada, b_ada, l)
        mod_s = mod[:nsm].reshape(nsm, 6, D_MODEL)
        mod_p = mod[nsm:nrow].reshape(bsz, 6, D_MODEL)

        yp, kw, vw = _prompt_layer(yp, mod_p, sinks, tabs_p, w_in_b, wsp, bexp, sg, sb, w_o_b,
                                   g1, b1, wg, wu, wd, g2, b2, tq)
        kwp.append(kw.reshape(bsz, WINDOW, KV_HEADS, HEAD_DIM))
        vwp.append(vw.reshape(bsz, WINDOW, KV_HEADS, HEAD_DIM))

        kc = cache_k_win[l].reshape(nsm, wb, KV_WIDTH)
        vc = cache_v_win[l].reshape(nsm, wb, KV_WIDTH)
        ys, kn, vn, gvs = _mixer_sample(ys, mod_s, tabs_s, kc, vc, sink_rows, w_in_b, wsx,
                                        bexp[:t], sg, sb, w_o_b, g1, b1, bb)
        ys = _ffn(ys, mod_s, wg, wu, wd, g2, b2, rows=512)
        kws.append(kn.reshape(nsm, wb, KV_HEADS, HEAD_DIM))
        vws.append(vn.reshape(nsm, wb, KV_HEADS, HEAD_DIM))
        sgv.append(gvs)
    return (yp, ys, jnp.stack(kwp, axis=0), jnp.stack(vwp, axis=0), jnp.stack(kws, axis=0),
            jnp.stack(vws, axis=0), jnp.stack(sgv, axis=0))
```

```python
import functools

import jax
import jax.numpy as jnp
from jax import lax
from jax.experimental import pallas as pl
from jax.experimental.pallas import tpu as pltpu

D_MODEL = 1024
HEAD_DIM = 64
ATT_HEADS = 8
KV_HEADS = 2
Q_PER_KV = ATT_HEADS // KV_HEADS
ATT_WIDTH = ATT_HEADS * HEAD_DIM
KV_WIDTH = KV_HEADS * HEAD_DIM
GM_HEADS = 8
GM_WIDTH = GM_HEADS * HEAD_DIM
IN_WIDTH = ATT_WIDTH + 2 * KV_WIDTH + 2 * GM_WIDTH
D_FF = 2816
WINDOW = 128
CHUNK = 128
PAST_LEN = 16384
ROPE_THETA = 500000.0
ROT_DIM = HEAD_DIM // 4
DEPTH = 1
ALPHA = (2 * DEPTH) ** 0.25
LN_EPS = 1e-5
ATT_SCALE = HEAD_DIM ** -0.5

LANES = 128
VMEM_LIMIT_BYTES = 56 * 1024 * 1024

BF16 = jnp.bfloat16
F32 = jnp.float32

HEADS_STRAIGHT = tuple(h for h in range(ATT_HEADS) if h % 2 == h // Q_PER_KV)
HEADS_SWAPPED = tuple(h for h in range(ATT_HEADS) if h % 2 != h // Q_PER_KV)


def _norm(x):
    mu = jnp.mean(x, axis=-1, keepdims=True)
    xc = x - mu
    var = jnp.mean(xc * xc, axis=-1, keepdims=True)
    return xc * lax.rsqrt(var + LN_EPS)


def _lo_half():
    return lax.broadcasted_iota(jnp.int32, (1, LANES), 1) < HEAD_DIM


def _rotary_lane_signs():
    half = ROT_DIM // 2
    l = lax.broadcasted_iota(jnp.int32, (1, LANES), 1) % HEAD_DIM
    up = jnp.where(l < half, -1.0, 0.0)
    dn = jnp.where((l >= half) & (l < ROT_DIM), 1.0, 0.0)
    return up, dn


def _rotary(x, cos, sin_up, sin_dn):
    half = ROT_DIM // 2
    return (x * cos + pltpu.roll(x, LANES - half, 1) * sin_up
            + pltpu.roll(x, half, 1) * sin_dn)


def _swap_halves(x):
    return pltpu.roll(x, HEAD_DIM, x.ndim - 1)


def _masked_heads(qcols, heads):
    lo = _lo_half()
    parts = []
    for h in heads:
        qc = qcols[h // 2]
        parts.append(jnp.where(lo, qc, 0.0) if h % 2 == 0 else jnp.where(lo, 0.0, qc))
    return jnp.concatenate(parts, axis=-2).astype(BF16)


def _prep_kernel(win_ref, wo_ref, wg_ref, wu_ref, wd_ref, oin_ref, oo_ref, og_ref, ou_ref, od_ref):
    col = lax.broadcasted_iota(jnp.int32, (1, IN_WIDTH), 1)
    qscale = jnp.where(col < ATT_WIDTH, ATT_SCALE, 1.0)
    oin_ref[...] = (win_ref[...] * qscale).astype(BF16)
    oo_ref[...] = wo_ref[...].astype(BF16)
    og_ref[...] = wg_ref[...].astype(BF16)
    ou_ref[...] = wu_ref[...].astype(BF16)
    od_ref[...] = wd_ref[...].astype(BF16)


def _prep_weights(w_in, w_o, w_gate, w_up, w_down, layer, steps=8):
    def spec(w):
        _, r, c = w.shape
        return pl.BlockSpec((None, r // steps, c), lambda j: (layer, j, 0))

    def ospec(w):
        _, r, c = w.shape
        return pl.BlockSpec((r // steps, c), lambda j: (j, 0))

    ws = (w_in, w_o, w_gate, w_up, w_down)
    return pl.pallas_call(
        _prep_kernel,
        grid=(steps,),
        in_specs=[spec(w) for w in ws],
        out_specs=[ospec(w) for w in ws],
        out_shape=[jax.ShapeDtypeStruct(w.shape[1:], BF16) for w in ws],
        compiler_params=pltpu.CompilerParams(
            dimension_semantics=("arbitrary",), vmem_limit_bytes=VMEM_LIMIT_BYTES),
        name="prep_weights",
    )(*ws)


def _ada_kernel(c_ref, w_ref, b_ref, o_ref):
    c = c_ref[...]
    a = (c * jax.nn.sigmoid(c)).astype(BF16)
    o_ref[...] = jnp.dot(a, w_ref[...].astype(BF16), preferred_element_type=F32) + b_ref[...]


def _ada(c_all, w_ada, b_ada, layer, tn=1024):
    rows = c_all.shape[0]
    n = w_ada.shape[2]
    return pl.pallas_call(
        _ada_kernel,
        grid=(n // tn,),
        in_specs=[
            pl.BlockSpec((rows, D_MODEL), lambda j: (0, 0)),
            pl.BlockSpec((None, D_MODEL, tn), lambda j: (layer, 0, j)),
            pl.BlockSpec((1, tn), lambda j: (layer, j)),
        ],
        out_specs=pl.BlockSpec((rows, tn), lambda j: (0, j)),
        out_shape=jax.ShapeDtypeStruct((rows, n), F32),
        compiler_params=pltpu.CompilerParams(dimension_semantics=("arbitrary",)),
        name="ada",
    )(c_all, w_ada, b_ada)


def _attn_scores(qcols, kcat, kcat_sw):
    dn = (((1,), (1,)), ((), ()))
    s_a = lax.dot_general(_masked_heads(qcols, HEADS_STRAIGHT), kcat, dn, preferred_element_type=F32)
    s_b = lax.dot_general(_masked_heads(qcols, HEADS_SWAPPED), kcat_sw, dn, preferred_element_type=F32)
    return s_a, s_b


def _softmax_weights(s, first_key, sinks_ref, heads):
    row = lax.broadcasted_iota(jnp.int32, (WINDOW, 2 * WINDOW), 0)
    col = lax.broadcasted_iota(jnp.int32, (WINDOW, 2 * WINDOW), 1)
    mask = (col > row) & (col <= row + WINDOW) & (col >= first_key)
    es, invs = [], []
    for n, h in enumerate(heads):
        sink = sinks_ref[h]
        sh = jnp.where(mask, s[n * WINDOW:(n + 1) * WINDOW], -jnp.inf)
        m = jnp.maximum(jnp.max(sh, axis=-1, keepdims=True), sink)
        e = jnp.exp(sh - m)
        den = jnp.sum(e, axis=-1, keepdims=True) + jnp.exp(sink - m)
        es.append(e.astype(BF16))
        invs.append(1.0 / den)
    return jnp.concatenate(es, axis=0), invs


def _attn_merge(o_a, inv_a, o_b, inv_b):
    o = {}
    for heads, oo, inv in ((HEADS_STRAIGHT, o_a, inv_a), (HEADS_SWAPPED, o_b, inv_b)):
        for n, h in enumerate(heads):
            o[h] = oo[n * WINDOW:(n + 1) * WINDOW] * inv[n]
    lo = _lo_half()
    return jnp.concatenate([jnp.where(lo, o[2 * c], o[2 * c + 1]) for c in range(4)], axis=1)


def _prompt_sgu(uj, gvj, wsp_ref, bexp):
    lo = _lo_half()
    svs = []
    for p in range(4):
        g = gvj[:, p * LANES:(p + 1) * LANES]
        rhs = jnp.concatenate([jnp.where(lo, g, 0.0), jnp.where(lo, 0.0, g)], axis=0).astype(BF16)
        svs.append(jnp.dot(wsp_ref[p], rhs, preferred_element_type=F32))
    sv = jnp.concatenate(svs, axis=1)
    return uj * (sv + bexp)


def _ffn_math(x, shift, scale, gate, wg_ref, wu_ref, wd_ref, g2_ref, b2_ref):
    h = (_norm(x) * (1.0 + scale) + shift).reshape(-1, D_MODEL).astype(BF16)
    g = jnp.dot(h, wg_ref[...], preferred_element_type=F32)
    up = jnp.dot(h, wu_ref[...], preferred_element_type=F32)
    a = (g * jax.nn.sigmoid(g) * up).astype(BF16)
    f = jnp.dot(a, wd_ref[...], preferred_element_type=F32).reshape(x.shape)
    return _norm(ALPHA * x + gate * f) * g2_ref[...] + b2_ref[...]


FF_SPLIT = 1280


def _prompt_kernel(sinks_ref, x_ref, xp_ref, mod_ref, cr_ref, sr_ref, cb_ref, sb_ref, w_in_ref,
                   wsp_ref, bexp_ref, sg_ref, sb2_ref, w_o_ref, g1_ref, b1_ref, wg_ref, wu_ref, wd_ref,
                   g2_ref, b2_ref, y_ref, kwin_ref, vwin_ref,
                   mixcat_ref, kprev_ref, vprev_ref, kprev_sw_ref, vprev_sw_ref, *, tq, nb, nblocks):
    s = pl.program_id(0)
    sm = jnp.minimum(s, nblocks - 1)
    b, i = sm // nb, sm % nb
    bp = jnp.maximum(s - 1, 0) // nb
    prev_refs = (kprev_ref, vprev_ref, kprev_sw_ref, vprev_sw_ref)

    @pl.when(s == 0)
    def _():
        for r in prev_refs:
            r[...] = jnp.zeros_like(r)
        mixcat_ref[...] = jnp.zeros_like(mixcat_ref)

    modm, modp = mod_ref[b], mod_ref[bp]

    mix = jnp.dot(mixcat_ref[...], w_o_ref[...], preferred_element_type=F32)

    x = x_ref[...]
    h = (_norm(x) * (1.0 + modm[1:2]) + modm[0:1]).astype(BF16)
    z = jnp.dot(h, w_in_ref[...], preferred_element_type=F32)

    y1 = _norm(ALPHA * xp_ref[...] + modp[2:3] * mix) * g1_ref[...] + b1_ref[...]
    hf = (_norm(y1) * (1.0 + modp[4:5]) + modp[3:4]).astype(BF16)
    acts = []
    for cols in (slice(0, FF_SPLIT), slice(FF_SPLIT, D_FF)):
        g = jnp.dot(hf, wg_ref[:, cols], preferred_element_type=F32)
        up = jnp.dot(hf, wu_ref[:, cols], preferred_element_type=F32)
        acts.append((g * jax.nn.sigmoid(g) * up).astype(BF16))

    cb, sb = cb_ref[pl.ds(i, 1), :], sb_ref[pl.ds(i, 1), :]
    cr, sr = cr_ref[...], sr_ref[...]
    cos = cb * cr - sb * sr
    sin = sb * cr + cb * sr
    sgn_up, sgn_dn = _rotary_lane_signs()
    sup, sdn = sin * sgn_up, sin * sgn_dn
    q = [_rotary(z[:, c * LANES:(c + 1) * LANES], cos, sup, sdn) for c in range(4)]
    k = _rotary(z[:, ATT_WIDTH:ATT_WIDTH + KV_WIDTH], cos, sup, sdn)
    v = z[:, ATT_WIDTH + KV_WIDTH:ATT_WIDTH + 2 * KV_WIDTH]
    u = jax.nn.gelu(z[:, ATT_WIDTH + 2 * KV_WIDTH:ATT_WIDTH + 2 * KV_WIDTH + GM_WIDTH])
    gv = _norm(jax.nn.gelu(z[:, ATT_WIDTH + 2 * KV_WIDTH + GM_WIDTH:])) * sg_ref[...] + sb2_ref[...]
    forms = (k.astype(BF16), v.astype(BF16), _swap_halves(k).astype(BF16), _swap_halves(v).astype(BF16))
    kb, vb, kb_sw, vb_sw = forms

    bexp = bexp_ref[...]
    scores, values, sgus = [], [], []
    for j in range(tq // WINDOW):
        sl = slice(j * WINDOW, (j + 1) * WINDOW)
        if j == 0:
            kcat, vcat, kcat_sw, vcat_sw = (
                jnp.concatenate([p[...], c[sl]], axis=0) for p, c in zip(prev_refs, forms))
        else:
            sl2 = slice((j - 1) * WINDOW, (j + 1) * WINDOW)
            kcat, vcat, kcat_sw, vcat_sw = kb[sl2], vb[sl2], kb_sw[sl2], vb_sw[sl2]
        scores.append(_attn_scores([qc[sl] for qc in q], kcat, kcat_sw))
        values.append((vcat, vcat_sw))
        sgus.append(_prompt_sgu(u[sl], gv[sl], wsp_ref, bexp))

    f = (jnp.dot(acts[0], wd_ref[0:FF_SPLIT, :], preferred_element_type=F32)
         + jnp.dot(acts[1], wd_ref[FF_SPLIT:D_FF, :], preferred_element_type=F32))
    y_ref[...] = _norm(ALPHA * y1 + modp[5:6] * f) * g2_ref[...] + b2_ref[...]

    for j in range(tq // WINDOW):
        first_key = jnp.where(i > 0, 0, WINDOW) if j == 0 else 0
        p_a, inv_a = _softmax_weights(scores[j][0], first_key, sinks_ref, HEADS_STRAIGHT)
        p_b, inv_b = _softmax_weights(scores[j][1], first_key, sinks_ref, HEADS_SWAPPED)
        o_a = jnp.dot(p_a, values[j][0], preferred_element_type=F32)
        o_b = jnp.dot(p_b, values[j][1], preferred_element_type=F32)
        att = _attn_merge(o_a, inv_a, o_b, inv_b)
        mixcat_ref[j * WINDOW:(j + 1) * WINDOW, :] = jnp.concatenate([att, sgus[j]], axis=1).astype(BF16)

    for r, fm in zip(prev_refs, forms):
        r[...] = fm[tq - WINDOW:]
    kwin_ref[...] = k[tq - WINDOW:]
    vwin_ref[...] = v[tq - WINDOW:]


def _const_spec(shape):
    nd = len(shape)
    return pl.BlockSpec(shape, lambda *_: (0,) * nd)


def _prompt_layer(x, mod, sinks, tabs, w_in_b, wsp, bexp, sgu_g, sgu_b, w_o_b, g1, b1,
                  wg, wu, wd, g2, b2, tq):
    bsz, seq, _ = x.shape
    nb = seq // tq
    nblocks = bsz * nb
    carry = pltpu.VMEM((WINDOW, KV_WIDTH), BF16)

    def this_block(s):
        sm = jnp.minimum(s, nblocks - 1)
        return sm // nb, sm % nb, 0

    def prev_block(s):
        sp = jnp.maximum(s - 1, 0)
        return sp // nb, sp % nb, 0

    consts = (*tabs, w_in_b, wsp, bexp, sgu_g, sgu_b, w_o_b, g1, b1, wg, wu, wd, g2, b2)
    return pl.pallas_call(
        functools.partial(_prompt_kernel, tq=tq, nb=nb, nblocks=nblocks),
        grid=(nblocks + 1,),
        in_specs=[
            pl.BlockSpec(memory_space=pltpu.SMEM),
            pl.BlockSpec((None, tq, D_MODEL), this_block),
            pl.BlockSpec((None, tq, D_MODEL), prev_block),
            _const_spec(mod.shape),
            *[_const_spec(c.shape) for c in consts],
        ],
        out_specs=[
            pl.BlockSpec((None, tq, D_MODEL), prev_block),
            pl.BlockSpec((None, WINDOW, KV_WIDTH), lambda s: (jnp.minimum(s, nblocks - 1) // nb, 0, 0)),
            pl.BlockSpec((None, WINDOW, KV_WIDTH), lambda s: (jnp.minimum(s, nblocks - 1) // nb, 0, 0)),
        ],
        out_shape=[
            jax.ShapeDtypeStruct((bsz, seq, D_MODEL), F32),
            jax.ShapeDtypeStruct((bsz, WINDOW, KV_WIDTH), F32),
            jax.ShapeDtypeStruct((bsz, WINDOW, KV_WIDTH), F32),
        ],
        scratch_shapes=[pltpu.VMEM((tq, D_MODEL), BF16), carry, carry, carry, carry],
        compiler_params=pltpu.CompilerParams(
            dimension_semantics=("arbitrary",), vmem_limit_bytes=VMEM_LIMIT_BYTES),
        name="prompt_layer",
    )(sinks, x, x, mod, *consts)


def _mixer_sample_kernel(x_ref, mod_ref, cos_ref, sin_ref, kc_ref, vc_ref, sinka_ref, sinkb_ref,
                         w_in_ref, wsx_ref, bexp_ref, sg_ref, sb_ref, w_o_ref, g1_ref, b1_ref,
                         y_ref, kwin_ref, vwin_ref, gv_ref, *, bb, t):
    x3 = x_ref[...]
    shift, scale, gate = mod_ref[:, 0:1, :], mod_ref[:, 1:2, :], mod_ref[:, 2:3, :]
    h = (_norm(x3) * (1.0 + scale) + shift).reshape(bb * t, D_MODEL).astype(BF16)
    z = jnp.dot(h, w_in_ref[...], preferred_element_type=F32)

    cos, sin = cos_ref[...], sin_ref[...]
    up, dn = _rotary_lane_signs()
    sup, sdn = sin * up, sin * dn
    qcols = [_rotary(z[:, c * LANES:(c + 1) * LANES], cos, sup, sdn).reshape(bb, t, LANES)
             for c in range(4)]
    k2 = _rotary(z[:, ATT_WIDTH:ATT_WIDTH + KV_WIDTH], cos, sup, sdn)
    v2 = z[:, ATT_WIDTH + KV_WIDTH:ATT_WIDTH + 2 * KV_WIDTH]
    k3, v3 = k2.reshape(bb, t, KV_WIDTH), v2.reshape(bb, t, KV_WIDTH)
    u3 = jax.nn.gelu(z[:, ATT_WIDTH + 2 * KV_WIDTH:ATT_WIDTH + 2 * KV_WIDTH + GM_WIDTH]
                     ).reshape(bb, t, GM_WIDTH)
    gv = _norm(jax.nn.gelu(z[:, ATT_WIDTH + 2 * KV_WIDTH + GM_WIDTH:])) * sg_ref[...] + sb_ref[...]
    gv3 = gv.reshape(bb, t, GM_WIDTH)
    gv_ref[...] = gv3

    kc = kc_ref[...]
    vc = vc_ref[...]
    wb = kc.shape[1]
    kwin_ref[:, 0:wb - t, :] = kc[:, t:, :]
    kwin_ref[:, wb - t:, :] = k3
    vwin_ref[:, 0:wb - t, :] = vc[:, t:, :]
    vwin_ref[:, wb - t:, :] = v3

    kcat = jnp.concatenate([kc, k3], axis=1)
    vcat = jnp.concatenate([vc, v3], axis=1)
    nq, nk = Q_PER_KV * t, wb + t
    tok = lax.broadcasted_iota(jnp.int32, (nq, nk), 0) % t
    col = lax.broadcasted_iota(jnp.int32, (nq, nk), 1)
    rel = tok + wb - col
    mask = ((rel >= 0) & (rel < WINDOW))[None]
    o = {}
    for heads, sink_ref, km, vm in (
            (HEADS_STRAIGHT, sinka_ref, kcat, vcat),
            (HEADS_SWAPPED, sinkb_ref, _swap_halves(kcat), _swap_halves(vcat))):
        s = jnp.einsum("bqd,bkd->bqk", _masked_heads(qcols, heads), km.astype(BF16),
                       preferred_element_type=F32)
        sink = sink_ref[:, 0:1][None]
        sm = jnp.where(mask, s, -jnp.inf)
        m = jnp.maximum(jnp.max(sm, axis=-1, keepdims=True), sink)
        e = jnp.exp(sm - m)
        inv = 1.0 / (jnp.sum(e, axis=-1, keepdims=True) + jnp.exp(sink - m))
        oh = jnp.einsum("bqk,bkd->bqd", e.astype(BF16), vm.astype(BF16),
                        preferred_element_type=F32) * inv
        for n, hd in enumerate(heads):
            o[hd] = oh[:, n * t:(n + 1) * t, :]
    lo = _lo_half()[None]
    att = jnp.concatenate([jnp.where(lo, o[2 * c], o[2 * c + 1]) for c in range(4)], axis=-1)

    sv = jnp.zeros((bb, t, GM_WIDTH), F32)
    for si in range(t):
        sv = sv + wsx_ref[si][None] * gv3[:, si:si + 1, :]
    sgu = u3 * (sv + bexp_ref[...][None])

    mixcat = jnp.concatenate([att, sgu], axis=-1).reshape(bb * t, D_MODEL).astype(BF16)
    mix = jnp.dot(mixcat, w_o_ref[...], preferred_element_type=F32).reshape(bb, t, D_MODEL)
    y_ref[...] = _norm(ALPHA * x3 + gate * mix) * g1_ref[...] + b1_ref[...]


def _mixer_sample(x, mod, tabs, kc, vc, sink_rows, w_in_b, wsx, bexp_t, sgu_g, sgu_b, w_o_b,
                  g1, b1, bb):
    n, t, _ = x.shape
    wb = kc.shape[1]
    row3 = lambda r, c: pl.BlockSpec((bb, r, c), lambda b: (b, 0, 0))
    consts = (*tabs, )
    return pl.pallas_call(
        functools.partial(_mixer_sample_kernel, bb=bb, t=t),
        grid=(n // bb,),
        in_specs=[
            row3(t, D_MODEL), row3(6, D_MODEL),
            *[_const_spec(a.shape) for a in consts],
            row3(wb, KV_WIDTH), row3(wb, KV_WIDTH),
            *[_const_spec(a.shape) for a in sink_rows],
            _const_spec(w_in_b.shape), _const_spec(wsx.shape), _const_spec(bexp_t.shape),
            _const_spec(sgu_g.shape), _const_spec(sgu_b.shape), _const_spec(w_o_b.shape),
            _const_spec(g1.shape), _const_spec(b1.shape),
        ],
        out_specs=[row3(t, D_MODEL), row3(wb, KV_WIDTH), row3(wb, KV_WIDTH), row3(t, GM_WIDTH)],
        out_shape=[
            jax.ShapeDtypeStruct((n, t, D_MODEL), F32),
            jax.ShapeDtypeStruct((n, wb, KV_WIDTH), F32),
            jax.ShapeDtypeStruct((n, wb, KV_WIDTH), F32),
            jax.ShapeDtypeStruct((n, t, GM_WIDTH), F32),
        ],
        compiler_params=pltpu.CompilerParams(
            dimension_semantics=("arbitrary",), vmem_limit_bytes=VMEM_LIMIT_BYTES),
        name="mixer_sample",
    )(x, mod, *consts, kc, vc, *sink_rows, w_in_b, wsx, bexp_t, sgu_g, sgu_b, w_o_b, g1, b1)


def _ffn_kernel(x_ref, mod_ref, wg_ref, wu_ref, wd_ref, g2_ref, b2_ref, y_ref):
    shift, scale, gate = mod_ref[:, 3:4, :], mod_ref[:, 4:5, :], mod_ref[:, 5:6, :]
    y_ref[...] = _ffn_math(x_ref[...], shift, scale, gate, wg_ref, wu_ref, wd_ref, g2_ref, b2_ref)


def _ffn(x, mod, wg, wu, wd, g2, b2, *, rows):
    gsz, r, _ = x.shape
    if r >= rows:
        grid = (gsz, r // rows)
        x_spec = pl.BlockSpec((None, rows, D_MODEL), lambda b, i: (b, i, 0))
        m_spec = pl.BlockSpec((None, 6, D_MODEL), lambda b, i: (b, 0, 0))
    else:
        bb = rows // r
        grid = (gsz // bb, 1)
        x_spec = pl.BlockSpec((bb, r, D_MODEL), lambda b, i: (b, 0, 0))
        m_spec = pl.BlockSpec((bb, 6, D_MODEL), lambda b, i: (b, 0, 0))
    return pl.pallas_call(
        _ffn_kernel,
        grid=grid,
        in_specs=[x_spec, m_spec, _const_spec(wg.shape), _const_spec(wu.shape),
                  _const_spec(wd.shape), _const_spec(g2.shape), _const_spec(b2.shape)],
        out_specs=x_spec,
        out_shape=jax.ShapeDtypeStruct(x.shape, F32),
        compiler_params=pltpu.CompilerParams(
            dimension_semantics=("arbitrary", "arbitrary"), vmem_limit_bytes=VMEM_LIMIT_BYTES),
        name="ffn",
    )(x, mod, wg, wu, wd, g2, b2)


def _lane_freqs():
    half = ROT_DIM // 2
    inv_freq = jnp.power(jnp.float32(ROPE_THETA), -jnp.arange(half, dtype=F32) * 2.0 / ROT_DIM)
    per_head = jnp.concatenate([inv_freq, inv_freq, jnp.zeros((HEAD_DIM - ROT_DIM,), F32)])
    return jnp.concatenate([per_head, per_head])


def _angle_tables(pos):
    ang = pos.astype(F32)[:, None] * _lane_freqs()[None, :]
    return jnp.cos(ang), jnp.sin(ang)


def kernel(x_prompt, x_sample, cache_k_win, cache_v_win, c_prompt, c_sample, w_ada, b_ada, w_in, attn_sinks, sgu_ln_g, sgu_ln_b, w_s, b_s, w_o, ln1_g, ln1_b, w_gate, w_up, w_down, ln2_g, ln2_b):
    bsz, seq, _ = x_prompt.shape
    nsm, t, _ = x_sample.shape
    wb = cache_k_win.shape[2]
    tq, bb = 256, 32
    tabs_p = (*_angle_tables(jnp.arange(tq, dtype=jnp.int32)),
              *_angle_tables(jnp.arange(seq // tq, dtype=jnp.int32) * tq))
    tabs_s = tuple(jnp.tile(a, (bb, 1))
                   for a in _angle_tables(PAST_LEN + jnp.arange(t, dtype=jnp.int32)))

    yp, ys = x_prompt, x_sample
    kwp, vwp, kws, vws, sgv = [], [], [], [], []
    for l in range(DEPTH):
        w_in_b, w_o_b, wg, wu, wd = _prep_weights(w_in, w_o, w_gate, w_up, w_down, l)
        wtril = jnp.tril(w_s[l])
        wsp = jnp.concatenate([wtril[0::2], wtril[1::2]], axis=2).astype(BF16)
        bexp = jnp.repeat(b_s[l].T, HEAD_DIM, axis=1)
        wsx = jnp.repeat(jnp.transpose(wtril[:, :t, :t], (2, 1, 0)), HEAD_DIM, axis=-1)
        sinks = attn_sinks[l]
        sink_rows = tuple(
            jnp.broadcast_to(jnp.repeat(sinks[jnp.asarray(hs)], t)[:, None], (Q_PER_KV * t, LANES))
            for hs in (HEADS_STRAIGHT, HEADS_SWAPPED))
        sg, sb = sgu_ln_g[l][None, :], sgu_ln_b[l][None, :]
        g1, b1 = ln1_g[l][None, :], ln1_b[l][None, :]
        g2, b2 = ln2_g[l][None, :], ln2_b[l][None, :]

        nrow = bsz + nsm
        pad = (-nrow) % 8
        c_all = jnp.concatenate([c_sample, c_prompt, jnp.zeros((pad, D_MODEL), F32)], axis=0)
        mod = _ada(c_all, w_ada, b_ada, l)
        mod_s = mod[:nsm].reshape(nsm, 6, D_MODEL)
        mod_p = mod[nsm:nrow].reshape(bsz, 6, D_MODEL)

        yp, kw, vw = _prompt_layer(yp, mod_p, sinks, tabs_p, w_in_b, wsp, bexp, sg, sb, w_o_b,
                                   g1, b1, wg, wu, wd, g2, b2, tq)
        kwp.append(kw.reshape(bsz, WINDOW, KV_HEADS, HEAD_DIM))
        vwp.append(vw.reshape(bsz, WINDOW, KV_HEADS, HEAD_DIM))

        kc = cache_k_win[l].reshape(nsm, wb, KV_WIDTH)
        vc = cache_v_win[l].reshape(nsm, wb, KV_WIDTH)
        ys, kn, vn, gvs = _mixer_sample(ys, mod_s, tabs_s, kc, vc, sink_rows, w_in_b, wsx,
                                        bexp[:t], sg, sb, w_o_b, g1, b1, bb)
        ys = _ffn(ys, mod_s, wg, wu, wd, g2, b2, rows=512)
        kws.append(kn.reshape(nsm, wb, KV_HEADS, HEAD_DIM))
        vws.append(vn.reshape(nsm, wb, KV_HEADS, HEAD_DIM))
        sgv.append(gvs)
    return (yp, ys, jnp.stack(kwp, axis=0), jnp.stack(vwp, axis=0), jnp.stack(kws, axis=0),
            jnp.stack(vws, axis=0), jnp.stack(sgv, axis=0))
```

```python
import functools

import jax
import jax.numpy as jnp
from jax import lax
from jax.experimental import pallas as pl
from jax.experimental.pallas import tpu as pltpu

D_MODEL = 1024
HEAD_DIM = 64
ATT_HEADS = 8
KV_HEADS = 2
Q_PER_KV = ATT_HEADS // KV_HEADS
ATT_WIDTH = ATT_HEADS * HEAD_DIM
KV_WIDTH = KV_HEADS * HEAD_DIM
GM_HEADS = 8
GM_WIDTH = GM_HEADS * HEAD_DIM
IN_WIDTH = ATT_WIDTH + 2 * KV_WIDTH + 2 * GM_WIDTH
D_FF = 2816
WINDOW = 128
CHUNK = 128
PAST_LEN = 16384
ROPE_THETA = 500000.0
ROT_DIM = HEAD_DIM // 4
DEPTH = 1
ALPHA = (2 * DEPTH) ** 0.25
LN_EPS = 1e-5
ATT_SCALE = HEAD_DIM ** -0.5

LANES = 128
VMEM_LIMIT_BYTES = 56 * 1024 * 1024
FF_SPLIT = 1280

BF16 = jnp.bfloat16
F32 = jnp.float32

HEADS_STRAIGHT = tuple(h for h in range(ATT_HEADS) if h % 2 == h // Q_PER_KV)
HEADS_SWAPPED = tuple(h for h in range(ATT_HEADS) if h % 2 != h // Q_PER_KV)


def _norm(x):
    mu = jnp.mean(x, axis=-1, keepdims=True)
    xc = x - mu
    var = jnp.mean(xc * xc, axis=-1, keepdims=True)
    return xc * lax.rsqrt(var + LN_EPS)


def _lo_half():
    return lax.broadcasted_iota(jnp.int32, (1, LANES), 1) < HEAD_DIM


def _rotary_lane_signs():
    half = ROT_DIM // 2
    l = lax.broadcasted_iota(jnp.int32, (1, LANES), 1) % HEAD_DIM
    up = jnp.where(l < half, -1.0, 0.0)
    dn = jnp.where((l >= half) & (l < ROT_DIM), 1.0, 0.0)
    return up, dn


def _rotary(x, cos, sin_up, sin_dn):
    half = ROT_DIM // 2
    return (x * cos + pltpu.roll(x, LANES - half, 1) * sin_up
            + pltpu.roll(x, half, 1) * sin_dn)


def _swap_halves(x):
    return pltpu.roll(x, HEAD_DIM, x.ndim - 1)


def _masked_heads(qcols, heads):
    lo = _lo_half()
    parts = []
    for h in heads:
        qc = qcols[h // 2]
        parts.append(jnp.where(lo, qc, 0.0) if h % 2 == 0 else jnp.where(lo, 0.0, qc))
    return jnp.concatenate(parts, axis=-2).astype(BF16)


def _prep_kernel(win_ref, wo_ref, wg_ref, wu_ref, wd_ref, oin_ref, oo_ref, og_ref, ou_ref, od_ref):
    col = lax.broadcasted_iota(jnp.int32, (1, IN_WIDTH), 1)
    qscale = jnp.where(col < ATT_WIDTH, ATT_SCALE, 1.0)
    oin_ref[...] = (win_ref[...] * qscale).astype(BF16)
    oo_ref[...] = wo_ref[...].astype(BF16)
    og_ref[...] = wg_ref[...].astype(BF16)
    ou_ref[...] = wu_ref[...].astype(BF16)
    od_ref[...] = wd_ref[...].astype(BF16)


def _prep_weights(w_in, w_o, w_gate, w_up, w_down, layer, steps=8):
    def spec(w):
        _, r, c = w.shape
        return pl.BlockSpec((None, r // steps, c), lambda j: (layer, j, 0))

    def ospec(w):
        _, r, c = w.shape
        return pl.BlockSpec((r // steps, c), lambda j: (j, 0))

    ws = (w_in, w_o, w_gate, w_up, w_down)
    return pl.pallas_call(
        _prep_kernel,
        grid=(steps,),
        in_specs=[spec(w) for w in ws],
        out_specs=[ospec(w) for w in ws],
        out_shape=[jax.ShapeDtypeStruct(w.shape[1:], BF16) for w in ws],
        compiler_params=pltpu.CompilerParams(
            dimension_semantics=("arbitrary",), vmem_limit_bytes=VMEM_LIMIT_BYTES),
        name="prep_weights",
    )(*ws)


def _ada_kernel(c_ref, w_ref, b_ref, o_ref):
    c = c_ref[...]
    a = (c * jax.nn.sigmoid(c)).astype(BF16)
    o_ref[...] = jnp.dot(a, w_ref[...].astype(BF16), preferred_element_type=F32) + b_ref[...]


def _ada(c_all, w_ada, b_ada, layer, tn=1024):
    rows = c_all.shape[0]
    n = w_ada.shape[2]
    return pl.pallas_call(
        _ada_kernel,
        grid=(n // tn,),
        in_specs=[
            pl.BlockSpec((rows, D_MODEL), lambda j: (0, 0)),
            pl.BlockSpec((None, D_MODEL, tn), lambda j: (layer, 0, j)),
            pl.BlockSpec((1, tn), lambda j: (layer, j)),
        ],
        out_specs=pl.BlockSpec((rows, tn), lambda j: (0, j)),
        out_shape=jax.ShapeDtypeStruct((rows, n), F32),
        compiler_params=pltpu.CompilerParams(dimension_semantics=("arbitrary",)),
        name="ada",
    )(c_all, w_ada, b_ada)


def _attn_scores(qcols, kcat, kcat_sw):
    dn = (((1,), (1,)), ((), ()))
    s_a = lax.dot_general(_masked_heads(qcols, HEADS_STRAIGHT), kcat, dn, preferred_element_type=F32)
    s_b = lax.dot_general(_masked_heads(qcols, HEADS_SWAPPED), kcat_sw, dn, preferred_element_type=F32)
    return s_a, s_b


def _softmax_weights(s, first_key, sinks_ref, heads):
    row = lax.broadcasted_iota(jnp.int32, (WINDOW, 2 * WINDOW), 0)
    col = lax.broadcasted_iota(jnp.int32, (WINDOW, 2 * WINDOW), 1)
    mask = (col > row) & (col <= row + WINDOW) & (col >= first_key)
    es, invs = [], []
    for n, h in enumerate(heads):
        sink = sinks_ref[h]
        sh = jnp.where(mask, s[n * WINDOW:(n + 1) * WINDOW], -jnp.inf)
        m = jnp.maximum(jnp.max(sh, axis=-1, keepdims=True), sink)
        e = jnp.exp(sh - m)
        den = jnp.sum(e, axis=-1, keepdims=True) + jnp.exp(sink - m)
        es.append(e.astype(BF16))
        invs.append(1.0 / den)
    return jnp.concatenate(es, axis=0), invs


def _attn_merge(o_a, inv_a, o_b, inv_b):
    o = {}
    for heads, oo, inv in ((HEADS_STRAIGHT, o_a, inv_a), (HEADS_SWAPPED, o_b, inv_b)):
        for n, h in enumerate(heads):
            o[h] = oo[n * WINDOW:(n + 1) * WINDOW] * inv[n]
    lo = _lo_half()
    return jnp.concatenate([jnp.where(lo, o[2 * c], o[2 * c + 1]) for c in range(4)], axis=1)


def _prompt_sgu(uj, gvj, wsp_ref, bexp):
    lo = _lo_half()
    svs = []
    for p in range(4):
        g = gvj[:, p * LANES:(p + 1) * LANES]
        rhs = jnp.concatenate([jnp.where(lo, g, 0.0), jnp.where(lo, 0.0, g)], axis=0).astype(BF16)
        svs.append(jnp.dot(wsp_ref[p], rhs, preferred_element_type=F32))
    sv = jnp.concatenate(svs, axis=1)
    return uj * (sv + bexp)


def _swiglu(hf, wg_ref, wu_ref, wd_ref):
    f = None
    for lo, hi in ((0, FF_SPLIT), (FF_SPLIT, D_FF)):
        g = jnp.dot(hf, wg_ref[:, lo:hi], preferred_element_type=F32)
        up = jnp.dot(hf, wu_ref[:, lo:hi], preferred_element_type=F32)
        a = (g * jax.nn.sigmoid(g) * up).astype(BF16)
        d = jnp.dot(a, wd_ref[lo:hi, :], preferred_element_type=F32)
        f = d if f is None else f + d
    return f


def _prompt_kernel(sinks_ref, x_ref, xp_ref, mod_ref, cr_ref, sr_ref, cb_ref, sb_ref, w_in_ref,
                   wsp_ref, bexp_ref, sg_ref, sb2_ref, w_o_ref, g1_ref, b1_ref, wg_ref, wu_ref, wd_ref,
                   g2_ref, b2_ref, y_ref, kwin_ref, vwin_ref,
                   mixcat_ref, kprev_ref, vprev_ref, kprev_sw_ref, vprev_sw_ref, *, tq, nb, nblocks):
    s = pl.program_id(0)
    sm = jnp.minimum(s, nblocks - 1)
    b, i = sm // nb, sm % nb
    bp = jnp.maximum(s - 1, 0) // nb
    prev_refs = (kprev_ref, vprev_ref, kprev_sw_ref, vprev_sw_ref)

    @pl.when(s == 0)
    def _():
        for r in prev_refs:
            r[...] = jnp.zeros_like(r)
        mixcat_ref[...] = jnp.zeros_like(mixcat_ref)

    modm, modp = mod_ref[b], mod_ref[bp]

    mix = jnp.dot(mixcat_ref[...], w_o_ref[...], preferred_element_type=F32)

    x = x_ref[...]
    h = (_norm(x) * (1.0 + modm[1:2]) + modm[0:1]).astype(BF16)
    z = jnp.dot(h, w_in_ref[...], preferred_element_type=F32)

    y1 = _norm(ALPHA * xp_ref[...] + modp[2:3] * mix) * g1_ref[...] + b1_ref[...]
    hf = (_norm(y1) * (1.0 + modp[4:5]) + modp[3:4]).astype(BF16)
    acts = []
    for cols in (slice(0, FF_SPLIT), slice(FF_SPLIT, D_FF)):
        g = jnp.dot(hf, wg_ref[:, cols], preferred_element_type=F32)
        up = jnp.dot(hf, wu_ref[:, cols], preferred_element_type=F32)
        acts.append((g * jax.nn.sigmoid(g) * up).astype(BF16))

    cb, sb = cb_ref[pl.ds(i, 1), :], sb_ref[pl.ds(i, 1), :]
    cr, sr = cr_ref[...], sr_ref[...]
    cos = cb * cr - sb * sr
    sin = sb * cr + cb * sr
    sgn_up, sgn_dn = _rotary_lane_signs()
    sup, sdn = sin * sgn_up, sin * sgn_dn
    q = [_rotary(z[:, c * LANES:(c + 1) * LANES], cos, sup, sdn) for c in range(4)]
    k = _rotary(z[:, ATT_WIDTH:ATT_WIDTH + KV_WIDTH], cos, sup, sdn)
    v = z[:, ATT_WIDTH + KV_WIDTH:ATT_WIDTH + 2 * KV_WIDTH]
    u = jax.nn.gelu(z[:, ATT_WIDTH + 2 * KV_WIDTH:ATT_WIDTH + 2 * KV_WIDTH + GM_WIDTH])
    gv = _norm(jax.nn.gelu(z[:, ATT_WIDTH + 2 * KV_WIDTH + GM_WIDTH:])) * sg_ref[...] + sb2_ref[...]
    forms = (k.astype(BF16), v.astype(BF16), _swap_halves(k).astype(BF16), _swap_halves(v).astype(BF16))
    kb, vb, kb_sw, vb_sw = forms

    bexp = bexp_ref[...]
    scores, values, sgus = [], [], []
    for j in range(tq // WINDOW):
        sl = slice(j * WINDOW, (j + 1) * WINDOW)
        if j == 0:
            kcat, vcat, kcat_sw, vcat_sw = (
                jnp.concatenate([p[...], c[sl]], axis=0) for p, c in zip(prev_refs, forms))
        else:
            sl2 = slice((j - 1) * WINDOW, (j + 1) * WINDOW)
            kcat, vcat, kcat_sw, vcat_sw = kb[sl2], vb[sl2], kb_sw[sl2], vb_sw[sl2]
        scores.append(_attn_scores([qc[sl] for qc in q], kcat, kcat_sw))
        values.append((vcat, vcat_sw))
        sgus.append(_prompt_sgu(u[sl], gv[sl], wsp_ref, bexp))

    f = (jnp.dot(acts[0], wd_ref[0:FF_SPLIT, :], preferred_element_type=F32)
         + jnp.dot(acts[1], wd_ref[FF_SPLIT:D_FF, :], preferred_element_type=F32))
    y_ref[...] = _norm(ALPHA * y1 + modp[5:6] * f) * g2_ref[...] + b2_ref[...]

    for j in range(tq // WINDOW):
        first_key = jnp.where(i > 0, 0, WINDOW) if j == 0 else 0
        p_a, inv_a = _softmax_weights(scores[j][0], first_key, sinks_ref, HEADS_STRAIGHT)
        p_b, inv_b = _softmax_weights(scores[j][1], first_key, sinks_ref, HEADS_SWAPPED)
        o_a = jnp.dot(p_a, values[j][0], preferred_element_type=F32)
        o_b = jnp.dot(p_b, values[j][1], preferred_element_type=F32)
        att = _attn_merge(o_a, inv_a, o_b, inv_b)
        mixcat_ref[j * WINDOW:(j + 1) * WINDOW, :] = jnp.concatenate([att, sgus[j]], axis=1).astype(BF16)

    for r, fm in zip(prev_refs, forms):
        r[...] = fm[tq - WINDOW:]
    kwin_ref[...] = k[tq - WINDOW:].T
    vwin_ref[...] = v[tq - WINDOW:].T


def _const_spec(shape):
    nd = len(shape)
    return pl.BlockSpec(shape, lambda *_: (0,) * nd)


def _prompt_layer(x, mod, sinks, tabs, w_in_b, wsp, bexp, sgu_g, sgu_b, w_o_b, g1, b1,
                  wg, wu, wd, g2, b2, tq):
    bsz, seq, _ = x.shape
    nb = seq // tq
    nblocks = bsz * nb
    carry = pltpu.VMEM((WINDOW, KV_WIDTH), BF16)

    def this_block(s):
        sm = jnp.minimum(s, nblocks - 1)
        return sm // nb, sm % nb, 0

    def prev_block(s):
        sp = jnp.maximum(s - 1, 0)
        return sp // nb, sp % nb, 0

    consts = (*tabs, w_in_b, wsp, bexp, sgu_g, sgu_b, w_o_b, g1, b1, wg, wu, wd, g2, b2)
    win_spec = pl.BlockSpec((None, KV_WIDTH, WINDOW),
                            lambda s: (jnp.minimum(s, nblocks - 1) // nb, 0, 0))
    return pl.pallas_call(
        functools.partial(_prompt_kernel, tq=tq, nb=nb, nblocks=nblocks),
        grid=(nblocks + 1,),
        in_specs=[
            pl.BlockSpec(memory_space=pltpu.SMEM),
            pl.BlockSpec((None, tq, D_MODEL), this_block),
            pl.BlockSpec((None, tq, D_MODEL), prev_block),
            _const_spec(mod.shape),
            *[_const_spec(c.shape) for c in consts],
        ],
        out_specs=[pl.BlockSpec((None, tq, D_MODEL), prev_block), win_spec, win_spec],
        out_shape=[
            jax.ShapeDtypeStruct((bsz, seq, D_MODEL), F32),
            jax.ShapeDtypeStruct((bsz, KV_WIDTH, WINDOW), F32),
            jax.ShapeDtypeStruct((bsz, KV_WIDTH, WINDOW), F32),
        ],
        scratch_shapes=[pltpu.VMEM((tq, D_MODEL), BF16), carry, carry, carry, carry],
        compiler_params=pltpu.CompilerParams(
            dimension_semantics=("arbitrary",), vmem_limit_bytes=VMEM_LIMIT_BYTES),
        name="prompt_layer",
    )(sinks, x, x, mod, *consts)


def _shifted_window(old, new2, bb, t):
    wb = old.shape[2]
    lane = lax.broadcasted_iota(jnp.int32, (1, wb), 1)
    shifted = pltpu.roll(old.reshape(bb * KV_WIDTH, wb), wb - t, 1).reshape(bb, KV_WIDTH, wb)
    new_t = new2.T
    outs = []
    for n in range(bb):
        tile = new_t[:, (n * t // LANES) * LANES:(n * t // LANES + 1) * LANES]
        tail = pltpu.roll(tile, (wb - t - (n * t) % LANES) % LANES, 1)
        outs.append(jnp.where(lane >= wb - t, tail, shifted[n]))
    return outs


def _sample_kernel(x_ref, mod_ref, cos_ref, sin_ref, kt_ref, vt_ref, sinka_ref, sinkb_ref,
                   w_in_ref, wsx_ref, bexp_ref, sg_ref, sb_ref, w_o_ref, g1_ref, b1_ref,
                   wg_ref, wu_ref, wd_ref, g2_ref, b2_ref,
                   y_ref, ktn_ref, vtn_ref, gv_ref, *, bb, t):
    x3 = x_ref[...]
    mod = [mod_ref[:, r:r + 1, :] for r in range(6)]
    h = (_norm(x3) * (1.0 + mod[1]) + mod[0]).reshape(bb * t, D_MODEL).astype(BF16)
    z = jnp.dot(h, w_in_ref[...], preferred_element_type=F32)

    cos, sin = cos_ref[...], sin_ref[...]
    sgn_up, sgn_dn = _rotary_lane_signs()
    sup, sdn = sin * sgn_up, sin * sgn_dn
    qcols = [_rotary(z[:, c * LANES:(c + 1) * LANES], cos, sup, sdn).reshape(bb, t, LANES)
             for c in range(4)]
    k2 = _rotary(z[:, ATT_WIDTH:ATT_WIDTH + KV_WIDTH], cos, sup, sdn)
    v2 = z[:, ATT_WIDTH + KV_WIDTH:ATT_WIDTH + 2 * KV_WIDTH]
    u3 = jax.nn.gelu(z[:, ATT_WIDTH + 2 * KV_WIDTH:ATT_WIDTH + 2 * KV_WIDTH + GM_WIDTH]
                     ).reshape(bb, t, GM_WIDTH)
    gv = _norm(jax.nn.gelu(z[:, ATT_WIDTH + 2 * KV_WIDTH + GM_WIDTH:])) * sg_ref[...] + sb_ref[...]
    gv3 = gv.reshape(bb, t, GM_WIDTH)
    gv_ref[...] = gv3

    kt, vt = kt_ref[...], vt_ref[...]
    wb = kt.shape[2]
    for n, (ko, vo) in enumerate(zip(_shifted_window(kt, k2, bb, t), _shifted_window(vt, v2, bb, t))):
        ktn_ref[n] = ko
        vtn_ref[n] = vo

    nq = Q_PER_KV * t
    tok = lax.broadcasted_iota(jnp.int32, (nq, 1), 0) % t
    mask_c = (lax.broadcasted_iota(jnp.int32, (nq, wb), 1) + (WINDOW - wb) > tok)[None]
    mask_n = (lax.broadcasted_iota(jnp.int32, (nq, t), 1) <= tok)[None]
    kb, vb = kt.astype(BF16), vt.astype(BF16)
    swap_rows = lambda a: jnp.concatenate([a[:, HEAD_DIM:, :], a[:, :HEAD_DIM, :]], axis=1)
    k3, v3 = k2.reshape(bb, t, KV_WIDTH), v2.reshape(bb, t, KV_WIDTH)
    o = {}
    for heads, sink_ref, km, vm, kn, vn in (
            (HEADS_STRAIGHT, sinka_ref, kb, vb, k3, v3),
            (HEADS_SWAPPED, sinkb_ref, swap_rows(kb), swap_rows(vb), _swap_halves(k3), _swap_halves(v3))):
        lhs = _masked_heads(qcols, heads)
        s_c = jnp.einsum("bqd,bdw->bqw", lhs, km, preferred_element_type=F32)
        s_n = jnp.einsum("bqd,bkd->bqk", lhs, kn.astype(BF16), preferred_element_type=F32)
        s_c = jnp.where(mask_c, s_c, -jnp.inf)
        s_n = jnp.where(mask_n, s_n, -jnp.inf)
        sink = sink_ref[:, 0:1][None]
        m = jnp.maximum(jnp.maximum(jnp.max(s_c, axis=-1, keepdims=True),
                                    jnp.max(s_n, axis=-1, keepdims=True)), sink)
        e_c, e_n = jnp.exp(s_c - m), jnp.exp(s_n - m)
        inv = 1.0 / (jnp.sum(e_c, axis=-1, keepdims=True) + jnp.sum(e_n, axis=-1, keepdims=True)
                     + jnp.exp(sink - m))
        oh = (jnp.einsum("bqw,bdw->bqd", e_c.astype(BF16), vm, preferred_element_type=F32)
              + jnp.einsum("bqk,bkd->bqd", e_n.astype(BF16), vn.astype(BF16),
                           preferred_element_type=F32)) * inv
        for n, hd in enumerate(heads):
            o[hd] = oh[:, n * t:(n + 1) * t, :]
    lo = _lo_half()[None]
    att = jnp.concatenate([jnp.where(lo, o[2 * c], o[2 * c + 1]) for c in range(4)], axis=-1)

    sv = jnp.zeros((bb, t, GM_WIDTH), F32)
    for si in range(t):
        sv = sv + wsx_ref[si][None] * gv3[:, si:si + 1, :]
    sgu = u3 * (sv + bexp_ref[...][None])

    mixcat = jnp.concatenate([att, sgu], axis=-1).reshape(bb * t, D_MODEL).astype(BF16)
    mix = jnp.dot(mixcat, w_o_ref[...], preferred_element_type=F32).reshape(bb, t, D_MODEL)
    y1 = _norm(ALPHA * x3 + mod[2] * mix) * g1_ref[...] + b1_ref[...]

    hf = (_norm(y1) * (1.0 + mod[4]) + mod[3]).reshape(bb * t, D_MODEL).astype(BF16)
    f = _swiglu(hf, wg_ref, wu_ref, wd_ref).reshape(bb, t, D_MODEL)
    y_ref[...] = _norm(ALPHA * y1 + mod[5] * f) * g2_ref[...] + b2_ref[...]


def _sample_layer(x, mod, tabs, kt, vt, sink_rows, w_in_b, wsx, bexp_t, sgu_g, sgu_b, w_o_b,
                  g1, b1, wg, wu, wd, g2, b2, bb):
    n, t, _ = x.shape
    wb = kt.shape[2]
    row3 = lambda r, c: pl.BlockSpec((bb, r, c), lambda b: (b, 0, 0))
    consts = (*tabs, )
    consts2 = (*sink_rows, w_in_b, wsx, bexp_t, sgu_g, sgu_b, w_o_b, g1, b1, wg, wu, wd, g2, b2)
    return pl.pallas_call(
        functools.partial(_sample_kernel, bb=bb, t=t),
        grid=(n // bb,),
        in_specs=[
            row3(t, D_MODEL), row3(6, D_MODEL),
            *[_const_spec(a.shape) for a in consts],
            row3(KV_WIDTH, wb), row3(KV_WIDTH, wb),
            *[_const_spec(a.shape) for a in consts2],
        ],
        out_specs=[row3(t, D_MODEL), row3(KV_WIDTH, wb), row3(KV_WIDTH, wb), row3(t, GM_WIDTH)],
        out_shape=[
            jax.ShapeDtypeStruct((n, t, D_MODEL), F32),
            jax.ShapeDtypeStruct((n, KV_WIDTH, wb), F32),
            jax.ShapeDtypeStruct((n, KV_WIDTH, wb), F32),
            jax.ShapeDtypeStruct((n, t, GM_WIDTH), F32),
        ],
        compiler_params=pltpu.CompilerParams(
            dimension_semantics=("arbitrary",), vmem_limit_bytes=VMEM_LIMIT_BYTES),
        name="sample_layer",
    )(x, mod, *consts, kt, vt, *consts2)


def _lane_freqs():
    half = ROT_DIM // 2
    inv_freq = jnp.power(jnp.float32(ROPE_THETA), -jnp.arange(half, dtype=F32) * 2.0 / ROT_DIM)
    per_head = jnp.concatenate([inv_freq, inv_freq, jnp.zeros((HEAD_DIM - ROT_DIM,), F32)])
    return jnp.concatenate([per_head, per_head])


def _angle_tables(pos):
    ang = pos.astype(F32)[:, None] * _lane_freqs()[None, :]
    return jnp.cos(ang), jnp.sin(ang)


def _to_feature_major(win):
    n, w = win.shape[0], win.shape[1]
    return jnp.transpose(win, (0, 2, 3, 1)).reshape(n, KV_WIDTH, w)


def _from_feature_major(win_t):
    n, _, w = win_t.shape
    return jnp.transpose(win_t.reshape(n, KV_HEADS, HEAD_DIM, w), (0, 3, 1, 2))


def kernel(x_prompt, x_sample, cache_k_win, cache_v_win, c_prompt, c_sample, w_ada, b_ada, w_in, attn_sinks, sgu_ln_g, sgu_ln_b, w_s, b_s, w_o, ln1_g, ln1_b, w_gate, w_up, w_down, ln2_g, ln2_b):
    bsz, seq, _ = x_prompt.shape
    nsm, t, _ = x_sample.shape
    tq, bb = 256, 32
    tabs_p = (*_angle_tables(jnp.arange(tq, dtype=jnp.int32)),
              *_angle_tables(jnp.arange(seq // tq, dtype=jnp.int32) * tq))
    tabs_s = tuple(jnp.tile(a, (bb, 1))
                   for a in _angle_tables(PAST_LEN + jnp.arange(t, dtype=jnp.int32)))

    yp, ys = x_prompt, x_sample
    kwp, vwp, kws, vws, sgv = [], [], [], [], []
    for l in range(DEPTH):
        w_in_b, w_o_b, wg, wu, wd = _prep_weights(w_in, w_o, w_gate, w_up, w_down, l)
        wtril = jnp.tril(w_s[l])
        wsp = jnp.concatenate([wtril[0::2], wtril[1::2]], axis=2).astype(BF16)
        bexp = jnp.repeat(b_s[l].T, HEAD_DIM, axis=1)
        wsx = jnp.repeat(jnp.transpose(wtril[:, :t, :t], (2, 1, 0)), HEAD_DIM, axis=-1)
        sinks = attn_sinks[l]
        sink_rows = tuple(
            jnp.broadcast_to(jnp.repeat(sinks[jnp.asarray(hs)], t)[:, None], (Q_PER_KV * t, LANES))
            for hs in (HEADS_STRAIGHT, HEADS_SWAPPED))
        sg, sb = sgu_ln_g[l][None, :], sgu_ln_b[l][None, :]
        g1, b1 = ln1_g[l][None, :], ln1_b[l][None, :]
        g2, b2 = ln2_g[l][None, :], ln2_b[l][None, :]

        nrow = bsz + nsm
        pad = (-nrow) % 8
        c_all = jnp.concatenate([c_sample, c_prompt, jnp.zeros((pad, D_MODEL), F32)], axis=0)
        mod = _ada(c_all, w_ada, b_ada, l)
        mod_s = mod[:nsm].reshape(nsm, 6, D_MODEL)
        mod_p = mod[nsm:nrow].reshape(bsz, 6, D_MODEL)

        yp, kw, vw = _prompt_layer(yp, mod_p, sinks, tabs_p, w_in_b, wsp, bexp, sg, sb, w_o_b,
                                   g1, b1, wg, wu, wd, g2, b2, tq)
        kwp.append(_from_feature_major(kw))
        vwp.append(_from_feature_major(vw))

        ys, kn, vn, gvs = _sample_layer(
            ys, mod_s, tabs_s, _to_feature_major(cache_k_win[l]), _to_feature_major(cache_v_win[l]),
            sink_rows, w_in_b, wsx, bexp[:t], sg, sb, w_o_b, g1, b1, wg, wu, wd, g2, b2, bb)
        kws.append(_from_feature_major(kn))
        vws.append(_from_feature_major(vn))
        sgv.append(gvs)
    return (yp, ys, jnp.stack(kwp, axis=0), jnp.stack(vwp, axis=0), jnp.stack(kws, axis=0),
            jnp.stack(vws, axis=0), jnp.stack(sgv, axis=0))
```

```python
import functools

import jax
import jax.numpy as jnp
from jax import lax
from jax.experimental import pallas as pl
from jax.experimental.pallas import tpu as pltpu

D_MODEL = 1024
HEAD_DIM = 64
ATT_HEADS = 8
KV_HEADS = 2
Q_PER_KV = ATT_HEADS // KV_HEADS
ATT_WIDTH = ATT_HEADS * HEAD_DIM
KV_WIDTH = KV_HEADS * HEAD_DIM
GM_HEADS = 8
GM_WIDTH = GM_HEADS * HEAD_DIM
IN_WIDTH = ATT_WIDTH + 2 * KV_WIDTH + 2 * GM_WIDTH
D_FF = 2816
WINDOW = 128
CHUNK = 128
PAST_LEN = 16384
ROPE_THETA = 500000.0
ROT_DIM = HEAD_DIM // 4
DEPTH = 1
ALPHA = (2 * DEPTH) ** 0.25
LN_EPS = 1e-5
ATT_SCALE = HEAD_DIM ** -0.5

LANES = 128
VMEM_LIMIT_BYTES = 56 * 1024 * 1024
FF_SPLIT = 1280

BF16 = jnp.bfloat16
F32 = jnp.float32

HEADS_STRAIGHT = tuple(h for h in range(ATT_HEADS) if h % 2 == h // Q_PER_KV)
HEADS_SWAPPED = tuple(h for h in range(ATT_HEADS) if h % 2 != h // Q_PER_KV)


def _norm(x):
    mu = jnp.mean(x, axis=-1, keepdims=True)
    xc = x - mu
    var = jnp.mean(xc * xc, axis=-1, keepdims=True)
    return xc * lax.rsqrt(var + LN_EPS)


def _lo_half():
    return lax.broadcasted_iota(jnp.int32, (1, LANES), 1) < HEAD_DIM


def _rotary_lane_signs():
    half = ROT_DIM // 2
    l = lax.broadcasted_iota(jnp.int32, (1, LANES), 1) % HEAD_DIM
    up = jnp.where(l < half, -1.0, 0.0)
    dn = jnp.where((l >= half) & (l < ROT_DIM), 1.0, 0.0)
    return up, dn


def _rotary(x, cos, sin_up, sin_dn):
    half = ROT_DIM // 2
    return (x * cos + pltpu.roll(x, LANES - half, 1) * sin_up
            + pltpu.roll(x, half, 1) * sin_dn)


def _swap_halves(x):
    return pltpu.roll(x, HEAD_DIM, x.ndim - 1)


def _masked_heads(qcols, heads):
    lo = _lo_half()
    parts = []
    for h in heads:
        qc = qcols[h // 2]
        parts.append(jnp.where(lo, qc, 0.0) if h % 2 == 0 else jnp.where(lo, 0.0, qc))
    return jnp.concatenate(parts, axis=-2).astype(BF16)


def _prep_kernel(win_ref, wo_ref, wg_ref, wu_ref, wd_ref, oin_ref, oo_ref, og_ref, ou_ref, od_ref):
    col = lax.broadcasted_iota(jnp.int32, (1, IN_WIDTH), 1)
    qscale = jnp.where(col < ATT_WIDTH, ATT_SCALE, 1.0)
    oin_ref[...] = (win_ref[...] * qscale).astype(BF16)
    oo_ref[...] = wo_ref[...].astype(BF16)
    og_ref[...] = wg_ref[...].astype(BF16)
    ou_ref[...] = wu_ref[...].astype(BF16)
    od_ref[...] = wd_ref[...].astype(BF16)


def _prep_weights(w_in, w_o, w_gate, w_up, w_down, layer, steps=8):
    def spec(w):
        _, r, c = w.shape
        return pl.BlockSpec((None, r // steps, c), lambda j: (layer, j, 0))

    def ospec(w):
        _, r, c = w.shape
        return pl.BlockSpec((r // steps, c), lambda j: (j, 0))

    ws = (w_in, w_o, w_gate, w_up, w_down)
    return pl.pallas_call(
        _prep_kernel,
        grid=(steps,),
        in_specs=[spec(w) for w in ws],
        out_specs=[ospec(w) for w in ws],
        out_shape=[jax.ShapeDtypeStruct(w.shape[1:], BF16) for w in ws],
        compiler_params=pltpu.CompilerParams(
            dimension_semantics=("arbitrary",), vmem_limit_bytes=VMEM_LIMIT_BYTES),
        name="prep_weights",
    )(*ws)


def _ada_kernel(cs_ref, cp_ref, w_ref, b_ref, os_ref, op_ref):
    w = w_ref[...].astype(BF16)
    for c_ref, o_ref in ((cs_ref, os_ref), (cp_ref, op_ref)):
        c = c_ref[...]
        a = (c * jax.nn.sigmoid(c)).astype(BF16)
        o_ref[...] = jnp.dot(a, w, preferred_element_type=F32) + b_ref[...]


def _ada(c_sample, c_prompt, w_ada, b_ada, layer):
    nvec = w_ada.shape[2] // D_MODEL
    whole = lambda c: pl.BlockSpec(c.shape, lambda j: (0, 0))
    vec = lambda c: pl.BlockSpec((None, c.shape[0], D_MODEL), lambda j: (j, 0, 0))
    return pl.pallas_call(
        _ada_kernel,
        grid=(nvec,),
        in_specs=[
            whole(c_sample), whole(c_prompt),
            pl.BlockSpec((None, D_MODEL, D_MODEL), lambda j: (layer, 0, j)),
            pl.BlockSpec((1, D_MODEL), lambda j: (layer, j)),
        ],
        out_specs=[vec(c_sample), vec(c_prompt)],
        out_shape=[jax.ShapeDtypeStruct((nvec, c.shape[0], D_MODEL), F32) for c in (c_sample, c_prompt)],
        compiler_params=pltpu.CompilerParams(dimension_semantics=("arbitrary",)),
        name="ada",
    )(c_sample, c_prompt, w_ada, b_ada)


def _attn_scores(qcols, kcat, kcat_sw):
    dn = (((1,), (1,)), ((), ()))
    s_a = lax.dot_general(_masked_heads(qcols, HEADS_STRAIGHT), kcat, dn, preferred_element_type=F32)
    s_b = lax.dot_general(_masked_heads(qcols, HEADS_SWAPPED), kcat_sw, dn, preferred_element_type=F32)
    return s_a, s_b


def _softmax_weights(s, first_key, sinks_ref, heads):
    row = lax.broadcasted_iota(jnp.int32, (WINDOW, 2 * WINDOW), 0)
    col = lax.broadcasted_iota(jnp.int32, (WINDOW, 2 * WINDOW), 1)
    mask = (col > row) & (col <= row + WINDOW) & (col >= first_key)
    es, invs = [], []
    for n, h in enumerate(heads):
        sink = sinks_ref[h]
        sh = jnp.where(mask, s[n * WINDOW:(n + 1) * WINDOW], -jnp.inf)
        m = jnp.maximum(jnp.max(sh, axis=-1, keepdims=True), sink)
        e = jnp.exp(sh - m)
        den = jnp.sum(e, axis=-1, keepdims=True) + jnp.exp(sink - m)
        es.append(e.astype(BF16))
        invs.append(1.0 / den)
    return jnp.concatenate(es, axis=0), invs


def _attn_merge(o_a, inv_a, o_b, inv_b):
    o = {}
    for heads, oo, inv in ((HEADS_STRAIGHT, o_a, inv_a), (HEADS_SWAPPED, o_b, inv_b)):
        for n, h in enumerate(heads):
            o[h] = oo[n * WINDOW:(n + 1) * WINDOW] * inv[n]
    lo = _lo_half()
    return jnp.concatenate([jnp.where(lo, o[2 * c], o[2 * c + 1]) for c in range(4)], axis=1)


def _prompt_sgu(uj, gvj, wsp_ref, bexp):
    lo = _lo_half()
    svs = []
    for p in range(4):
        g = gvj[:, p * LANES:(p + 1) * LANES]
        rhs = jnp.concatenate([jnp.where(lo, g, 0.0), jnp.where(lo, 0.0, g)], axis=0).astype(BF16)
        svs.append(jnp.dot(wsp_ref[p], rhs, preferred_element_type=F32))
    sv = jnp.concatenate(svs, axis=1)
    return uj * (sv + bexp)


def _swiglu(hf, wg_ref, wu_ref, wd_ref):
    f = None
    for lo, hi in ((0, FF_SPLIT), (FF_SPLIT, D_FF)):
        g = jnp.dot(hf, wg_ref[:, lo:hi], preferred_element_type=F32)
        up = jnp.dot(hf, wu_ref[:, lo:hi], preferred_element_type=F32)
        a = (g * jax.nn.sigmoid(g) * up).astype(BF16)
        d = jnp.dot(a, wd_ref[lo:hi, :], preferred_element_type=F32)
        f = d if f is None else f + d
    return f


def _prompt_kernel(sinks_ref, x_ref, xp_ref, mod_ref, cr_ref, sr_ref, cb_ref, sb_ref, w_in_ref,
                   wsp_ref, bexp_ref, sg_ref, sb2_ref, w_o_ref, g1_ref, b1_ref, wg_ref, wu_ref, wd_ref,
                   g2_ref, b2_ref, y_ref, kwin_ref, vwin_ref,
                   mixcat_ref, kprev_ref, vprev_ref, kprev_sw_ref, vprev_sw_ref, *, tq, nb, nblocks):
    s = pl.program_id(0)
    sm = jnp.minimum(s, nblocks - 1)
    b, i = sm // nb, sm % nb
    bp = jnp.maximum(s - 1, 0) // nb
    prev_refs = (kprev_ref, vprev_ref, kprev_sw_ref, vprev_sw_ref)

    @pl.when(s == 0)
    def _():
        for r in prev_refs:
            r[...] = jnp.zeros_like(r)
        mixcat_ref[...] = jnp.zeros_like(mixcat_ref)

    modm = [mod_ref[r, pl.ds(b, 1), :] for r in range(2)]
    modp = [mod_ref[r, pl.ds(bp, 1), :] for r in range(6)]

    mix = jnp.dot(mixcat_ref[...], w_o_ref[...], preferred_element_type=F32)

    x = x_ref[...]
    h = (_norm(x) * (1.0 + modm[1]) + modm[0]).astype(BF16)
    z = jnp.dot(h, w_in_ref[...], preferred_element_type=F32)

    y1 = _norm(ALPHA * xp_ref[...] + modp[2] * mix) * g1_ref[...] + b1_ref[...]
    hf = (_norm(y1) * (1.0 + modp[4]) + modp[3]).astype(BF16)
    acts = []
    for cols in (slice(0, FF_SPLIT), slice(FF_SPLIT, D_FF)):
        g = jnp.dot(hf, wg_ref[:, cols], preferred_element_type=F32)
        up = jnp.dot(hf, wu_ref[:, cols], preferred_element_type=F32)
        acts.append((g * jax.nn.sigmoid(g) * up).astype(BF16))

    cb, sb = cb_ref[pl.ds(i, 1), :], sb_ref[pl.ds(i, 1), :]
    cr, sr = cr_ref[...], sr_ref[...]
    cos = cb * cr - sb * sr
    sin = sb * cr + cb * sr
    sgn_up, sgn_dn = _rotary_lane_signs()
    sup, sdn = sin * sgn_up, sin * sgn_dn
    q = [_rotary(z[:, c * LANES:(c + 1) * LANES], cos, sup, sdn) for c in range(4)]
    k = _rotary(z[:, ATT_WIDTH:ATT_WIDTH + KV_WIDTH], cos, sup, sdn)
    v = z[:, ATT_WIDTH + KV_WIDTH:ATT_WIDTH + 2 * KV_WIDTH]
    u = jax.nn.gelu(z[:, ATT_WIDTH + 2 * KV_WIDTH:ATT_WIDTH + 2 * KV_WIDTH + GM_WIDTH])
    gv = _norm(jax.nn.gelu(z[:, ATT_WIDTH + 2 * KV_WIDTH + GM_WIDTH:])) * sg_ref[...] + sb2_ref[...]
    forms = (k.astype(BF16), v.astype(BF16), _swap_halves(k).astype(BF16), _swap_halves(v).astype(BF16))
    kb, vb, kb_sw, vb_sw = forms

    bexp = bexp_ref[...]
    scores, values, sgus = [], [], []
    for j in range(tq // WINDOW):
        sl = slice(j * WINDOW, (j + 1) * WINDOW)
        if j == 0:
            kcat, vcat, kcat_sw, vcat_sw = (
                jnp.concatenate([p[...], c[sl]], axis=0) for p, c in zip(prev_refs, forms))
        else:
            sl2 = slice((j - 1) * WINDOW, (j + 1) * WINDOW)
            kcat, vcat, kcat_sw, vcat_sw = kb[sl2], vb[sl2], kb_sw[sl2], vb_sw[sl2]
        scores.append(_attn_scores([qc[sl] for qc in q], kcat, kcat_sw))
        values.append((vcat, vcat_sw))
        sgus.append(_prompt_sgu(u[sl], gv[sl], wsp_ref, bexp))

    f = (jnp.dot(acts[0], wd_ref[0:FF_SPLIT, :], preferred_element_type=F32)
         + jnp.dot(acts[1], wd_ref[FF_SPLIT:D_FF, :], preferred_element_type=F32))
    y_ref[...] = _norm(ALPHA * y1 + modp[5] * f) * g2_ref[...] + b2_ref[...]

    for j in range(tq // WINDOW):
        first_key = jnp.where(i > 0, 0, WINDOW) if j == 0 else 0
        p_a, inv_a = _softmax_weights(scores[j][0], first_key, sinks_ref, HEADS_STRAIGHT)
        p_b, inv_b = _softmax_weights(scores[j][1], first_key, sinks_ref, HEADS_SWAPPED)
        o_a = jnp.dot(p_a, values[j][0], preferred_element_type=F32)
        o_b = jnp.dot(p_b, values[j][1], preferred_element_type=F32)
        att = _attn_merge(o_a, inv_a, o_b, inv_b)
        mixcat_ref[j * WINDOW:(j + 1) * WINDOW, :] = jnp.concatenate([att, sgus[j]], axis=1).astype(BF16)

    for r, fm in zip(prev_refs, forms):
        r[...] = fm[tq - WINDOW:]
    kwin_ref[...] = k[tq - WINDOW:].T
    vwin_ref[...] = v[tq - WINDOW:].T


def _const_spec(shape):
    nd = len(shape)
    return pl.BlockSpec(shape, lambda *_: (0,) * nd)


def _prompt_layer(x, mod, sinks, tabs, w_in_b, wsp, bexp, sgu_g, sgu_b, w_o_b, g1, b1,
                  wg, wu, wd, g2, b2, tq):
    bsz, seq, _ = x.shape
    nb = seq // tq
    nblocks = bsz * nb
    carry = pltpu.VMEM((WINDOW, KV_WIDTH), BF16)

    def this_block(s):
        sm = jnp.minimum(s, nblocks - 1)
        return sm // nb, sm % nb, 0

    def prev_block(s):
        sp = jnp.maximum(s - 1, 0)
        return sp // nb, sp % nb, 0

    consts = (*tabs, w_in_b, wsp, bexp, sgu_g, sgu_b, w_o_b, g1, b1, wg, wu, wd, g2, b2)
    win_spec = pl.BlockSpec((None, KV_WIDTH, WINDOW),
                            lambda s: (jnp.minimum(s, nblocks - 1) // nb, 0, 0))
    return pl.pallas_call(
        functools.partial(_prompt_kernel, tq=tq, nb=nb, nblocks=nblocks),
        grid=(nblocks + 1,),
        in_specs=[
            pl.BlockSpec(memory_space=pltpu.SMEM),
            pl.BlockSpec((None, tq, D_MODEL), this_block),
            pl.BlockSpec((None, tq, D_MODEL), prev_block),
            _const_spec(mod.shape),
            *[_const_spec(c.shape) for c in consts],
        ],
        out_specs=[pl.BlockSpec((None, tq, D_MODEL), prev_block), win_spec, win_spec],
        out_shape=[
            jax.ShapeDtypeStruct((bsz, seq, D_MODEL), F32),
            jax.ShapeDtypeStruct((bsz, KV_WIDTH, WINDOW), F32),
            jax.ShapeDtypeStruct((bsz, KV_WIDTH, WINDOW), F32),
        ],
        scratch_shapes=[pltpu.VMEM((tq, D_MODEL), BF16), carry, carry, carry, carry],
        compiler_params=pltpu.CompilerParams(
            dimension_semantics=("arbitrary",), vmem_limit_bytes=VMEM_LIMIT_BYTES),
        name="prompt_layer",
    )(sinks, x, x, mod, *consts)


def _shifted_window(old, new2, bb, t):
    wb = old.shape[2]
    lane = lax.broadcasted_iota(jnp.int32, (1, wb), 1)
    shifted = pltpu.roll(old.reshape(bb * KV_WIDTH, wb), wb - t, 1).reshape(bb, KV_WIDTH, wb)
    new_t = new2.T
    outs = []
    for n in range(bb):
        tile = new_t[:, (n * t // LANES) * LANES:(n * t // LANES + 1) * LANES]
        tail = pltpu.roll(tile, (wb - t - (n * t) % LANES) % LANES, 1)
        outs.append(jnp.where(lane >= wb - t, tail, shifted[n]))
    return outs


def _sample_kernel(x_ref, mod_ref, cos_ref, sin_ref, kt_ref, vt_ref, sinka_ref, sinkb_ref,
                   w_in_ref, wsx_ref, bexp_ref, sg_ref, sb_ref, w_o_ref, g1_ref, b1_ref,
                   wg_ref, wu_ref, wd_ref, g2_ref, b2_ref,
                   y_ref, ktn_ref, vtn_ref, gv_ref, *, bb, t):
    x3 = x_ref[...]
    mod = [mod_ref[r][:, None, :] for r in range(6)]
    h = (_norm(x3) * (1.0 + mod[1]) + mod[0]).reshape(bb * t, D_MODEL).astype(BF16)
    z = jnp.dot(h, w_in_ref[...], preferred_element_type=F32)

    cos, sin = cos_ref[...], sin_ref[...]
    sgn_up, sgn_dn = _rotary_lane_signs()
    sup, sdn = sin * sgn_up, sin * sgn_dn
    qcols = [_rotary(z[:, c * LANES:(c + 1) * LANES], cos, sup, sdn).reshape(bb, t, LANES)
             for c in range(4)]
    k2 = _rotary(z[:, ATT_WIDTH:ATT_WIDTH + KV_WIDTH], cos, sup, sdn)
    v2 = z[:, ATT_WIDTH + KV_WIDTH:ATT_WIDTH + 2 * KV_WIDTH]
    u3 = jax.nn.gelu(z[:, ATT_WIDTH + 2 * KV_WIDTH:ATT_WIDTH + 2 * KV_WIDTH + GM_WIDTH]
                     ).reshape(bb, t, GM_WIDTH)
    gv = _norm(jax.nn.gelu(z[:, ATT_WIDTH + 2 * KV_WIDTH + GM_WIDTH:])) * sg_ref[...] + sb_ref[...]
    gv3 = gv.reshape(bb, t, GM_WIDTH)
    gv_ref[...] = gv3

    kt, vt = kt_ref[...], vt_ref[...]
    wb = kt.shape[2]
    for n, (ko, vo) in enumerate(zip(_shifted_window(kt, k2, bb, t), _shifted_window(vt, v2, bb, t))):
        ktn_ref[n] = ko
        vtn_ref[n] = vo

    nq = Q_PER_KV * t
    tok = lax.broadcasted_iota(jnp.int32, (nq, 1), 0) % t
    mask_c = (lax.broadcasted_iota(jnp.int32, (nq, wb), 1) + (WINDOW - wb) > tok)[None]
    mask_n = (lax.broadcasted_iota(jnp.int32, (nq, t), 1) <= tok)[None]
    kb, vb = kt.astype(BF16), vt.astype(BF16)
    swap_rows = lambda a: jnp.concatenate([a[:, HEAD_DIM:, :], a[:, :HEAD_DIM, :]], axis=1)
    k3, v3 = k2.reshape(bb, t, KV_WIDTH), v2.reshape(bb, t, KV_WIDTH)
    o = {}
    for heads, sink_ref, km, vm, kn, vn in (
            (HEADS_STRAIGHT, sinka_ref, kb, vb, k3, v3),
            (HEADS_SWAPPED, sinkb_ref, swap_rows(kb), swap_rows(vb), _swap_halves(k3), _swap_halves(v3))):
        lhs = _masked_heads(qcols, heads)
        s_c = jnp.einsum("bqd,bdw->bqw", lhs, km, preferred_element_type=F32)
        s_n = jnp.einsum("bqd,bkd->bqk", lhs, kn.astype(BF16), preferred_element_type=F32)
        s_c = jnp.where(mask_c, s_c, -jnp.inf)
        s_n = jnp.where(mask_n, s_n, -jnp.inf)
        sink = sink_ref[:, 0:1][None]
        m = jnp.maximum(jnp.maximum(jnp.max(s_c, axis=-1, keepdims=True),
                                    jnp.max(s_n, axis=-1, keepdims=True)), sink)
        e_c, e_n = jnp.exp(s_c - m), jnp.exp(s_n - m)
        inv = 1.0 / (jnp.sum(e_c, axis=-1, keepdims=True) + jnp.sum(e_n, axis=-1, keepdims=True)
                     + jnp.exp(sink - m))
        oh = (jnp.einsum("bqw,bdw->bqd", e_c.astype(BF16), vm, preferred_element_type=F32)
              + jnp.einsum("bqk,bkd->bqd", e_n.astype(BF16), vn.astype(BF16),
                           preferred_element_type=F32)) * inv
        for n, hd in enumerate(heads):
            o[hd] = oh[:, n * t:(n + 1) * t, :]
    lo = _lo_half()[None]
    att = jnp.concatenate([jnp.where(lo, o[2 * c], o[2 * c + 1]) for c in range(4)], axis=-1)

    sv = jnp.zeros((bb, t, GM_WIDTH), F32)
    for si in range(t):
        sv = sv + wsx_ref[si][None] * gv3[:, si:si + 1, :]
    sgu = u3 * (sv + bexp_ref[...][None])

    mixcat = jnp.concatenate([att, sgu], axis=-1).reshape(bb * t, D_MODEL).astype(BF16)
    mix = jnp.dot(mixcat, w_o_ref[...], preferred_element_type=F32).reshape(bb, t, D_MODEL)
    y1 = _norm(ALPHA * x3 + mod[2] * mix) * g1_ref[...] + b1_ref[...]

    hf = (_norm(y1) * (1.0 + mod[4]) + mod[3]).reshape(bb * t, D_MODEL).astype(BF16)
    f = _swiglu(hf, wg_ref, wu_ref, wd_ref).reshape(bb, t, D_MODEL)
    y_ref[...] = _norm(ALPHA * y1 + mod[5] * f) * g2_ref[...] + b2_ref[...]


def _sample_layer(x, mod, tabs, kt, vt, sink_rows, w_in_b, wsx, bexp_t, sgu_g, sgu_b, w_o_b,
                  g1, b1, wg, wu, wd, g2, b2, bb):
    n, t, _ = x.shape
    wb = kt.shape[2]
    row3 = lambda r, c: pl.BlockSpec((bb, r, c), lambda b: (b, 0, 0))
    consts = (*tabs, )
    consts2 = (*sink_rows, w_in_b, wsx, bexp_t, sgu_g, sgu_b, w_o_b, g1, b1, wg, wu, wd, g2, b2)
    return pl.pallas_call(
        functools.partial(_sample_kernel, bb=bb, t=t),
        grid=(n // bb,),
        in_specs=[
            row3(t, D_MODEL), pl.BlockSpec((6, bb, D_MODEL), lambda b: (0, b, 0)),
            *[_const_spec(a.shape) for a in consts],
            row3(KV_WIDTH, wb), row3(KV_WIDTH, wb),
            *[_const_spec(a.shape) for a in consts2],
        ],
        out_specs=[row3(t, D_MODEL), row3(KV_WIDTH, wb), row3(KV_WIDTH, wb), row3(t, GM_WIDTH)],
        out_shape=[
            jax.ShapeDtypeStruct((n, t, D_MODEL), F32),
            jax.ShapeDtypeStruct((n, KV_WIDTH, wb), F32),
            jax.ShapeDtypeStruct((n, KV_WIDTH, wb), F32),
            jax.ShapeDtypeStruct((n, t, GM_WIDTH), F32),
        ],
        compiler_params=pltpu.CompilerParams(
            dimension_semantics=("arbitrary",), vmem_limit_bytes=VMEM_LIMIT_BYTES),
        name="sample_layer",
    )(x, mod, *consts, kt, vt, *consts2)


def _lane_freqs():
    half = ROT_DIM // 2
    inv_freq = jnp.power(jnp.float32(ROPE_THETA), -jnp.arange(half, dtype=F32) * 2.0 / ROT_DIM)
    per_head = jnp.concatenate([inv_freq, inv_freq, jnp.zeros((HEAD_DIM - ROT_DIM,), F32)])
    return jnp.concatenate([per_head, per_head])


def _angle_tables(pos):
    ang = pos.astype(F32)[:, None] * _lane_freqs()[None, :]
    return jnp.cos(ang), jnp.sin(ang)


def _to_feature_major(win):
    n, w = win.shape[0], win.shape[1]
    return jnp.transpose(win, (0, 2, 3, 1)).reshape(n, KV_WIDTH, w)


def _from_feature_major(win_t):
    n, _, w = win_t.shape
    return jnp.transpose(win_t.reshape(n, KV_HEADS, HEAD_DIM, w), (0, 3, 1, 2))


def kernel(x_prompt, x_sample, cache_k_win, cache_v_win, c_prompt, c_sample, w_ada, b_ada, w_in, attn_sinks, sgu_ln_g, sgu_ln_b, w_s, b_s, w_o, ln1_g, ln1_b, w_gate, w_up, w_down, ln2_g, ln2_b):
    bsz, seq, _ = x_prompt.shape
    nsm, t, _ = x_sample.shape
    tq, bb = 256, 32
    tabs_p = (*_angle_tables(jnp.arange(tq, dtype=jnp.int32)),
              *_angle_tables(jnp.arange(seq // tq, dtype=jnp.int32) * tq))
    tabs_s = tuple(jnp.tile(a, (bb, 1))
                   for a in _angle_tables(PAST_LEN + jnp.arange(t, dtype=jnp.int32)))

    yp, ys = x_prompt, x_sample
    kwp, vwp, kws, vws, sgv = [], [], [], [], []
    for l in range(DEPTH):
        w_in_b, w_o_b, wg, wu, wd = _prep_weights(w_in, w_o, w_gate, w_up, w_down, l)
        wtril = jnp.tril(w_s[l])
        wsp = jnp.concatenate([wtril[0::2], wtril[1::2]], axis=2).astype(BF16)
        bexp = jnp.repeat(b_s[l].T, HEAD_DIM, axis=1)
        wsx = jnp.repeat(jnp.transpose(wtril[:, :t, :t], (2, 1, 0)), HEAD_DIM, axis=-1)
        sinks = attn_sinks[l]
        sink_rows = tuple(
            jnp.broadcast_to(jnp.repeat(sinks[jnp.asarray(hs)], t)[:, None], (Q_PER_KV * t, LANES))
            for hs in (HEADS_STRAIGHT, HEADS_SWAPPED))
        sg, sb = sgu_ln_g[l][None, :], sgu_ln_b[l][None, :]
        g1, b1 = ln1_g[l][None, :], ln1_b[l][None, :]
        g2, b2 = ln2_g[l][None, :], ln2_b[l][None, :]

        mod_s, mod_p = _ada(c_sample, c_prompt, w_ada, b_ada, l)

        yp, kw, vw = _prompt_layer(yp, mod_p, sinks, tabs_p, w_in_b, wsp, bexp, sg, sb, w_o_b,
                                   g1, b1, wg, wu, wd, g2, b2, tq)
        kwp.append(_from_feature_major(kw))
        vwp.append(_from_feature_major(vw))

        ys, kn, vn, gvs = _sample_layer(
            ys, mod_s, tabs_s, _to_feature_major(cache_k_win[l]), _to_feature_major(cache_v_win[l]),
            sink_rows, w_in_b, wsx, bexp[:t], sg, sb, w_o_b, g1, b1, wg, wu, wd, g2, b2, bb)
        kws.append(_from_feature_major(kn))
        vws.append(_from_feature_major(vn))
        sgv.append(gvs)
    return (yp, ys, jnp.stack(kwp, axis=0), jnp.stack(vwp, axis=0), jnp.stack(kws, axis=0),
            jnp.stack(vws, axis=0), jnp.stack(sgv, axis=0))
```

```python
import functools

import jax
import jax.numpy as jnp
import numpy as np
from jax import lax
from jax.experimental import pallas as pl
from jax.experimental.pallas import tpu as pltpu

D_MODEL = 1024
HEAD_DIM = 64
ATT_HEADS = 8
KV_HEADS = 2
Q_PER_KV = ATT_HEADS // KV_HEADS
ATT_WIDTH = ATT_HEADS * HEAD_DIM
KV_WIDTH = KV_HEADS * HEAD_DIM
GM_HEADS = 8
GM_WIDTH = GM_HEADS * HEAD_DIM
IN_WIDTH = ATT_WIDTH + 2 * KV_WIDTH + 2 * GM_WIDTH
D_FF = 2816
WINDOW = 128
CHUNK = 128
PAST_LEN = 16384
ROPE_THETA = 500000.0
ROT_DIM = HEAD_DIM // 4
DEPTH = 1
ALPHA = (2 * DEPTH) ** 0.25
LN_EPS = 1e-5
ATT_SCALE = HEAD_DIM ** -0.5

LANES = 128
VMEM_LIMIT_BYTES = 56 * 1024 * 1024
FF_SPLIT = 1280

BF16 = jnp.bfloat16
F32 = jnp.float32

HEADS_STRAIGHT = tuple(h for h in range(ATT_HEADS) if h % 2 == h // Q_PER_KV)
HEADS_SWAPPED = tuple(h for h in range(ATT_HEADS) if h % 2 != h // Q_PER_KV)


def _norm(x):
    mu = jnp.mean(x, axis=-1, keepdims=True)
    xc = x - mu
    var = jnp.mean(xc * xc, axis=-1, keepdims=True)
    return xc * lax.rsqrt(var + LN_EPS)


def _lo_half():
    return lax.broadcasted_iota(jnp.int32, (1, LANES), 1) < HEAD_DIM


def _rotary_lane_signs():
    half = ROT_DIM // 2
    l = lax.broadcasted_iota(jnp.int32, (1, LANES), 1) % HEAD_DIM
    up = jnp.where(l < half, -1.0, 0.0)
    dn = jnp.where((l >= half) & (l < ROT_DIM), 1.0, 0.0)
    return up, dn


def _rotary(x, cos, sin_up, sin_dn):
    half = ROT_DIM // 2
    return (x * cos + pltpu.roll(x, LANES - half, 1) * sin_up
            + pltpu.roll(x, half, 1) * sin_dn)


def _swap_halves(x):
    return pltpu.roll(x, HEAD_DIM, x.ndim - 1)


def _masked_heads(qcols, heads):
    lo = _lo_half()
    parts = []
    for h in heads:
        qc = qcols[h // 2]
        parts.append(jnp.where(lo, qc, 0.0) if h % 2 == 0 else jnp.where(lo, 0.0, qc))
    return jnp.concatenate(parts, axis=-2).astype(BF16)


def _prep_kernel(win_ref, wo_ref, wg_ref, wu_ref, wd_ref, ws_ref,
                 oin_ref, oo_ref, og_ref, ou_ref, od_ref, osp_ref):
    col = lax.broadcasted_iota(jnp.int32, (1, IN_WIDTH), 1)
    qscale = jnp.where(col < ATT_WIDTH, ATT_SCALE, 1.0)
    oin_ref[...] = (win_ref[...] * qscale).astype(BF16)
    oo_ref[...] = wo_ref[...].astype(BF16)
    og_ref[...] = wg_ref[...].astype(BF16)
    ou_ref[...] = wu_ref[...].astype(BF16)
    od_ref[...] = wd_ref[...].astype(BF16)
    row = lax.broadcasted_iota(jnp.int32, (CHUNK, CHUNK), 0)
    colc = lax.broadcasted_iota(jnp.int32, (CHUNK, CHUNK), 1)
    osp_ref[...] = jnp.where(colc <= row, ws_ref[...], 0.0).astype(BF16)


def _prep_weights(w_in, w_o, w_gate, w_up, w_down, w_s, layer):
    steps = GM_HEADS

    def spec(w):
        _, r, c = w.shape
        return pl.BlockSpec((None, r // steps, c), lambda j: (layer, j, 0))

    def ospec(w):
        _, r, c = w.shape
        return pl.BlockSpec((r // steps, c), lambda j: (j, 0))

    ws = (w_in, w_o, w_gate, w_up, w_down)
    return pl.pallas_call(
        _prep_kernel,
        grid=(steps,),
        in_specs=[spec(w) for w in ws]
        + [pl.BlockSpec((None, None, CHUNK, CHUNK), lambda j: (layer, j, 0, 0))],
        out_specs=[ospec(w) for w in ws]
        + [pl.BlockSpec((None, CHUNK, CHUNK), lambda j: (j // 2, 0, j % 2))],
        out_shape=[jax.ShapeDtypeStruct(w.shape[1:], BF16) for w in ws]
        + [jax.ShapeDtypeStruct((GM_HEADS // 2, CHUNK, 2 * CHUNK), BF16)],
        compiler_params=pltpu.CompilerParams(
            dimension_semantics=("arbitrary",), vmem_limit_bytes=VMEM_LIMIT_BYTES),
        name="prep_weights",
    )(*ws, w_s)


ADA_K_CHUNK = 256


def _ada_kernel(cs_ref, cp_ref, w_ref, b_ref, os_ref, op_ref):
    nvec = os_ref.shape[0]
    vec_cols = [slice(r * D_MODEL, (r + 1) * D_MODEL) for r in range(nvec)]

    @pl.when(pl.program_id(0) == 0)
    def _():
        for o_ref in (os_ref, op_ref):
            for r, cols in enumerate(vec_cols):
                o_ref[r] = jnp.broadcast_to(b_ref[:, cols], o_ref.shape[1:])

    w = w_ref[...].astype(BF16)
    for c_ref, o_ref in ((cs_ref, os_ref), (cp_ref, op_ref)):
        c = c_ref[...]
        a = (c * jax.nn.sigmoid(c)).astype(BF16)
        part = jnp.dot(a, w, preferred_element_type=F32)
        for r, cols in enumerate(vec_cols):
            o_ref[r] += part[:, cols]


def _ada(c_sample, c_prompt, w_ada, b_ada, layer):
    n = w_ada.shape[2]
    nvec = n // D_MODEL
    kchunk = lambda c: pl.BlockSpec((c.shape[0], ADA_K_CHUNK), lambda k: (0, k))
    acc = lambda c: pl.BlockSpec((nvec, c.shape[0], D_MODEL), lambda k: (0, 0, 0))
    return pl.pallas_call(
        _ada_kernel,
        grid=(D_MODEL // ADA_K_CHUNK,),
        in_specs=[
            kchunk(c_sample), kchunk(c_prompt),
            pl.BlockSpec((None, ADA_K_CHUNK, n), lambda k: (layer, k, 0)),
            pl.BlockSpec((1, n), lambda k: (layer, 0)),
        ],
        out_specs=[acc(c_sample), acc(c_prompt)],
        out_shape=[jax.ShapeDtypeStruct((nvec, c.shape[0], D_MODEL), F32) for c in (c_sample, c_prompt)],
        compiler_params=pltpu.CompilerParams(dimension_semantics=("arbitrary",)),
        name="ada",
    )(c_sample, c_prompt, w_ada, b_ada)


def _attn_scores(qcols, kcat, kcat_sw):
    dn = (((1,), (1,)), ((), ()))
    s_a = lax.dot_general(_masked_heads(qcols, HEADS_STRAIGHT), kcat, dn, preferred_element_type=F32)
    s_b = lax.dot_general(_masked_heads(qcols, HEADS_SWAPPED), kcat_sw, dn, preferred_element_type=F32)
    return s_a, s_b


def _softmax_weights(s, first_key, sinks_ref, heads):
    row = lax.broadcasted_iota(jnp.int32, (WINDOW, 2 * WINDOW), 0)
    col = lax.broadcasted_iota(jnp.int32, (WINDOW, 2 * WINDOW), 1)
    mask = (col > row) & (col <= row + WINDOW) & (col >= first_key)
    es, invs = [], []
    for n, h in enumerate(heads):
        sink = sinks_ref[h]
        sh = jnp.where(mask, s[n * WINDOW:(n + 1) * WINDOW], -jnp.inf)
        m = jnp.maximum(jnp.max(sh, axis=-1, keepdims=True), sink)
        e = jnp.exp(sh - m)
        den = jnp.sum(e, axis=-1, keepdims=True) + jnp.exp(sink - m)
        es.append(e.astype(BF16))
        invs.append(1.0 / den)
    return jnp.concatenate(es, axis=0), invs


def _attn_merge(o_a, inv_a, o_b, inv_b):
    o = {}
    for heads, oo, inv in ((HEADS_STRAIGHT, o_a, inv_a), (HEADS_SWAPPED, o_b, inv_b)):
        for n, h in enumerate(heads):
            o[h] = oo[n * WINDOW:(n + 1) * WINDOW] * inv[n]
    lo = _lo_half()
    return jnp.concatenate([jnp.where(lo, o[2 * c], o[2 * c + 1]) for c in range(4)], axis=1)


def _prompt_sgu(uj, gvj, wsp_ref, bexp):
    lo = _lo_half()
    svs = []
    for p in range(4):
        g = gvj[:, p * LANES:(p + 1) * LANES]
        rhs = jnp.concatenate([jnp.where(lo, g, 0.0), jnp.where(lo, 0.0, g)], axis=0).astype(BF16)
        svs.append(jnp.dot(wsp_ref[p], rhs, preferred_element_type=F32))
    sv = jnp.concatenate(svs, axis=1)
    return uj * (sv + bexp)


def _swiglu(hf, wg_ref, wu_ref, wd_ref):
    f = None
    for lo, hi in ((0, FF_SPLIT), (FF_SPLIT, D_FF)):
        g = jnp.dot(hf, wg_ref[:, lo:hi], preferred_element_type=F32)
        up = jnp.dot(hf, wu_ref[:, lo:hi], preferred_element_type=F32)
        a = (g * jax.nn.sigmoid(g) * up).astype(BF16)
        d = jnp.dot(a, wd_ref[lo:hi, :], preferred_element_type=F32)
        f = d if f is None else f + d
    return f


def _prompt_kernel(sinks_ref, x_ref, xp_ref, mod_ref, cr_ref, sr_ref, cb_ref, sb_ref, w_in_ref,
                   wsp_ref, bexp_ref, sg_ref, sb2_ref, w_o_ref, g1_ref, b1_ref, wg_ref, wu_ref, wd_ref,
                   g2_ref, b2_ref, y_ref, kwin_ref, vwin_ref,
                   mixcat_ref, kprev_ref, vprev_ref, kprev_sw_ref, vprev_sw_ref, *, tq, nb, nblocks):
    s = pl.program_id(0)
    sm = jnp.minimum(s, nblocks - 1)
    b, i = sm // nb, sm % nb
    bp = jnp.maximum(s - 1, 0) // nb
    prev_refs = (kprev_ref, vprev_ref, kprev_sw_ref, vprev_sw_ref)

    @pl.when(s == 0)
    def _():
        for r in prev_refs:
            r[...] = jnp.zeros_like(r)
        mixcat_ref[...] = jnp.zeros_like(mixcat_ref)

    modm = [mod_ref[r, pl.ds(b, 1), :] for r in range(2)]
    modp = [mod_ref[r, pl.ds(bp, 1), :] for r in range(6)]

    mix = jnp.dot(mixcat_ref[...], w_o_ref[...], preferred_element_type=F32)

    x = x_ref[...]
    h = (_norm(x) * (1.0 + modm[1]) + modm[0]).astype(BF16)
    z = jnp.dot(h, w_in_ref[...], preferred_element_type=F32)

    y1 = _norm(ALPHA * xp_ref[...] + modp[2] * mix) * g1_ref[...] + b1_ref[...]
    hf = (_norm(y1) * (1.0 + modp[4]) + modp[3]).astype(BF16)
    acts = []
    for cols in (slice(0, FF_SPLIT), slice(FF_SPLIT, D_FF)):
        g = jnp.dot(hf, wg_ref[:, cols], preferred_element_type=F32)
        up = jnp.dot(hf, wu_ref[:, cols], preferred_element_type=F32)
        acts.append((g * jax.nn.sigmoid(g) * up).astype(BF16))

    cb, sb = cb_ref[pl.ds(i, 1), :], sb_ref[pl.ds(i, 1), :]
    cr, sr = cr_ref[...], sr_ref[...]
    cos = cb * cr - sb * sr
    sin = sb * cr + cb * sr
    sgn_up, sgn_dn = _rotary_lane_signs()
    sup, sdn = sin * sgn_up, sin * sgn_dn
    q = [_rotary(z[:, c * LANES:(c + 1) * LANES], cos, sup, sdn) for c in range(4)]
    k = _rotary(z[:, ATT_WIDTH:ATT_WIDTH + KV_WIDTH], cos, sup, sdn)
    v = z[:, ATT_WIDTH + KV_WIDTH:ATT_WIDTH + 2 * KV_WIDTH]
    u = jax.nn.gelu(z[:, ATT_WIDTH + 2 * KV_WIDTH:ATT_WIDTH + 2 * KV_WIDTH + GM_WIDTH])
    gv = _norm(jax.nn.gelu(z[:, ATT_WIDTH + 2 * KV_WIDTH + GM_WIDTH:])) * sg_ref[...] + sb2_ref[...]
    forms = (k.astype(BF16), v.astype(BF16), _swap_halves(k).astype(BF16), _swap_halves(v).astype(BF16))
    kb, vb, kb_sw, vb_sw = forms

    bexp = bexp_ref[...]
    scores, values, sgus = [], [], []
    for j in range(tq // WINDOW):
        sl = slice(j * WINDOW, (j + 1) * WINDOW)
        if j == 0:
            kcat, vcat, kcat_sw, vcat_sw = (
                jnp.concatenate([p[...], c[sl]], axis=0) for p, c in zip(prev_refs, forms))
        else:
            sl2 = slice((j - 1) * WINDOW, (j + 1) * WINDOW)
            kcat, vcat, kcat_sw, vcat_sw = kb[sl2], vb[sl2], kb_sw[sl2], vb_sw[sl2]
        scores.append(_attn_scores([qc[sl] for qc in q], kcat, kcat_sw))
        values.append((vcat, vcat_sw))
        sgus.append(_prompt_sgu(u[sl], gv[sl], wsp_ref, bexp))

    f = (jnp.dot(acts[0], wd_ref[0:FF_SPLIT, :], preferred_element_type=F32)
         + jnp.dot(acts[1], wd_ref[FF_SPLIT:D_FF, :], preferred_element_type=F32))
    y_ref[...] = _norm(ALPHA * y1 + modp[5] * f) * g2_ref[...] + b2_ref[...]

    for j in range(tq // WINDOW):
        first_key = jnp.where(i > 0, 0, WINDOW) if j == 0 else 0
        p_a, inv_a = _softmax_weights(scores[j][0], first_key, sinks_ref, HEADS_STRAIGHT)
        p_b, inv_b = _softmax_weights(scores[j][1], first_key, sinks_ref, HEADS_SWAPPED)
        o_a = jnp.dot(p_a, values[j][0], preferred_element_type=F32)
        o_b = jnp.dot(p_b, values[j][1], preferred_element_type=F32)
        att = _attn_merge(o_a, inv_a, o_b, inv_b)
        mixcat_ref[j * WINDOW:(j + 1) * WINDOW, :] = jnp.concatenate([att, sgus[j]], axis=1).astype(BF16)

    for r, fm in zip(prev_refs, forms):
        r[...] = fm[tq - WINDOW:]
    kwin_ref[...] = k[tq - WINDOW:].T
    vwin_ref[...] = v[tq - WINDOW:].T


def _const_spec(shape):
    nd = len(shape)
    return pl.BlockSpec(shape, lambda *_: (0,) * nd)


def _prompt_layer(x, mod, sinks, tabs, w_in_b, wsp, bexp, sgu_g, sgu_b, w_o_b, g1, b1,
                  wg, wu, wd, g2, b2, tq):
    bsz, seq, _ = x.shape
    nb = seq // tq
    nblocks = bsz * nb
    carry = pltpu.VMEM((WINDOW, KV_WIDTH), BF16)

    def this_block(s):
        sm = jnp.minimum(s, nblocks - 1)
        return sm // nb, sm % nb, 0

    def prev_block(s):
        sp = jnp.maximum(s - 1, 0)
        return sp // nb, sp % nb, 0

    consts = (*tabs, w_in_b, wsp, bexp, sgu_g, sgu_b, w_o_b, g1, b1, wg, wu, wd, g2, b2)
    win_spec = pl.BlockSpec((None, KV_WIDTH, WINDOW),
                            lambda s: (jnp.minimum(s, nblocks - 1) // nb, 0, 0))
    return pl.pallas_call(
        functools.partial(_prompt_kernel, tq=tq, nb=nb, nblocks=nblocks),
        grid=(nblocks + 1,),
        in_specs=[
            pl.BlockSpec(memory_space=pltpu.SMEM),
            pl.BlockSpec((None, tq, D_MODEL), this_block),
            pl.BlockSpec((None, tq, D_MODEL), prev_block),
            _const_spec(mod.shape),
            *[_const_spec(c.shape) for c in consts],
        ],
        out_specs=[pl.BlockSpec((None, tq, D_MODEL), prev_block), win_spec, win_spec],
        out_shape=[
            jax.ShapeDtypeStruct((bsz, seq, D_MODEL), F32),
            jax.ShapeDtypeStruct((bsz, KV_WIDTH, WINDOW), F32),
            jax.ShapeDtypeStruct((bsz, KV_WIDTH, WINDOW), F32),
        ],
        scratch_shapes=[pltpu.VMEM((tq, D_MODEL), BF16), carry, carry, carry, carry],
        compiler_params=pltpu.CompilerParams(
            dimension_semantics=("arbitrary",), vmem_limit_bytes=VMEM_LIMIT_BYTES),
        name="prompt_layer",
    )(sinks, x, x, mod, *consts)


def _shifted_window(old, new2, bb, t):
    wb = old.shape[2]
    lane = lax.broadcasted_iota(jnp.int32, (1, wb), 1)
    shifted = pltpu.roll(old.reshape(bb * KV_WIDTH, wb), wb - t, 1).reshape(bb, KV_WIDTH, wb)
    new_t = new2.T
    outs = []
    for n in range(bb):
        tile = new_t[:, (n * t // LANES) * LANES:(n * t // LANES + 1) * LANES]
        tail = pltpu.roll(tile, (wb - t - (n * t) % LANES) % LANES, 1)
        outs.append(jnp.where(lane >= wb - t, tail, shifted[n]))
    return outs


def _sample_kernel(sinks_ref, x_ref, mod_ref, cos_ref, sin_ref, kt_ref, vt_ref,
                   w_in_ref, wsx_ref, bexp_ref, sg_ref, sb_ref, w_o_ref, g1_ref, b1_ref,
                   wg_ref, wu_ref, wd_ref, g2_ref, b2_ref,
                   y_ref, ktn_ref, vtn_ref, gv_ref, *, bb, t):
    x3 = x_ref[...]
    mod = [mod_ref[r][:, None, :] for r in range(6)]
    h = (_norm(x3) * (1.0 + mod[1]) + mod[0]).reshape(bb * t, D_MODEL).astype(BF16)
    z = jnp.dot(h, w_in_ref[...], preferred_element_type=F32)

    cos, sin = cos_ref[...], sin_ref[...]
    sgn_up, sgn_dn = _rotary_lane_signs()
    sup, sdn = sin * sgn_up, sin * sgn_dn
    qcols = [_rotary(z[:, c * LANES:(c + 1) * LANES], cos, sup, sdn).reshape(bb, t, LANES)
             for c in range(4)]
    k2 = _rotary(z[:, ATT_WIDTH:ATT_WIDTH + KV_WIDTH], cos, sup, sdn)
    v2 = z[:, ATT_WIDTH + KV_WIDTH:ATT_WIDTH + 2 * KV_WIDTH]
    u3 = jax.nn.gelu(z[:, ATT_WIDTH + 2 * KV_WIDTH:ATT_WIDTH + 2 * KV_WIDTH + GM_WIDTH]
                     ).reshape(bb, t, GM_WIDTH)
    gv = _norm(jax.nn.gelu(z[:, ATT_WIDTH + 2 * KV_WIDTH + GM_WIDTH:])) * sg_ref[...] + sb_ref[...]
    gv3 = gv.reshape(bb, t, GM_WIDTH)
    gv_ref[...] = gv3

    kt, vt = kt_ref[...], vt_ref[...]
    wb = kt.shape[2]
    for n, (ko, vo) in enumerate(zip(_shifted_window(kt, k2, bb, t), _shifted_window(vt, v2, bb, t))):
        ktn_ref[n] = ko
        vtn_ref[n] = vo

    nq = Q_PER_KV * t
    tok = lax.broadcasted_iota(jnp.int32, (nq, 1), 0) % t
    mask_c = (lax.broadcasted_iota(jnp.int32, (nq, wb), 1) + (WINDOW - wb) > tok)[None]
    mask_n = (lax.broadcasted_iota(jnp.int32, (nq, t), 1) <= tok)[None]
    kb, vb = kt.astype(BF16), vt.astype(BF16)
    swap_rows = lambda a: jnp.concatenate([a[:, HEAD_DIM:, :], a[:, :HEAD_DIM, :]], axis=1)
    k3, v3 = k2.reshape(bb, t, KV_WIDTH), v2.reshape(bb, t, KV_WIDTH)
    head_row = lax.broadcasted_iota(jnp.int32, (nq, 1), 0) // t
    o = {}
    for heads, km, vm, kn, vn in (
            (HEADS_STRAIGHT, kb, vb, k3, v3),
            (HEADS_SWAPPED, swap_rows(kb), swap_rows(vb), _swap_halves(k3), _swap_halves(v3))):
        lhs = _masked_heads(qcols, heads)
        s_c = jnp.einsum("bqd,bdw->bqw", lhs, km, preferred_element_type=F32)
        s_n = jnp.einsum("bqd,bkd->bqk", lhs, kn.astype(BF16), preferred_element_type=F32)
        s_c = jnp.where(mask_c, s_c, -jnp.inf)
        s_n = jnp.where(mask_n, s_n, -jnp.inf)
        sink = jnp.full((nq, 1), sinks_ref[heads[0]], F32)
        for n in range(1, len(heads)):
            sink = jnp.where(head_row == n, sinks_ref[heads[n]], sink)
        sink = sink[None]
        m = jnp.maximum(jnp.maximum(jnp.max(s_c, axis=-1, keepdims=True),
                                    jnp.max(s_n, axis=-1, keepdims=True)), sink)
        e_c, e_n = jnp.exp(s_c - m), jnp.exp(s_n - m)
        inv = 1.0 / (jnp.sum(e_c, axis=-1, keepdims=True) + jnp.sum(e_n, axis=-1, keepdims=True)
                     + jnp.exp(sink - m))
        oh = (jnp.einsum("bqw,bdw->bqd", e_c.astype(BF16), vm, preferred_element_type=F32)
              + jnp.einsum("bqk,bkd->bqd", e_n.astype(BF16), vn.astype(BF16),
                           preferred_element_type=F32)) * inv
        for n, hd in enumerate(heads):
            o[hd] = oh[:, n * t:(n + 1) * t, :]
    lo = _lo_half()[None]
    att = jnp.concatenate([jnp.where(lo, o[2 * c], o[2 * c + 1]) for c in range(4)], axis=-1)

    sv = jnp.zeros((bb, t, GM_WIDTH), F32)
    for si in range(t):
        sv = sv + wsx_ref[si][None] * gv3[:, si:si + 1, :]
    sgu = u3 * (sv + bexp_ref[...][None])

    mixcat = jnp.concatenate([att, sgu], axis=-1).reshape(bb * t, D_MODEL).astype(BF16)
    mix = jnp.dot(mixcat, w_o_ref[...], preferred_element_type=F32).reshape(bb, t, D_MODEL)
    y1 = _norm(ALPHA * x3 + mod[2] * mix) * g1_ref[...] + b1_ref[...]

    hf = (_norm(y1) * (1.0 + mod[4]) + mod[3]).reshape(bb * t, D_MODEL).astype(BF16)
    f = _swiglu(hf, wg_ref, wu_ref, wd_ref).reshape(bb, t, D_MODEL)
    y_ref[...] = _norm(ALPHA * y1 + mod[5] * f) * g2_ref[...] + b2_ref[...]


def _sample_layer(x, mod, sinks, tabs, kt, vt, w_in_b, wsx, bexp_t, sgu_g, sgu_b, w_o_b,
                  g1, b1, wg, wu, wd, g2, b2, bb):
    n, t, _ = x.shape
    wb = kt.shape[2]
    row3 = lambda r, c: pl.BlockSpec((bb, r, c), lambda b: (b, 0, 0))
    consts = (*tabs, )
    consts2 = (w_in_b, wsx, bexp_t, sgu_g, sgu_b, w_o_b, g1, b1, wg, wu, wd, g2, b2)
    return pl.pallas_call(
        functools.partial(_sample_kernel, bb=bb, t=t),
        grid=(n // bb,),
        in_specs=[
            pl.BlockSpec(memory_space=pltpu.SMEM),
            row3(t, D_MODEL), pl.BlockSpec((6, bb, D_MODEL), lambda b: (0, b, 0)),
            *[_const_spec(a.shape) for a in consts],
            row3(KV_WIDTH, wb), row3(KV_WIDTH, wb),
            *[_const_spec(a.shape) for a in consts2],
        ],
        out_specs=[row3(t, D_MODEL), row3(KV_WIDTH, wb), row3(KV_WIDTH, wb), row3(t, GM_WIDTH)],
        out_shape=[
            jax.ShapeDtypeStruct((n, t, D_MODEL), F32),
            jax.ShapeDtypeStruct((n, KV_WIDTH, wb), F32),
            jax.ShapeDtypeStruct((n, KV_WIDTH, wb), F32),
            jax.ShapeDtypeStruct((n, t, GM_WIDTH), F32),
        ],
        compiler_params=pltpu.CompilerParams(
            dimension_semantics=("arbitrary",), vmem_limit_bytes=VMEM_LIMIT_BYTES),
        name="sample_layer",
    )(sinks, x, mod, *consts, kt, vt, *consts2)


def _lane_freqs():
    half = ROT_DIM // 2
    inv_freq = np.power(np.float32(ROPE_THETA), -np.arange(half, dtype=np.float32) * np.float32(2.0 / ROT_DIM))
    per_head = np.concatenate([inv_freq, inv_freq, np.zeros((HEAD_DIM - ROT_DIM,), np.float32)])
    return np.concatenate([per_head, per_head]).astype(np.float32)


def _angle_tables(pos):
    ang = np.asarray(pos, np.float32)[:, None] * _lane_freqs()[None, :]
    return jnp.asarray(np.cos(ang), F32), jnp.asarray(np.sin(ang), F32)


def _to_feature_major(win):
    n, w = win.shape[0], win.shape[1]
    return jnp.transpose(win, (0, 2, 3, 1)).reshape(n, KV_WIDTH, w)


def _from_feature_major(win_t):
    n, _, w = win_t.shape
    return jnp.transpose(win_t.reshape(n, KV_HEADS, HEAD_DIM, w), (0, 3, 1, 2))


def kernel(x_prompt, x_sample, cache_k_win, cache_v_win, c_prompt, c_sample, w_ada, b_ada, w_in, attn_sinks, sgu_ln_g, sgu_ln_b, w_s, b_s, w_o, ln1_g, ln1_b, w_gate, w_up, w_down, ln2_g, ln2_b):
    bsz, seq, _ = x_prompt.shape
    nsm, t, _ = x_sample.shape
    tq, bb = 256, 32
    tabs_p = (*_angle_tables(np.arange(tq)), *_angle_tables(np.arange(seq // tq) * tq))
    tabs_s = _angle_tables(np.tile(PAST_LEN + np.arange(t), bb))

    yp, ys = x_prompt, x_sample
    kwp, vwp, kws, vws, sgv = [], [], [], [], []
    for l in range(DEPTH):
        w_in_b, w_o_b, wg, wu, wd, wsp = _prep_weights(w_in, w_o, w_gate, w_up, w_down, w_s, l)
        bexp = jnp.repeat(b_s[l].T, HEAD_DIM, axis=1)
        wsx = jnp.repeat(jnp.transpose(jnp.tril(w_s[l, :, :t, :t]), (2, 1, 0)), HEAD_DIM, axis=-1)
        sinks = attn_sinks[l]
        sg, sb = sgu_ln_g[l][None, :], sgu_ln_b[l][None, :]
        g1, b1 = ln1_g[l][None, :], ln1_b[l][None, :]
        g2, b2 = ln2_g[l][None, :], ln2_b[l][None, :]

        mod_s, mod_p = _ada(c_sample, c_prompt, w_ada, b_ada, l)

        yp, kw, vw = _prompt_layer(yp, mod_p, sinks, tabs_p, w_in_b, wsp, bexp, sg, sb, w_o_b,
                                   g1, b1, wg, wu, wd, g2, b2, tq)
        kwp.append(_from_feature_major(kw))
        vwp.append(_from_feature_major(vw))

        ys, kn, vn, gvs = _sample_layer(
            ys, mod_s, sinks, tabs_s, _to_feature_major(cache_k_win[l]),
            _to_feature_major(cache_v_win[l]), w_in_b, wsx, bexp[:t], sg, sb, w_o_b, g1, b1, wg, wu, wd, g2, b2, bb)
        kws.append(_from_feature_major(kn))
        vws.append(_from_feature_major(vn))
        sgv.append(gvs)
    return (yp, ys, jnp.stack(kwp, axis=0), jnp.stack(vwp, axis=0), jnp.stack(kws, axis=0),
            jnp.stack(vws, axis=0), jnp.stack(sgv, axis=0))
```

```python
import functools

import jax
import jax.numpy as jnp
import numpy as np
from jax import lax
from jax.experimental import pallas as pl
from jax.experimental.pallas import tpu as pltpu

D_MODEL = 1024
HEAD_DIM = 64
ATT_HEADS = 8
KV_HEADS = 2
Q_PER_KV = ATT_HEADS // KV_HEADS
ATT_WIDTH = ATT_HEADS * HEAD_DIM
KV_WIDTH = KV_HEADS * HEAD_DIM
GM_HEADS = 8
GM_WIDTH = GM_HEADS * HEAD_DIM
IN_WIDTH = ATT_WIDTH + 2 * KV_WIDTH + 2 * GM_WIDTH
D_FF = 2816
WINDOW = 128
CHUNK = 128
PAST_LEN = 16384
ROPE_THETA = 500000.0
ROT_DIM = HEAD_DIM // 4
DEPTH = 1
ALPHA = (2 * DEPTH) ** 0.25
LN_EPS = 1e-5
ATT_SCALE = HEAD_DIM ** -0.5

LANES = 128
VMEM_LIMIT_BYTES = 56 * 1024 * 1024
FF_SPLIT = 1280

BF16 = jnp.bfloat16
F32 = jnp.float32

HEADS_STRAIGHT = tuple(h for h in range(ATT_HEADS) if h % 2 == h // Q_PER_KV)
HEADS_SWAPPED = tuple(h for h in range(ATT_HEADS) if h % 2 != h // Q_PER_KV)


def _norm(x):
    mu = jnp.mean(x, axis=-1, keepdims=True)
    xc = x - mu
    var = jnp.mean(xc * xc, axis=-1, keepdims=True)
    return xc * lax.rsqrt(var + LN_EPS)


def _lo_half():
    return lax.broadcasted_iota(jnp.int32, (1, LANES), 1) < HEAD_DIM


def _rotary_lane_signs():
    half = ROT_DIM // 2
    l = lax.broadcasted_iota(jnp.int32, (1, LANES), 1) % HEAD_DIM
    up = jnp.where(l < half, -1.0, 0.0)
    dn = jnp.where((l >= half) & (l < ROT_DIM), 1.0, 0.0)
    return up, dn


def _rotary(x, cos, sin_up, sin_dn):
    half = ROT_DIM // 2
    return (x * cos + pltpu.roll(x, LANES - half, 1) * sin_up
            + pltpu.roll(x, half, 1) * sin_dn)


def _swap_halves(x):
    return pltpu.roll(x, HEAD_DIM, x.ndim - 1)


def _masked_heads(qcols, heads):
    lo = _lo_half()
    parts = []
    for h in heads:
        qc = qcols[h // 2]
        parts.append(jnp.where(lo, qc, 0.0) if h % 2 == 0 else jnp.where(lo, 0.0, qc))
    return jnp.concatenate(parts, axis=-2).astype(BF16)


PREP_ROWS = 128
PREP_SLOTS = 3


def _load_weight_bf16(w_hbm, dst_ref, stage_ref, sem_ref, scaled_cols=0):
    rows, cols = dst_ref.shape
    nchunk = rows // PREP_ROWS

    def chunk_copy(c, slot):
        return pltpu.make_async_copy(
            w_hbm.at[pl.ds(c * PREP_ROWS, PREP_ROWS), :],
            stage_ref.at[slot, :, pl.ds(0, cols)], sem_ref.at[slot])

    for c in range(min(PREP_SLOTS, nchunk)):
        chunk_copy(c, c).start()
    col = lax.broadcasted_iota(jnp.int32, (1, cols), 1)
    scale = jnp.where(col < scaled_cols, ATT_SCALE, 1.0)

    def body(c, carry):
        slot = c % PREP_SLOTS
        chunk_copy(c, slot).wait()
        v = stage_ref[slot, :, pl.ds(0, cols)]
        dst_ref[pl.ds(pl.multiple_of(c * PREP_ROWS, PREP_ROWS), PREP_ROWS), :] = (v * scale).astype(BF16)

        @pl.when(c + PREP_SLOTS < nchunk)
        def _():
            chunk_copy(c + PREP_SLOTS, slot).start()

        return carry

    lax.fori_loop(0, nchunk, body, 0)


ADA_K_CHUNK = 256


def _ada_kernel(cs_ref, cp_ref, w_ref, b_ref, os_ref, op_ref):
    nvec = os_ref.shape[0]
    vec_cols = [slice(r * D_MODEL, (r + 1) * D_MODEL) for r in range(nvec)]

    @pl.when(pl.program_id(0) == 0)
    def _():
        for o_ref in (os_ref, op_ref):
            for r, cols in enumerate(vec_cols):
                o_ref[r] = jnp.broadcast_to(b_ref[:, cols], o_ref.shape[1:])

    w = w_ref[...].astype(BF16)
    for c_ref, o_ref in ((cs_ref, os_ref), (cp_ref, op_ref)):
        c = c_ref[...]
        a = (c * jax.nn.sigmoid(c)).astype(BF16)
        part = jnp.dot(a, w, preferred_element_type=F32)
        for r, cols in enumerate(vec_cols):
            o_ref[r] += part[:, cols]


def _ada(c_sample, c_prompt, w_ada, b_ada, layer):
    n = w_ada.shape[2]
    nvec = n // D_MODEL
    kchunk = lambda c: pl.BlockSpec((c.shape[0], ADA_K_CHUNK), lambda k: (0, k))
    acc = lambda c: pl.BlockSpec((nvec, c.shape[0], D_MODEL), lambda k: (0, 0, 0))
    return pl.pallas_call(
        _ada_kernel,
        grid=(D_MODEL // ADA_K_CHUNK,),
        in_specs=[
            kchunk(c_sample), kchunk(c_prompt),
            pl.BlockSpec((None, ADA_K_CHUNK, n), lambda k: (layer, k, 0)),
            pl.BlockSpec((1, n), lambda k: (layer, 0)),
        ],
        out_specs=[acc(c_sample), acc(c_prompt)],
        out_shape=[jax.ShapeDtypeStruct((nvec, c.shape[0], D_MODEL), F32) for c in (c_sample, c_prompt)],
        compiler_params=pltpu.CompilerParams(dimension_semantics=("arbitrary",)),
        name="ada",
    )(c_sample, c_prompt, w_ada, b_ada)


def _attn_scores(qcols, kcat, kcat_sw):
    dn = (((1,), (1,)), ((), ()))
    s_a = lax.dot_general(_masked_heads(qcols, HEADS_STRAIGHT), kcat, dn, preferred_element_type=F32)
    s_b = lax.dot_general(_masked_heads(qcols, HEADS_SWAPPED), kcat_sw, dn, preferred_element_type=F32)
    return s_a, s_b


def _softmax_weights(s, first_key, sinks_ref, heads):
    row = lax.broadcasted_iota(jnp.int32, (WINDOW, 2 * WINDOW), 0)
    col = lax.broadcasted_iota(jnp.int32, (WINDOW, 2 * WINDOW), 1)
    mask = (col > row) & (col <= row + WINDOW) & (col >= first_key)
    es, invs = [], []
    for n, h in enumerate(heads):
        sink = sinks_ref[h]
        sh = jnp.where(mask, s[n * WINDOW:(n + 1) * WINDOW], -jnp.inf)
        m = jnp.maximum(jnp.max(sh, axis=-1, keepdims=True), sink)
        e = jnp.exp(sh - m)
        den = jnp.sum(e, axis=-1, keepdims=True) + jnp.exp(sink - m)
        es.append(e.astype(BF16))
        invs.append(1.0 / den)
    return jnp.concatenate(es, axis=0), invs


def _attn_merge(o_a, inv_a, o_b, inv_b):
    o = {}
    for heads, oo, inv in ((HEADS_STRAIGHT, o_a, inv_a), (HEADS_SWAPPED, o_b, inv_b)):
        for n, h in enumerate(heads):
            o[h] = oo[n * WINDOW:(n + 1) * WINDOW] * inv[n]
    lo = _lo_half()
    return jnp.concatenate([jnp.where(lo, o[2 * c], o[2 * c + 1]) for c in range(4)], axis=1)


def _prompt_sgu(uj, gvj, wsp_ref, bexp):
    lo = _lo_half()
    svs = []
    for p in range(4):
        g = gvj[:, p * LANES:(p + 1) * LANES]
        rhs = jnp.concatenate([jnp.where(lo, g, 0.0), jnp.where(lo, 0.0, g)], axis=0).astype(BF16)
        svs.append(jnp.dot(wsp_ref[p], rhs, preferred_element_type=F32))
    sv = jnp.concatenate(svs, axis=1)
    return uj * (sv + bexp)


def _swiglu(hf, wg_ref, wu_ref, wd_ref):
    f = None
    for lo, hi in ((0, FF_SPLIT), (FF_SPLIT, D_FF)):
        g = jnp.dot(hf, wg_ref[:, lo:hi], preferred_element_type=F32)
        up = jnp.dot(hf, wu_ref[:, lo:hi], preferred_element_type=F32)
        a = (g * jax.nn.sigmoid(g) * up).astype(BF16)
        d = jnp.dot(a, wd_ref[lo:hi, :], preferred_element_type=F32)
        f = d if f is None else f + d
    return f


def _prompt_kernel(sinks_ref, x_ref, xp_ref, mod_ref, cr_ref, sr_ref, cb_ref, sb_ref,
                   w_in_hbm, w_o_hbm, wg_hbm, wu_hbm, wd_hbm, ws_ref,
                   bexp_ref, sg_ref, sb2_ref, g1_ref, b1_ref, g2_ref, b2_ref,
                   y_ref, kwin_ref, vwin_ref, w_in_out, w_o_out, wg_out, wu_out, wd_out,
                   w_in_ref, w_o_ref, wg_ref, wu_ref, wd_ref, wsp_ref, stage_ref, load_sem, out_sem,
                   mixcat_ref, kprev_ref, vprev_ref, kprev_sw_ref, vprev_sw_ref,
                   *, tq, nb, nblocks, layer):
    s = pl.program_id(0)
    sm = jnp.minimum(s, nblocks - 1)
    b, i = sm // nb, sm % nb
    bp = jnp.maximum(s - 1, 0) // nb
    prev_refs = (kprev_ref, vprev_ref, kprev_sw_ref, vprev_sw_ref)
    exports = ((w_in_ref, w_in_out), (w_o_ref, w_o_out), (wg_ref, wg_out), (wu_ref, wu_out),
               (wd_ref, wd_out))

    def export_copy(n):
        return pltpu.make_async_copy(exports[n][0], exports[n][1], out_sem.at[n])

    @pl.when(s == 0)
    def _():
        for r in prev_refs:
            r[...] = jnp.zeros_like(r)
        mixcat_ref[...] = jnp.zeros_like(mixcat_ref)
        _load_weight_bf16(w_in_hbm.at[layer], w_in_ref, stage_ref, load_sem, scaled_cols=ATT_WIDTH)
        for hbm, (dst, _) in zip((w_o_hbm, wg_hbm, wu_hbm, wd_hbm), exports[1:]):
            _load_weight_bf16(hbm.at[layer], dst, stage_ref, load_sem)
        for n in range(len(exports)):
            export_copy(n).start()
        row = lax.broadcasted_iota(jnp.int32, (CHUNK, CHUNK), 0)
        col = lax.broadcasted_iota(jnp.int32, (CHUNK, CHUNK), 1)
        for hd in range(GM_HEADS):
            wsp_ref[hd // 2, :, (hd % 2) * CHUNK:(hd % 2 + 1) * CHUNK] = jnp.where(
                col <= row, ws_ref[hd], 0.0).astype(BF16)

    @pl.when(s == nblocks)
    def _():
        for n in range(len(exports)):
            export_copy(n).wait()

    modm = [mod_ref[r, pl.ds(b, 1), :] for r in range(2)]
    modp = [mod_ref[r, pl.ds(bp, 1), :] for r in range(6)]

    mix = jnp.dot(mixcat_ref[...], w_o_ref[...], preferred_element_type=F32)

    x = x_ref[...]
    h = (_norm(x) * (1.0 + modm[1]) + modm[0]).astype(BF16)
    z = jnp.dot(h, w_in_ref[...], preferred_element_type=F32)

    y1 = _norm(ALPHA * xp_ref[...] + modp[2] * mix) * g1_ref[...] + b1_ref[...]
    hf = (_norm(y1) * (1.0 + modp[4]) + modp[3]).astype(BF16)
    acts = []
    for cols in (slice(0, FF_SPLIT), slice(FF_SPLIT, D_FF)):
        g = jnp.dot(hf, wg_ref[:, cols], preferred_element_type=F32)
        up = jnp.dot(hf, wu_ref[:, cols], preferred_element_type=F32)
        acts.append((g * jax.nn.sigmoid(g) * up).astype(BF16))

    cb, sb = cb_ref[pl.ds(i, 1), :], sb_ref[pl.ds(i, 1), :]
    cr, sr = cr_ref[...], sr_ref[...]
    cos = cb * cr - sb * sr
    sin = sb * cr + cb * sr
    sgn_up, sgn_dn = _rotary_lane_signs()
    sup, sdn = sin * sgn_up, sin * sgn_dn
    q = [_rotary(z[:, c * LANES:(c + 1) * LANES], cos, sup, sdn) for c in range(4)]
    k = _rotary(z[:, ATT_WIDTH:ATT_WIDTH + KV_WIDTH], cos, sup, sdn)
    v = z[:, ATT_WIDTH + KV_WIDTH:ATT_WIDTH + 2 * KV_WIDTH]
    u = jax.nn.gelu(z[:, ATT_WIDTH + 2 * KV_WIDTH:ATT_WIDTH + 2 * KV_WIDTH + GM_WIDTH])
    gv = _norm(jax.nn.gelu(z[:, ATT_WIDTH + 2 * KV_WIDTH + GM_WIDTH:])) * sg_ref[...] + sb2_ref[...]
    forms = (k.astype(BF16), v.astype(BF16), _swap_halves(k).astype(BF16), _swap_halves(v).astype(BF16))
    kb, vb, kb_sw, vb_sw = forms

    bexp = bexp_ref[...]
    scores, values, sgus = [], [], []
    for j in range(tq // WINDOW):
        sl = slice(j * WINDOW, (j + 1) * WINDOW)
        if j == 0:
            kcat, vcat, kcat_sw, vcat_sw = (
                jnp.concatenate([p[...], c[sl]], axis=0) for p, c in zip(prev_refs, forms))
        else:
            sl2 = slice((j - 1) * WINDOW, (j + 1) * WINDOW)
            kcat, vcat, kcat_sw, vcat_sw = kb[sl2], vb[sl2], kb_sw[sl2], vb_sw[sl2]
        scores.append(_attn_scores([qc[sl] for qc in q], kcat, kcat_sw))
        values.append((vcat, vcat_sw))
        sgus.append(_prompt_sgu(u[sl], gv[sl], wsp_ref, bexp))

    f = (jnp.dot(acts[0], wd_ref[0:FF_SPLIT, :], preferred_element_type=F32)
         + jnp.dot(acts[1], wd_ref[FF_SPLIT:D_FF, :], preferred_element_type=F32))
    y_ref[...] = _norm(ALPHA * y1 + modp[5] * f) * g2_ref[...] + b2_ref[...]

    for j in range(tq // WINDOW):
        first_key = jnp.where(i > 0, 0, WINDOW) if j == 0 else 0
        p_a, inv_a = _softmax_weights(scores[j][0], first_key, sinks_ref, HEADS_STRAIGHT)
        p_b, inv_b = _softmax_weights(scores[j][1], first_key, sinks_ref, HEADS_SWAPPED)
        o_a = jnp.dot(p_a, values[j][0], preferred_element_type=F32)
        o_b = jnp.dot(p_b, values[j][1], preferred_element_type=F32)
        att = _attn_merge(o_a, inv_a, o_b, inv_b)
        mixcat_ref[j * WINDOW:(j + 1) * WINDOW, :] = jnp.concatenate([att, sgus[j]], axis=1).astype(BF16)

    for r, fm in zip(prev_refs, forms):
        r[...] = fm[tq - WINDOW:]
    kwin_ref[...] = k[tq - WINDOW:].T
    vwin_ref[...] = v[tq - WINDOW:].T


def _const_spec(shape):
    nd = len(shape)
    return pl.BlockSpec(shape, lambda *_: (0,) * nd)


def _prompt_layer(x, mod, sinks, tabs, weights, w_s, bexp, sgu_g, sgu_b, g1, b1, g2, b2, tq, layer):
    bsz, seq, _ = x.shape
    nb = seq // tq
    nblocks = bsz * nb
    carry = pltpu.VMEM((WINDOW, KV_WIDTH), BF16)
    wshapes = [w.shape[1:] for w in weights]
    widest = max(c for _, c in wshapes)
    assert all(r % PREP_ROWS == 0 for r, _ in wshapes)

    def this_block(s):
        sm = jnp.minimum(s, nblocks - 1)
        return sm // nb, sm % nb, 0

    def prev_block(s):
        sp = jnp.maximum(s - 1, 0)
        return sp // nb, sp % nb, 0

    consts = (bexp, sgu_g, sgu_b, g1, b1, g2, b2)
    win_spec = pl.BlockSpec((None, KV_WIDTH, WINDOW),
                            lambda s: (jnp.minimum(s, nblocks - 1) // nb, 0, 0))
    hbm = pl.BlockSpec(memory_space=pl.ANY)
    return pl.pallas_call(
        functools.partial(_prompt_kernel, tq=tq, nb=nb, nblocks=nblocks, layer=layer),
        grid=(nblocks + 1,),
        in_specs=[
            pl.BlockSpec(memory_space=pltpu.SMEM),
            pl.BlockSpec((None, tq, D_MODEL), this_block),
            pl.BlockSpec((None, tq, D_MODEL), prev_block),
            _const_spec(mod.shape),
            *[_const_spec(c.shape) for c in tabs],
            *[hbm for _ in weights],
            pl.BlockSpec((None,) + w_s.shape[1:], lambda s: (layer, 0, 0, 0)),
            *[_const_spec(c.shape) for c in consts],
        ],
        out_specs=[pl.BlockSpec((None, tq, D_MODEL), prev_block), win_spec, win_spec,
                   *[hbm for _ in weights]],
        out_shape=[
            jax.ShapeDtypeStruct((bsz, seq, D_MODEL), F32),
            jax.ShapeDtypeStruct((bsz, KV_WIDTH, WINDOW), F32),
            jax.ShapeDtypeStruct((bsz, KV_WIDTH, WINDOW), F32),
            *[jax.ShapeDtypeStruct(shp, BF16) for shp in wshapes],
        ],
        scratch_shapes=[
            *[pltpu.VMEM(shp, BF16) for shp in wshapes],
            pltpu.VMEM((GM_HEADS // 2, CHUNK, 2 * CHUNK), BF16),
            pltpu.VMEM((PREP_SLOTS, PREP_ROWS, widest), F32),
            pltpu.SemaphoreType.DMA((PREP_SLOTS,)),
            pltpu.SemaphoreType.DMA((len(weights),)),
            pltpu.VMEM((tq, D_MODEL), BF16), carry, carry, carry, carry,
        ],
        compiler_params=pltpu.CompilerParams(
            dimension_semantics=("arbitrary",), vmem_limit_bytes=VMEM_LIMIT_BYTES),
        name="prompt_layer",
    )(sinks, x, x, mod, *tabs, *weights, w_s, *consts)


def _shifted_window(old, new2, bb, t):
    wb = old.shape[2]
    lane = lax.broadcasted_iota(jnp.int32, (1, wb), 1)
    shifted = pltpu.roll(old.reshape(bb * KV_WIDTH, wb), wb - t, 1).reshape(bb, KV_WIDTH, wb)
    new_t = new2.T
    outs = []
    for n in range(bb):
        tile = new_t[:, (n * t // LANES) * LANES:(n * t // LANES + 1) * LANES]
        tail = pltpu.roll(tile, (wb - t - (n * t) % LANES) % LANES, 1)
        outs.append(jnp.where(lane >= wb - t, tail, shifted[n]))
    return outs


def _sample_kernel(sinks_ref, x_ref, mod_ref, cos_ref, sin_ref, kt_ref, vt_ref,
                   w_in_ref, wsx_ref, bexp_ref, sg_ref, sb_ref, w_o_ref, g1_ref, b1_ref,
                   wg_ref, wu_ref, wd_ref, g2_ref, b2_ref,
                   y_ref, ktn_ref, vtn_ref, gv_ref, *, bb, t):
    x3 = x_ref[...]
    mod = [mod_ref[r][:, None, :] for r in range(6)]
    h = (_norm(x3) * (1.0 + mod[1]) + mod[0]).reshape(bb * t, D_MODEL).astype(BF16)
    z = jnp.dot(h, w_in_ref[...], preferred_element_type=F32)

    cos, sin = cos_ref[...], sin_ref[...]
    sgn_up, sgn_dn = _rotary_lane_signs()
    sup, sdn = sin * sgn_up, sin * sgn_dn
    qcols = [_rotary(z[:, c * LANES:(c + 1) * LANES], cos, sup, sdn).reshape(bb, t, LANES)
             for c in range(4)]
    k2 = _rotary(z[:, ATT_WIDTH:ATT_WIDTH + KV_WIDTH], cos, sup, sdn)
    v2 = z[:, ATT_WIDTH + KV_WIDTH:ATT_WIDTH + 2 * KV_WIDTH]
    u3 = jax.nn.gelu(z[:, ATT_WIDTH + 2 * KV_WIDTH:ATT_WIDTH + 2 * KV_WIDTH + GM_WIDTH]
                     ).reshape(bb, t, GM_WIDTH)
    gv = _norm(jax.nn.gelu(z[:, ATT_WIDTH + 2 * KV_WIDTH + GM_WIDTH:])) * sg_ref[...] + sb_ref[...]
    gv3 = gv.reshape(bb, t, GM_WIDTH)
    gv_ref[...] = gv3

    kt, vt = kt_ref[...], vt_ref[...]
    wb = kt.shape[2]
    for n, (ko, vo) in enumerate(zip(_shifted_window(kt, k2, bb, t), _shifted_window(vt, v2, bb, t))):
        ktn_ref[n] = ko
        vtn_ref[n] = vo

    nq = Q_PER_KV * t
    tok = lax.broadcasted_iota(jnp.int32, (nq, 1), 0) % t
    mask_c = (lax.broadcasted_iota(jnp.int32, (nq, wb), 1) + (WINDOW - wb) > tok)[None]
    mask_n = (lax.broadcasted_iota(jnp.int32, (nq, t), 1) <= tok)[None]
    kb, vb = kt.astype(BF16), vt.astype(BF16)
    swap_rows = lambda a: jnp.concatenate([a[:, HEAD_DIM:, :], a[:, :HEAD_DIM, :]], axis=1)
    k3, v3 = k2.reshape(bb, t, KV_WIDTH), v2.reshape(bb, t, KV_WIDTH)
    head_row = lax.broadcasted_iota(jnp.int32, (nq, 1), 0) // t
    o = {}
    for heads, km, vm, kn, vn in (
            (HEADS_STRAIGHT, kb, vb, k3, v3),
            (HEADS_SWAPPED, swap_rows(kb), swap_rows(vb), _swap_halves(k3), _swap_halves(v3))):
        lhs = _masked_heads(qcols, heads)
        s_c = jnp.einsum("bqd,bdw->bqw", lhs, km, preferred_element_type=F32)
        s_n = jnp.einsum("bqd,bkd->bqk", lhs, kn.astype(BF16), preferred_element_type=F32)
        s_c = jnp.where(mask_c, s_c, -jnp.inf)
        s_n = jnp.where(mask_n, s_n, -jnp.inf)
        sink = jnp.full((nq, 1), sinks_ref[heads[0]], F32)
        for n in range(1, len(heads)):
            sink = jnp.where(head_row == n, sinks_ref[heads[n]], sink)
        sink = sink[None]
        m = jnp.maximum(jnp.maximum(jnp.max(s_c, axis=-1, keepdims=True),
                                    jnp.max(s_n, axis=-1, keepdims=True)), sink)
        e_c, e_n = jnp.exp(s_c - m), jnp.exp(s_n - m)
        inv = 1.0 / (jnp.sum(e_c, axis=-1, keepdims=True) + jnp.sum(e_n, axis=-1, keepdims=True)
                     + jnp.exp(sink - m))
        oh = (jnp.einsum("bqw,bdw->bqd", e_c.astype(BF16), vm, preferred_element_type=F32)
              + jnp.einsum("bqk,bkd->bqd", e_n.astype(BF16), vn.astype(BF16),
                           preferred_element_type=F32)) * inv
        for n, hd in enumerate(heads):
            o[hd] = oh[:, n * t:(n + 1) * t, :]
    lo = _lo_half()[None]
    att = jnp.concatenate([jnp.where(lo, o[2 * c], o[2 * c + 1]) for c in range(4)], axis=-1)

    sv = jnp.zeros((bb, t, GM_WIDTH), F32)
    for si in range(t):
        sv = sv + wsx_ref[si][None] * gv3[:, si:si + 1, :]
    sgu = u3 * (sv + bexp_ref[...][None])

    mixcat = jnp.concatenate([att, sgu], axis=-1).reshape(bb * t, D_MODEL).astype(BF16)
    mix = jnp.dot(mixcat, w_o_ref[...], preferred_element_type=F32).reshape(bb, t, D_MODEL)
    y1 = _norm(ALPHA * x3 + mod[2] * mix) * g1_ref[...] + b1_ref[...]

    hf = (_norm(y1) * (1.0 + mod[4]) + mod[3]).reshape(bb * t, D_MODEL).astype(BF16)
    f = _swiglu(hf, wg_ref, wu_ref, wd_ref).reshape(bb, t, D_MODEL)
    y_ref[...] = _norm(ALPHA * y1 + mod[5] * f) * g2_ref[...] + b2_ref[...]


def _sample_layer(x, mod, sinks, tabs, kt, vt, w_in_b, wsx, bexp_t, sgu_g, sgu_b, w_o_b,
                  g1, b1, wg, wu, wd, g2, b2, bb):
    n, t, _ = x.shape
    wb = kt.shape[2]
    row3 = lambda r, c: pl.BlockSpec((bb, r, c), lambda b: (b, 0, 0))
    consts = (*tabs, )
    consts2 = (w_in_b, wsx, bexp_t, sgu_g, sgu_b, w_o_b, g1, b1, wg, wu, wd, g2, b2)
    return pl.pallas_call(
        functools.partial(_sample_kernel, bb=bb, t=t),
        grid=(n // bb,),
        in_specs=[
            pl.BlockSpec(memory_space=pltpu.SMEM),
            row3(t, D_MODEL), pl.BlockSpec((6, bb, D_MODEL), lambda b: (0, b, 0)),
            *[_const_spec(a.shape) for a in consts],
            row3(KV_WIDTH, wb), row3(KV_WIDTH, wb),
            *[_const_spec(a.shape) for a in consts2],
        ],
        out_specs=[row3(t, D_MODEL), row3(KV_WIDTH, wb), row3(KV_WIDTH, wb), row3(t, GM_WIDTH)],
        out_shape=[
            jax.ShapeDtypeStruct((n, t, D_MODEL), F32),
            jax.ShapeDtypeStruct((n, KV_WIDTH, wb), F32),
            jax.ShapeDtypeStruct((n, KV_WIDTH, wb), F32),
            jax.ShapeDtypeStruct((n, t, GM_WIDTH), F32),
        ],
        compiler_params=pltpu.CompilerParams(
            dimension_semantics=("arbitrary",), vmem_limit_bytes=VMEM_LIMIT_BYTES),
        name="sample_layer",
    )(sinks, x, mod, *consts, kt, vt, *consts2)


def _lane_freqs():
    half = ROT_DIM // 2
    inv_freq = np.power(np.float32(ROPE_THETA), -np.arange(half, dtype=np.float32) * np.float32(2.0 / ROT_DIM))
    per_head = np.concatenate([inv_freq, inv_freq, np.zeros((HEAD_DIM - ROT_DIM,), np.float32)])
    return np.concatenate([per_head, per_head]).astype(np.float32)


def _angle_tables(pos):
    ang = np.asarray(pos, np.float32)[:, None] * _lane_freqs()[None, :]
    return jnp.asarray(np.cos(ang), F32), jnp.asarray(np.sin(ang), F32)


def _to_feature_major(win):
    n, w = win.shape[0], win.shape[1]
    return jnp.transpose(win, (0, 2, 3, 1)).reshape(n, KV_WIDTH, w)


def _from_feature_major(win_t):
    n, _, w = win_t.shape
    return jnp.transpose(win_t.reshape(n, KV_HEADS, HEAD_DIM, w), (0, 3, 1, 2))


def kernel(x_prompt, x_sample, cache_k_win, cache_v_win, c_prompt, c_sample, w_ada, b_ada, w_in, attn_sinks, sgu_ln_g, sgu_ln_b, w_s, b_s, w_o, ln1_g, ln1_b, w_gate, w_up, w_down, ln2_g, ln2_b):
    bsz, seq, _ = x_prompt.shape
    nsm, t, _ = x_sample.shape
    tq, bb = 256, 32
    tabs_p = (*_angle_tables(np.arange(tq)), *_angle_tables(np.arange(seq // tq) * tq))
    tabs_s = _angle_tables(np.tile(PAST_LEN + np.arange(t), bb))

    yp, ys = x_prompt, x_sample
    kwp, vwp, kws, vws, sgv = [], [], [], [], []
    for l in range(DEPTH):
        bexp = jnp.repeat(b_s[l].T, HEAD_DIM, axis=1)
        wsx = jnp.repeat(jnp.transpose(jnp.tril(w_s[l, :, :t, :t]), (2, 1, 0)), HEAD_DIM, axis=-1)
        sinks = attn_sinks[l]
        sg, sb = sgu_ln_g[l][None, :], sgu_ln_b[l][None, :]
        g1, b1 = ln1_g[l][None, :], ln1_b[l][None, :]
        g2, b2 = ln2_g[l][None, :], ln2_b[l][None, :]

        mod_s, mod_p = _ada(c_sample, c_prompt, w_ada, b_ada, l)

        yp, kw, vw, w_in_b, w_o_b, wg, wu, wd = _prompt_layer(
            yp, mod_p, sinks, tabs_p, (w_in, w_o, w_gate, w_up, w_down), w_s, bexp, sg, sb,
            g1, b1, g2, b2, tq, l)
        kwp.append(_from_feature_major(kw))
        vwp.append(_from_feature_major(vw))

        ys, kn, vn, gvs = _sample_layer(
            ys, mod_s, sinks, tabs_s, _to_feature_major(cache_k_win[l]),
            _to_feature_major(cache_v_win[l]), w_in_b, wsx, bexp[:t], sg, sb, w_o_b, g1, b1, wg, wu, wd, g2, b2, bb)
        kws.append(_from_feature_major(kn))
        vws.append(_from_feature_major(vn))
        sgv.append(gvs)
    return (yp, ys, jnp.stack(kwp, axis=0), jnp.stack(vwp, axis=0), jnp.stack(kws, axis=0),
            jnp.stack(vws, axis=0), jnp.stack(sgv, axis=0))
```

```python
import functools

import jax
import jax.numpy as jnp
import numpy as np
from jax import lax
from jax.experimental import pallas as pl
from jax.experimental.pallas import tpu as pltpu

D_MODEL = 1024
HEAD_DIM = 64
ATT_HEADS = 8
KV_HEADS = 2
Q_PER_KV = ATT_HEADS // KV_HEADS
ATT_WIDTH = ATT_HEADS * HEAD_DIM
KV_WIDTH = KV_HEADS * HEAD_DIM
GM_HEADS = 8
GM_WIDTH = GM_HEADS * HEAD_DIM
IN_WIDTH = ATT_WIDTH + 2 * KV_WIDTH + 2 * GM_WIDTH
D_FF = 2816
WINDOW = 128
CHUNK = 128
PAST_LEN = 16384
ROPE_THETA = 500000.0
ROT_DIM = HEAD_DIM // 4
DEPTH = 1
ALPHA = (2 * DEPTH) ** 0.25
LN_EPS = 1e-5
ATT_SCALE = HEAD_DIM ** -0.5

LANES = 128
VMEM_LIMIT_BYTES = 56 * 1024 * 1024
FF_SPLIT = 1280

BF16 = jnp.bfloat16
F32 = jnp.float32

HEADS_STRAIGHT = tuple(h for h in range(ATT_HEADS) if h % 2 == h // Q_PER_KV)
HEADS_SWAPPED = tuple(h for h in range(ATT_HEADS) if h % 2 != h // Q_PER_KV)


def _norm(x):
    mu = jnp.mean(x, axis=-1, keepdims=True)
    xc = x - mu
    var = jnp.mean(xc * xc, axis=-1, keepdims=True)
    return xc * lax.rsqrt(var + LN_EPS)


def _lo_half():
    return lax.broadcasted_iota(jnp.int32, (1, LANES), 1) < HEAD_DIM


def _rotary_lane_signs():
    half = ROT_DIM // 2
    l = lax.broadcasted_iota(jnp.int32, (1, LANES), 1) % HEAD_DIM
    up = jnp.where(l < half, -1.0, 0.0)
    dn = jnp.where((l >= half) & (l < ROT_DIM), 1.0, 0.0)
    return up, dn


def _rotary(x, cos, sin_up, sin_dn):
    half = ROT_DIM // 2
    return (x * cos + pltpu.roll(x, LANES - half, 1) * sin_up
            + pltpu.roll(x, half, 1) * sin_dn)


def _swap_halves(x):
    return pltpu.roll(x, HEAD_DIM, x.ndim - 1)


def _masked_heads(qcols, heads):
    lo = _lo_half()
    parts = []
    for h in heads:
        qc = qcols[h // 2]
        parts.append(jnp.where(lo, qc, 0.0) if h % 2 == 0 else jnp.where(lo, 0.0, qc))
    return jnp.concatenate(parts, axis=-2).astype(BF16)


PREP_ROWS = 256
PREP_SLOTS = 4


def _load_weights_bf16(pairs, stage_ref, sem_ref, scaled_cols):
    chunks = [(n, r0) for n, (_, dst) in enumerate(pairs) for r0 in range(0, dst.shape[0], PREP_ROWS)]

    def chunk_copy(idx):
        n, r0 = chunks[idx]
        hbm, dst = pairs[n]
        slot = idx % PREP_SLOTS
        return pltpu.make_async_copy(
            hbm.at[pl.ds(r0, PREP_ROWS), :],
            stage_ref.at[slot, :, pl.ds(0, dst.shape[1])], sem_ref.at[slot])

    for idx in range(min(PREP_SLOTS, len(chunks))):
        chunk_copy(idx).start()
    for idx, (n, r0) in enumerate(chunks):
        dst = pairs[n][1]
        cols = dst.shape[1]
        chunk_copy(idx).wait()
        v = stage_ref[idx % PREP_SLOTS, :, pl.ds(0, cols)]
        if scaled_cols[n]:
            col = lax.broadcasted_iota(jnp.int32, (1, cols), 1)
            v = v * jnp.where(col < scaled_cols[n], ATT_SCALE, 1.0)
        dst[pl.ds(r0, PREP_ROWS), :] = v.astype(BF16)
        if idx + PREP_SLOTS < len(chunks):
            chunk_copy(idx + PREP_SLOTS).start()


ADA_K_CHUNK = 256


def _ada_kernel(cs_ref, cp_ref, w_ref, b_ref, os_ref, op_ref):
    nvec = os_ref.shape[0]
    vec_cols = [slice(r * D_MODEL, (r + 1) * D_MODEL) for r in range(nvec)]

    @pl.when(pl.program_id(0) == 0)
    def _():
        for o_ref in (os_ref, op_ref):
            for r, cols in enumerate(vec_cols):
                o_ref[r] = jnp.broadcast_to(b_ref[:, cols], o_ref.shape[1:])

    w = w_ref[...].astype(BF16)
    for c_ref, o_ref in ((cs_ref, os_ref), (cp_ref, op_ref)):
        c = c_ref[...]
        a = (c * jax.nn.sigmoid(c)).astype(BF16)
        part = jnp.dot(a, w, preferred_element_type=F32)
        for r, cols in enumerate(vec_cols):
            o_ref[r] += part[:, cols]


def _ada(c_sample, c_prompt, w_ada, b_ada, layer):
    n = w_ada.shape[2]
    nvec = n // D_MODEL
    kchunk = lambda c: pl.BlockSpec((c.shape[0], ADA_K_CHUNK), lambda k: (0, k))
    acc = lambda c: pl.BlockSpec((nvec, c.shape[0], D_MODEL), lambda k: (0, 0, 0))
    return pl.pallas_call(
        _ada_kernel,
        grid=(D_MODEL // ADA_K_CHUNK,),
        in_specs=[
            kchunk(c_sample), kchunk(c_prompt),
            pl.BlockSpec((None, ADA_K_CHUNK, n), lambda k: (layer, k, 0)),
            pl.BlockSpec((1, n), lambda k: (layer, 0)),
        ],
        out_specs=[acc(c_sample), acc(c_prompt)],
        out_shape=[jax.ShapeDtypeStruct((nvec, c.shape[0], D_MODEL), F32) for c in (c_sample, c_prompt)],
        compiler_params=pltpu.CompilerParams(dimension_semantics=("arbitrary",)),
        name="ada",
    )(c_sample, c_prompt, w_ada, b_ada)


def _attn_scores(qcols, kcat, kcat_sw):
    dn = (((1,), (1,)), ((), ()))
    s_a = lax.dot_general(_masked_heads(qcols, HEADS_STRAIGHT), kcat, dn, preferred_element_type=F32)
    s_b = lax.dot_general(_masked_heads(qcols, HEADS_SWAPPED), kcat_sw, dn, preferred_element_type=F32)
    return s_a, s_b


def _softmax_weights(s, first_key, sinks_ref, heads):
    row = lax.broadcasted_iota(jnp.int32, (WINDOW, 2 * WINDOW), 0)
    col = lax.broadcasted_iota(jnp.int32, (WINDOW, 2 * WINDOW), 1)
    mask = (col > row) & (col <= row + WINDOW) & (col >= first_key)
    es, invs = [], []
    for n, h in enumerate(heads):
        sink = sinks_ref[h]
        sh = jnp.where(mask, s[n * WINDOW:(n + 1) * WINDOW], -jnp.inf)
        m = jnp.maximum(jnp.max(sh, axis=-1, keepdims=True), sink)
        e = jnp.exp(sh - m)
        den = jnp.sum(e, axis=-1, keepdims=True) + jnp.exp(sink - m)
        es.append(e.astype(BF16))
        invs.append(1.0 / den)
    return jnp.concatenate(es, axis=0), invs


def _attn_merge(o_a, inv_a, o_b, inv_b):
    o = {}
    for heads, oo, inv in ((HEADS_STRAIGHT, o_a, inv_a), (HEADS_SWAPPED, o_b, inv_b)):
        for n, h in enumerate(heads):
            o[h] = oo[n * WINDOW:(n + 1) * WINDOW] * inv[n]
    lo = _lo_half()
    return jnp.concatenate([jnp.where(lo, o[2 * c], o[2 * c + 1]) for c in range(4)], axis=1)


def _prompt_sgu(uj, gvj, wsp_ref, bexp):
    lo = _lo_half()
    svs = []
    for p in range(4):
        g = gvj[:, p * LANES:(p + 1) * LANES]
        rhs = jnp.concatenate([jnp.where(lo, g, 0.0), jnp.where(lo, 0.0, g)], axis=0).astype(BF16)
        svs.append(jnp.dot(wsp_ref[p], rhs, preferred_element_type=F32))
    sv = jnp.concatenate(svs, axis=1)
    return uj * (sv + bexp)


def _swiglu(hf, wg_ref, wu_ref, wd_ref):
    f = None
    for lo, hi in ((0, FF_SPLIT), (FF_SPLIT, D_FF)):
        g = jnp.dot(hf, wg_ref[:, lo:hi], preferred_element_type=F32)
        up = jnp.dot(hf, wu_ref[:, lo:hi], preferred_element_type=F32)
        a = (g * jax.nn.sigmoid(g) * up).astype(BF16)
        d = jnp.dot(a, wd_ref[lo:hi, :], preferred_element_type=F32)
        f = d if f is None else f + d
    return f


def _prompt_kernel(sinks_ref, x_ref, xp_ref, mod_ref, cr_ref, sr_ref, cb_ref, sb_ref,
                   w_in_hbm, w_o_hbm, wg_hbm, wu_hbm, wd_hbm, ws_ref,
                   bexp_ref, sg_ref, sb2_ref, g1_ref, b1_ref, g2_ref, b2_ref,
                   y_ref, kwin_ref, vwin_ref, w_in_out, w_o_out, wg_out, wu_out, wd_out,
                   w_in_ref, w_o_ref, wg_ref, wu_ref, wd_ref, wsp_ref, stage_ref, load_sem, out_sem,
                   mixcat_ref, kprev_ref, vprev_ref, kprev_sw_ref, vprev_sw_ref,
                   *, tq, nb, nblocks, layer):
    s = pl.program_id(0)
    sm = jnp.minimum(s, nblocks - 1)
    b, i = sm // nb, sm % nb
    bp = jnp.maximum(s - 1, 0) // nb
    prev_refs = (kprev_ref, vprev_ref, kprev_sw_ref, vprev_sw_ref)
    exports = ((w_in_ref, w_in_out), (w_o_ref, w_o_out), (wg_ref, wg_out), (wu_ref, wu_out),
               (wd_ref, wd_out))

    def export_copy(n):
        return pltpu.make_async_copy(exports[n][0], exports[n][1], out_sem.at[n])

    @pl.when(s == 0)
    def _():
        for r in prev_refs:
            r[...] = jnp.zeros_like(r)
        mixcat_ref[...] = jnp.zeros_like(mixcat_ref)
        sources = (w_in_hbm, w_o_hbm, wg_hbm, wu_hbm, wd_hbm)
        _load_weights_bf16([(src.at[layer], dst) for src, (dst, _) in zip(sources, exports)],
                           stage_ref, load_sem, scaled_cols=(ATT_WIDTH, 0, 0, 0, 0))
        for n in range(len(exports)):
            export_copy(n).start()
        row = lax.broadcasted_iota(jnp.int32, (CHUNK, CHUNK), 0)
        col = lax.broadcasted_iota(jnp.int32, (CHUNK, CHUNK), 1)
        for hd in range(GM_HEADS):
            wsp_ref[hd // 2, :, (hd % 2) * CHUNK:(hd % 2 + 1) * CHUNK] = jnp.where(
                col <= row, ws_ref[hd], 0.0).astype(BF16)

    @pl.when(s == nblocks)
    def _():
        for n in range(len(exports)):
            export_copy(n).wait()

    modm = [mod_ref[r, pl.ds(b, 1), :] for r in range(2)]
    modp = [mod_ref[r, pl.ds(bp, 1), :] for r in range(6)]

    mix = jnp.dot(mixcat_ref[...], w_o_ref[...], preferred_element_type=F32)

    x = x_ref[...]
    h = (_norm(x) * (1.0 + modm[1]) + modm[0]).astype(BF16)
    z = jnp.dot(h, w_in_ref[...], preferred_element_type=F32)

    y1 = _norm(ALPHA * xp_ref[...] + modp[2] * mix) * g1_ref[...] + b1_ref[...]
    hf = (_norm(y1) * (1.0 + modp[4]) + modp[3]).astype(BF16)
    acts = []
    for cols in (slice(0, FF_SPLIT), slice(FF_SPLIT, D_FF)):
        g = jnp.dot(hf, wg_ref[:, cols], preferred_element_type=F32)
        up = jnp.dot(hf, wu_ref[:, cols], preferred_element_type=F32)
        acts.append((g * jax.nn.sigmoid(g) * up).astype(BF16))

    cb, sb = cb_ref[pl.ds(i, 1), :], sb_ref[pl.ds(i, 1), :]
    cr, sr = cr_ref[...], sr_ref[...]
    cos = cb * cr - sb * sr
    sin = sb * cr + cb * sr
    sgn_up, sgn_dn = _rotary_lane_signs()
    sup, sdn = sin * sgn_up, sin * sgn_dn
    q = [_rotary(z[:, c * LANES:(c + 1) * LANES], cos, sup, sdn) for c in range(4)]
    k = _rotary(z[:, ATT_WIDTH:ATT_WIDTH + KV_WIDTH], cos, sup, sdn)
    v = z[:, ATT_WIDTH + KV_WIDTH:ATT_WIDTH + 2 * KV_WIDTH]
    u = jax.nn.gelu(z[:, ATT_WIDTH + 2 * KV_WIDTH:ATT_WIDTH + 2 * KV_WIDTH + GM_WIDTH])
    gv = _norm(jax.nn.gelu(z[:, ATT_WIDTH + 2 * KV_WIDTH + GM_WIDTH:])) * sg_ref[...] + sb2_ref[...]
    forms = (k.astype(BF16), v.astype(BF16), _swap_halves(k).astype(BF16), _swap_halves(v).astype(BF16))
    kb, vb, kb_sw, vb_sw = forms

    bexp = bexp_ref[...]
    scores, values, sgus = [], [], []
    for j in range(tq // WINDOW):
        sl = slice(j * WINDOW, (j + 1) * WINDOW)
        if j == 0:
            kcat, vcat, kcat_sw, vcat_sw = (
                jnp.concatenate([p[...], c[sl]], axis=0) for p, c in zip(prev_refs, forms))
        else:
            sl2 = slice((j - 1) * WINDOW, (j + 1) * WINDOW)
            kcat, vcat, kcat_sw, vcat_sw = kb[sl2], vb[sl2], kb_sw[sl2], vb_sw[sl2]
        scores.append(_attn_scores([qc[sl] for qc in q], kcat, kcat_sw))
        values.append((vcat, vcat_sw))
        sgus.append(_prompt_sgu(u[sl], gv[sl], wsp_ref, bexp))

    f = (jnp.dot(acts[0], wd_ref[0:FF_SPLIT, :], preferred_element_type=F32)
         + jnp.dot(acts[1], wd_ref[FF_SPLIT:D_FF, :], preferred_element_type=F32))
    y_ref[...] = _norm(ALPHA * y1 + modp[5] * f) * g2_ref[...] + b2_ref[...]

    for j in range(tq // WINDOW):
        first_key = jnp.where(i > 0, 0, WINDOW) if j == 0 else 0
        p_a, inv_a = _softmax_weights(scores[j][0], first_key, sinks_ref, HEADS_STRAIGHT)
        p_b, inv_b = _softmax_weights(scores[j][1], first_key, sinks_ref, HEADS_SWAPPED)
        o_a = jnp.dot(p_a, values[j][0], preferred_element_type=F32)
        o_b = jnp.dot(p_b, values[j][1], preferred_element_type=F32)
        att = _attn_merge(o_a, inv_a, o_b, inv_b)
        mixcat_ref[j * WINDOW:(j + 1) * WINDOW, :] = jnp.concatenate([att, sgus[j]], axis=1).astype(BF16)

    for r, fm in zip(prev_refs, forms):
        r[...] = fm[tq - WINDOW:]
    kwin_ref[...] = k[tq - WINDOW:].T
    vwin_ref[...] = v[tq - WINDOW:].T


def _const_spec(shape):
    nd = len(shape)
    return pl.BlockSpec(shape, lambda *_: (0,) * nd)


def _prompt_layer(x, mod, sinks, tabs, weights, w_s, bexp, sgu_g, sgu_b, g1, b1, g2, b2, tq, layer):
    bsz, seq, _ = x.shape
    nb = seq // tq
    nblocks = bsz * nb
    carry = pltpu.VMEM((WINDOW, KV_WIDTH), BF16)
    wshapes = [w.shape[1:] for w in weights]
    widest = max(c for _, c in wshapes)
    assert all(r % PREP_ROWS == 0 for r, _ in wshapes)

    def this_block(s):
        sm = jnp.minimum(s, nblocks - 1)
        return sm // nb, sm % nb, 0

    def prev_block(s):
        sp = jnp.maximum(s - 1, 0)
        return sp // nb, sp % nb, 0

    consts = (bexp, sgu_g, sgu_b, g1, b1, g2, b2)
    win_spec = pl.BlockSpec((None, KV_WIDTH, WINDOW),
                            lambda s: (jnp.minimum(s, nblocks - 1) // nb, 0, 0))
    hbm = pl.BlockSpec(memory_space=pl.ANY)
    return pl.pallas_call(
        functools.partial(_prompt_kernel, tq=tq, nb=nb, nblocks=nblocks, layer=layer),
        grid=(nblocks + 1,),
        in_specs=[
            pl.BlockSpec(memory_space=pltpu.SMEM),
            pl.BlockSpec((None, tq, D_MODEL), this_block),
            pl.BlockSpec((None, tq, D_MODEL), prev_block),
            _const_spec(mod.shape),
            *[_const_spec(c.shape) for c in tabs],
            *[hbm for _ in weights],
            pl.BlockSpec((None,) + w_s.shape[1:], lambda s: (layer, 0, 0, 0)),
            *[_const_spec(c.shape) for c in consts],
        ],
        out_specs=[pl.BlockSpec((None, tq, D_MODEL), prev_block), win_spec, win_spec,
                   *[hbm for _ in weights]],
        out_shape=[
            jax.ShapeDtypeStruct((bsz, seq, D_MODEL), F32),
            jax.ShapeDtypeStruct((bsz, KV_WIDTH, WINDOW), F32),
            jax.ShapeDtypeStruct((bsz, KV_WIDTH, WINDOW), F32),
            *[jax.ShapeDtypeStruct(shp, BF16) for shp in wshapes],
        ],
        scratch_shapes=[
            *[pltpu.VMEM(shp, BF16) for shp in wshapes],
            pltpu.VMEM((GM_HEADS // 2, CHUNK, 2 * CHUNK), BF16),
            pltpu.VMEM((PREP_SLOTS, PREP_ROWS, widest), F32),
            pltpu.SemaphoreType.DMA((PREP_SLOTS,)),
            pltpu.SemaphoreType.DMA((len(weights),)),
            pltpu.VMEM((tq, D_MODEL), BF16), carry, carry, carry, carry,
        ],
        compiler_params=pltpu.CompilerParams(
            dimension_semantics=("arbitrary",), vmem_limit_bytes=VMEM_LIMIT_BYTES),
        name="prompt_layer",
    )(sinks, x, x, mod, *tabs, *weights, w_s, *consts)


def _shifted_window(old, new2, bb, t):
    wb = old.shape[2]
    lane = lax.broadcasted_iota(jnp.int32, (1, wb), 1)
    shifted = pltpu.roll(old.reshape(bb * KV_WIDTH, wb), wb - t, 1).reshape(bb, KV_WIDTH, wb)
    new_t = new2.T
    outs = []
    for n in range(bb):
        tile = new_t[:, (n * t // LANES) * LANES:(n * t // LANES + 1) * LANES]
        tail = pltpu.roll(tile, (wb - t - (n * t) % LANES) % LANES, 1)
        outs.append(jnp.where(lane >= wb - t, tail, shifted[n]))
    return outs


def _sample_kernel(sinks_ref, x_ref, mod_ref, cos_ref, sin_ref, kt_ref, vt_ref,
                   w_in_ref, wsx_ref, bexp_ref, sg_ref, sb_ref, w_o_ref, g1_ref, b1_ref,
                   wg_ref, wu_ref, wd_ref, g2_ref, b2_ref,
                   y_ref, ktn_ref, vtn_ref, gv_ref, *, bb, t):
    x3 = x_ref[...]
    mod = [mod_ref[r][:, None, :] for r in range(6)]
    h = (_norm(x3) * (1.0 + mod[1]) + mod[0]).reshape(bb * t, D_MODEL).astype(BF16)
    z = jnp.dot(h, w_in_ref[...], preferred_element_type=F32)

    cos, sin = cos_ref[...], sin_ref[...]
    sgn_up, sgn_dn = _rotary_lane_signs()
    sup, sdn = sin * sgn_up, sin * sgn_dn
    qcols = [_rotary(z[:, c * LANES:(c + 1) * LANES], cos, sup, sdn).reshape(bb, t, LANES)
             for c in range(4)]
    k2 = _rotary(z[:, ATT_WIDTH:ATT_WIDTH + KV_WIDTH], cos, sup, sdn)
    v2 = z[:, ATT_WIDTH + KV_WIDTH:ATT_WIDTH + 2 * KV_WIDTH]
    u3 = jax.nn.gelu(z[:, ATT_WIDTH + 2 * KV_WIDTH:ATT_WIDTH + 2 * KV_WIDTH + GM_WIDTH]
                     ).reshape(bb, t, GM_WIDTH)
    gv = _norm(jax.nn.gelu(z[:, ATT_WIDTH + 2 * KV_WIDTH + GM_WIDTH:])) * sg_ref[...] + sb_ref[...]
    gv3 = gv.reshape(bb, t, GM_WIDTH)
    gv_ref[...] = gv3

    kt, vt = kt_ref[...], vt_ref[...]
    wb = kt.shape[2]
    for n, (ko, vo) in enumerate(zip(_shifted_window(kt, k2, bb, t), _shifted_window(vt, v2, bb, t))):
        ktn_ref[n] = ko
        vtn_ref[n] = vo

    nq = Q_PER_KV * t
    tok = lax.broadcasted_iota(jnp.int32, (nq, 1), 0) % t
    mask_c = (lax.broadcasted_iota(jnp.int32, (nq, wb), 1) + (WINDOW - wb) > tok)[None]
    mask_n = (lax.broadcasted_iota(jnp.int32, (nq, t), 1) <= tok)[None]
    kb, vb = kt.astype(BF16), vt.astype(BF16)
    swap_rows = lambda a: jnp.concatenate([a[:, HEAD_DIM:, :], a[:, :HEAD_DIM, :]], axis=1)
    k3, v3 = k2.reshape(bb, t, KV_WIDTH), v2.reshape(bb, t, KV_WIDTH)
    head_row = lax.broadcasted_iota(jnp.int32, (nq, 1), 0) // t
    o = {}
    for heads, km, vm, kn, vn in (
            (HEADS_STRAIGHT, kb, vb, k3, v3),
            (HEADS_SWAPPED, swap_rows(kb), swap_rows(vb), _swap_halves(k3), _swap_halves(v3))):
        lhs = _masked_heads(qcols, heads)
        s_c = jnp.einsum("bqd,bdw->bqw", lhs, km, preferred_element_type=F32)
        s_n = jnp.einsum("bqd,bkd->bqk", lhs, kn.astype(BF16), preferred_element_type=F32)
        s_c = jnp.where(mask_c, s_c, -jnp.inf)
        s_n = jnp.where(mask_n, s_n, -jnp.inf)
        sink = jnp.full((nq, 1), sinks_ref[heads[0]], F32)
        for n in range(1, len(heads)):
            sink = jnp.where(head_row == n, sinks_ref[heads[n]], sink)
        sink = sink[None]
        m = jnp.maximum(jnp.maximum(jnp.max(s_c, axis=-1, keepdims=True),
                                    jnp.max(s_n, axis=-1, keepdims=True)), sink)
        e_c, e_n = jnp.exp(s_c - m), jnp.exp(s_n - m)
        inv = 1.0 / (jnp.sum(e_c, axis=-1, keepdims=True) + jnp.sum(e_n, axis=-1, keepdims=True)
                     + jnp.exp(sink - m))
        oh = (jnp.einsum("bqw,bdw->bqd", e_c.astype(BF16), vm, preferred_element_type=F32)
              + jnp.einsum("bqk,bkd->bqd", e_n.astype(BF16), vn.astype(BF16),
                           preferred_element_type=F32)) * inv
        for n, hd in enumerate(heads):
            o[hd] = oh[:, n * t:(n + 1) * t, :]
    lo = _lo_half()[None]
    att = jnp.concatenate([jnp.where(lo, o[2 * c], o[2 * c + 1]) for c in range(4)], axis=-1)

    sv = jnp.zeros((bb, t, GM_WIDTH), F32)
    for si in range(t):
        sv = sv + wsx_ref[si][None] * gv3[:, si:si + 1, :]
    sgu = u3 * (sv + bexp_ref[...][None])

    mixcat = jnp.concatenate([att, sgu], axis=-1).reshape(bb * t, D_MODEL).astype(BF16)
    mix = jnp.dot(mixcat, w_o_ref[...], preferred_element_type=F32).reshape(bb, t, D_MODEL)
    y1 = _norm(ALPHA * x3 + mod[2] * mix) * g1_ref[...] + b1_ref[...]

    hf = (_norm(y1) * (1.0 + mod[4]) + mod[3]).reshape(bb * t, D_MODEL).astype(BF16)
    f = _swiglu(hf, wg_ref, wu_ref, wd_ref).reshape(bb, t, D_MODEL)
    y_ref[...] = _norm(ALPHA * y1 + mod[5] * f) * g2_ref[...] + b2_ref[...]


def _sample_layer(x, mod, sinks, tabs, kt, vt, w_in_b, wsx, bexp_t, sgu_g, sgu_b, w_o_b,
                  g1, b1, wg, wu, wd, g2, b2, bb):
    n, t, _ = x.shape
    wb = kt.shape[2]
    row3 = lambda r, c: pl.BlockSpec((bb, r, c), lambda b: (b, 0, 0))
    consts = (*tabs, )
    consts2 = (w_in_b, wsx, bexp_t, sgu_g, sgu_b, w_o_b, g1, b1, wg, wu, wd, g2, b2)
    return pl.pallas_call(
        functools.partial(_sample_kernel, bb=bb, t=t),
        grid=(n // bb,),
        in_specs=[
            pl.BlockSpec(memory_space=pltpu.SMEM),
            row3(t, D_MODEL), pl.BlockSpec((6, bb, D_MODEL), lambda b: (0, b, 0)),
            *[_const_spec(a.shape) for a in consts],
            row3(KV_WIDTH, wb), row3(KV_WIDTH, wb),
            *[_const_spec(a.shape) for a in consts2],
        ],
        out_specs=[row3(t, D_MODEL), row3(KV_WIDTH, wb), row3(KV_WIDTH, wb), row3(t, GM_WIDTH)],
        out_shape=[
            jax.ShapeDtypeStruct((n, t, D_MODEL), F32),
            jax.ShapeDtypeStruct((n, KV_WIDTH, wb), F32),
            jax.ShapeDtypeStruct((n, KV_WIDTH, wb), F32),
            jax.ShapeDtypeStruct((n, t, GM_WIDTH), F32),
        ],
        compiler_params=pltpu.CompilerParams(
            dimension_semantics=("arbitrary",), vmem_limit_bytes=VMEM_LIMIT_BYTES),
        name="sample_layer",
    )(sinks, x, mod, *consts, kt, vt, *consts2)


def _lane_freqs():
    half = ROT_DIM // 2
    inv_freq = np.power(np.float32(ROPE_THETA), -np.arange(half, dtype=np.float32) * np.float32(2.0 / ROT_DIM))
    per_head = np.concatenate([inv_freq, inv_freq, np.zeros((HEAD_DIM - ROT_DIM,), np.float32)])
    return np.concatenate([per_head, per_head]).astype(np.float32)


def _angle_tables(pos):
    ang = np.asarray(pos, np.float32)[:, None] * _lane_freqs()[None, :]
    return jnp.asarray(np.cos(ang), F32), jnp.asarray(np.sin(ang), F32)


def _to_feature_major(win):
    n, w = win.shape[0], win.shape[1]
    return jnp.transpose(win, (0, 2, 3, 1)).reshape(n, KV_WIDTH, w)


def _from_feature_major(win_t):
    n, _, w = win_t.shape
    return jnp.transpose(win_t.reshape(n, KV_HEADS, HEAD_DIM, w), (0, 3, 1, 2))


def kernel(x_prompt, x_sample, cache_k_win, cache_v_win, c_prompt, c_sample, w_ada, b_ada, w_in, attn_sinks, sgu_ln_g, sgu_ln_b, w_s, b_s, w_o, ln1_g, ln1_b, w_gate, w_up, w_down, ln2_g, ln2_b):
    bsz, seq, _ = x_prompt.shape
    nsm, t, _ = x_sample.shape
    tq, bb = 256, 32
    tabs_p = (*_angle_tables(np.arange(tq)), *_angle_tables(np.arange(seq // tq) * tq))
    tabs_s = _angle_tables(np.tile(PAST_LEN + np.arange(t), bb))

    yp, ys = x_prompt, x_sample
    kwp, vwp, kws, vws, sgv = [], [], [], [], []
    for l in range(DEPTH):
        bexp = jnp.repeat(b_s[l].T, HEAD_DIM, axis=1)
        wsx = jnp.repeat(jnp.transpose(jnp.tril(w_s[l, :, :t, :t]), (2, 1, 0)), HEAD_DIM, axis=-1)
        sinks = attn_sinks[l]
        sg, sb = sgu_ln_g[l][None, :], sgu_ln_b[l][None, :]
        g1, b1 = ln1_g[l][None, :], ln1_b[l][None, :]
        g2, b2 = ln2_g[l][None, :], ln2_b[l][None, :]

        mod_s, mod_p = _ada(c_sample, c_prompt, w_ada, b_ada, l)

        yp, kw, vw, w_in_b, w_o_b, wg, wu, wd = _prompt_layer(
            yp, mod_p, sinks, tabs_p, (w_in, w_o, w_gate, w_up, w_down), w_s, bexp, sg, sb,
            g1, b1, g2, b2, tq, l)
        kwp.append(_from_feature_major(kw))
        vwp.append(_from_feature_major(vw))

        ys, kn, vn, gvs = _sample_layer(
            ys, mod_s, sinks, tabs_s, _to_feature_major(cache_k_win[l]),
            _to_feature_major(cache_v_win[l]), w_in_b, wsx, bexp[:t], sg, sb, w_o_b, g1, b1, wg, wu, wd, g2, b2, bb)
        kws.append(_from_feature_major(kn))
        vws.append(_from_feature_major(vn))
        sgv.append(gvs)
    return (yp, ys, jnp.stack(kwp, axis=0), jnp.stack(vwp, axis=0), jnp.stack(kws, axis=0),
            jnp.stack(vws, axis=0), jnp.stack(sgv, axis=0))
```

```python
import functools

import jax
import jax.numpy as jnp
import numpy as np
from jax import lax
from jax.experimental import pallas as pl
from jax.experimental.pallas import tpu as pltpu

D_MODEL = 1024
HEAD_DIM = 64
ATT_HEADS = 8
KV_HEADS = 2
Q_PER_KV = ATT_HEADS // KV_HEADS
ATT_WIDTH = ATT_HEADS * HEAD_DIM
KV_WIDTH = KV_HEADS * HEAD_DIM
GM_HEADS = 8
GM_WIDTH = GM_HEADS * HEAD_DIM
IN_WIDTH = ATT_WIDTH + 2 * KV_WIDTH + 2 * GM_WIDTH
D_FF = 2816
WINDOW = 128
CHUNK = 128
PAST_LEN = 16384
ROPE_THETA = 500000.0
ROT_DIM = HEAD_DIM // 4
DEPTH = 1
ALPHA = (2 * DEPTH) ** 0.25
LN_EPS = 1e-5
ATT_SCALE = HEAD_DIM ** -0.5

LANES = 128
VMEM_LIMIT_BYTES = 56 * 1024 * 1024
FF_SPLIT = 1280

BF16 = jnp.bfloat16
F32 = jnp.float32

HEADS_STRAIGHT = tuple(h for h in range(ATT_HEADS) if h % 2 == h // Q_PER_KV)
HEADS_SWAPPED = tuple(h for h in range(ATT_HEADS) if h % 2 != h // Q_PER_KV)


def _norm(x):
    mu = jnp.mean(x, axis=-1, keepdims=True)
    xc = x - mu
    var = jnp.mean(xc * xc, axis=-1, keepdims=True)
    return xc * lax.rsqrt(var + LN_EPS)


def _lo_half():
    return lax.broadcasted_iota(jnp.int32, (1, LANES), 1) < HEAD_DIM


def _rotary_lane_signs():
    half = ROT_DIM // 2
    l = lax.broadcasted_iota(jnp.int32, (1, LANES), 1) % HEAD_DIM
    up = jnp.where(l < half, -1.0, 0.0)
    dn = jnp.where((l >= half) & (l < ROT_DIM), 1.0, 0.0)
    return up, dn


def _rotary(x, cos, sin_up, sin_dn):
    half = ROT_DIM // 2
    return (x * cos + pltpu.roll(x, LANES - half, 1) * sin_up
            + pltpu.roll(x, half, 1) * sin_dn)


def _swap_halves(x):
    return pltpu.roll(x, HEAD_DIM, x.ndim - 1)


def _masked_heads(qcols, heads):
    lo = _lo_half()
    parts = []
    for h in heads:
        qc = qcols[h // 2]
        parts.append(jnp.where(lo, qc, 0.0) if h % 2 == 0 else jnp.where(lo, 0.0, qc))
    return jnp.concatenate(parts, axis=-2).astype(BF16)


PREP_ROWS = 256
PREP_SLOTS = 4


def _load_weights_bf16(pairs, stage_ref, sem_ref, scaled_cols):
    chunks = [(n, r0) for n, (_, dst) in enumerate(pairs) for r0 in range(0, dst.shape[0], PREP_ROWS)]

    def chunk_copy(idx):
        n, r0 = chunks[idx]
        hbm, dst = pairs[n]
        slot = idx % PREP_SLOTS
        return pltpu.make_async_copy(
            hbm.at[pl.ds(r0, PREP_ROWS), :],
            stage_ref.at[slot, :, pl.ds(0, dst.shape[1])], sem_ref.at[slot])

    for idx in range(min(PREP_SLOTS, len(chunks))):
        chunk_copy(idx).start()
    for idx, (n, r0) in enumerate(chunks):
        dst = pairs[n][1]
        cols = dst.shape[1]
        chunk_copy(idx).wait()
        v = stage_ref[idx % PREP_SLOTS, :, pl.ds(0, cols)]
        if scaled_cols[n]:
            col = lax.broadcasted_iota(jnp.int32, (1, cols), 1)
            v = v * jnp.where(col < scaled_cols[n], ATT_SCALE, 1.0)
        dst[pl.ds(r0, PREP_ROWS), :] = v.astype(BF16)
        if idx + PREP_SLOTS < len(chunks):
            chunk_copy(idx + PREP_SLOTS).start()


ADA_K_CHUNK = 256


def _ada_kernel(cs_ref, cp_ref, w_ref, b_ref, os_ref, op_ref):
    nvec = os_ref.shape[0]
    vec_cols = [slice(r * D_MODEL, (r + 1) * D_MODEL) for r in range(nvec)]

    @pl.when(pl.program_id(0) == 0)
    def _():
        for o_ref in (os_ref, op_ref):
            for r, cols in enumerate(vec_cols):
                o_ref[r] = jnp.broadcast_to(b_ref[:, cols], o_ref.shape[1:])

    w = w_ref[...].astype(BF16)
    for c_ref, o_ref in ((cs_ref, os_ref), (cp_ref, op_ref)):
        c = c_ref[...]
        a = (c * jax.nn.sigmoid(c)).astype(BF16)
        part = jnp.dot(a, w, preferred_element_type=F32)
        for r, cols in enumerate(vec_cols):
            o_ref[r] += part[:, cols]


def _ada(c_sample, c_prompt, w_ada, b_ada, layer):
    n = w_ada.shape[2]
    nvec = n // D_MODEL
    kchunk = lambda c: pl.BlockSpec((c.shape[0], ADA_K_CHUNK), lambda k: (0, k))
    acc = lambda c: pl.BlockSpec((nvec, c.shape[0], D_MODEL), lambda k: (0, 0, 0))
    return pl.pallas_call(
        _ada_kernel,
        grid=(D_MODEL // ADA_K_CHUNK,),
        in_specs=[
            kchunk(c_sample), kchunk(c_prompt),
            pl.BlockSpec((None, ADA_K_CHUNK, n), lambda k: (layer, k, 0)),
            pl.BlockSpec((1, n), lambda k: (layer, 0)),
        ],
        out_specs=[acc(c_sample), acc(c_prompt)],
        out_shape=[jax.ShapeDtypeStruct((nvec, c.shape[0], D_MODEL), F32) for c in (c_sample, c_prompt)],
        compiler_params=pltpu.CompilerParams(dimension_semantics=("arbitrary",)),
        name="ada",
    )(c_sample, c_prompt, w_ada, b_ada)


def _attn_scores(qcols, kcat, kcat_sw):
    dn = (((1,), (1,)), ((), ()))
    s_a = lax.dot_general(_masked_heads(qcols, HEADS_STRAIGHT), kcat, dn, preferred_element_type=F32)
    s_b = lax.dot_general(_masked_heads(qcols, HEADS_SWAPPED), kcat_sw, dn, preferred_element_type=F32)
    return s_a, s_b


def _softmax_weights(s, first_key, sinks_ref, heads):
    row = lax.broadcasted_iota(jnp.int32, (WINDOW, 2 * WINDOW), 0)
    col = lax.broadcasted_iota(jnp.int32, (WINDOW, 2 * WINDOW), 1)
    mask = (col > row) & (col <= row + WINDOW) & (col >= first_key)
    es, invs = [], []
    for n, h in enumerate(heads):
        sink = sinks_ref[h]
        sh = jnp.where(mask, s[n * WINDOW:(n + 1) * WINDOW], -jnp.inf)
        m = jnp.maximum(jnp.max(sh, axis=-1, keepdims=True), sink)
        e = jnp.exp(sh - m)
        den = jnp.sum(e, axis=-1, keepdims=True) + jnp.exp(sink - m)
        es.append(e.astype(BF16))
        invs.append(1.0 / den)
    return jnp.concatenate(es, axis=0), invs


def _attn_merge(o_a, inv_a, o_b, inv_b):
    o = {}
    for heads, oo, inv in ((HEADS_STRAIGHT, o_a, inv_a), (HEADS_SWAPPED, o_b, inv_b)):
        for n, h in enumerate(heads):
            o[h] = oo[n * WINDOW:(n + 1) * WINDOW] * inv[n]
    lo = _lo_half()
    return jnp.concatenate([jnp.where(lo, o[2 * c], o[2 * c + 1]) for c in range(4)], axis=1)


def _prompt_sgu(uj, gvj, wsp_ref, bexp):
    lo = _lo_half()
    svs = []
    for p in range(4):
        g = gvj[:, p * LANES:(p + 1) * LANES]
        rhs = jnp.concatenate([jnp.where(lo, g, 0.0), jnp.where(lo, 0.0, g)], axis=0).astype(BF16)
        svs.append(jnp.dot(wsp_ref[p], rhs, preferred_element_type=F32))
    sv = jnp.concatenate(svs, axis=1)
    return uj * (sv + bexp)


def _swiglu(hf, wg_ref, wu_ref, wd_ref):
    f = None
    for lo, hi in ((0, FF_SPLIT), (FF_SPLIT, D_FF)):
        g = jnp.dot(hf, wg_ref[:, lo:hi], preferred_element_type=F32)
        up = jnp.dot(hf, wu_ref[:, lo:hi], preferred_element_type=F32)
        a = (g * jax.nn.sigmoid(g) * up).astype(BF16)
        d = jnp.dot(a, wd_ref[lo:hi, :], preferred_element_type=F32)
        f = d if f is None else f + d
    return f


def _prompt_kernel(sinks_ref, x_ref, xp_ref, mod_ref, cr_ref, sr_ref, cb_ref, sb_ref,
                   w_in_hbm, w_o_hbm, wg_hbm, wu_hbm, wd_hbm, ws_ref,
                   bexp_ref, sg_ref, sb2_ref, g1_ref, b1_ref, g2_ref, b2_ref,
                   y_ref, kwin_ref, vwin_ref, w_in_out, w_o_out, wg_out, wu_out, wd_out,
                   w_in_ref, w_o_ref, wg_ref, wu_ref, wd_ref, wsp_ref, stage_ref, load_sem, out_sem,
                   mixcat_ref, kprev_ref, vprev_ref, kprev_sw_ref, vprev_sw_ref,
                   *, tq, nb, nblocks, layer):
    s = pl.program_id(0)
    sm = jnp.minimum(s, nblocks - 1)
    b, i = sm // nb, sm % nb
    bp = jnp.maximum(s - 1, 0) // nb
    prev_refs = (kprev_ref, vprev_ref, kprev_sw_ref, vprev_sw_ref)
    exports = ((w_in_ref, w_in_out), (w_o_ref, w_o_out), (wg_ref, wg_out), (wu_ref, wu_out),
               (wd_ref, wd_out))

    def export_copy(n):
        return pltpu.make_async_copy(exports[n][0], exports[n][1], out_sem.at[n])

    @pl.when(s == 0)
    def _():
        for r in prev_refs:
            r[...] = jnp.zeros_like(r)
        sources = (w_in_hbm, w_o_hbm, wg_hbm, wu_hbm, wd_hbm)
        _load_weights_bf16([(src.at[layer], dst) for src, (dst, _) in zip(sources, exports)],
                           stage_ref, load_sem, scaled_cols=(ATT_WIDTH, 0, 0, 0, 0))
        for n in range(len(exports)):
            export_copy(n).start()
        row = lax.broadcasted_iota(jnp.int32, (CHUNK, CHUNK), 0)
        col = lax.broadcasted_iota(jnp.int32, (CHUNK, CHUNK), 1)
        for hd in range(GM_HEADS):
            wsp_ref[hd // 2, :, (hd % 2) * CHUNK:(hd % 2 + 1) * CHUNK] = jnp.where(
                col <= row, ws_ref[hd], 0.0).astype(BF16)

    def mixer_project(st):
        modm = [mod_ref[r, pl.ds(b, 1), :] for r in range(2)]
        h = (_norm(x_ref[...]) * (1.0 + modm[1]) + modm[0]).astype(BF16)
        z = jnp.dot(h, w_in_ref[...], preferred_element_type=F32)
        cb, sb = cb_ref[pl.ds(i, 1), :], sb_ref[pl.ds(i, 1), :]
        cr, sr = cr_ref[...], sr_ref[...]
        cos = cb * cr - sb * sr
        sin = sb * cr + cb * sr
        sgn_up, sgn_dn = _rotary_lane_signs()
        sup, sdn = sin * sgn_up, sin * sgn_dn
        st["q"] = [_rotary(z[:, c * LANES:(c + 1) * LANES], cos, sup, sdn) for c in range(4)]
        k = _rotary(z[:, ATT_WIDTH:ATT_WIDTH + KV_WIDTH], cos, sup, sdn)
        v = z[:, ATT_WIDTH + KV_WIDTH:ATT_WIDTH + 2 * KV_WIDTH]
        st["u"] = jax.nn.gelu(z[:, ATT_WIDTH + 2 * KV_WIDTH:ATT_WIDTH + 2 * KV_WIDTH + GM_WIDTH])
        st["gv"] = (_norm(jax.nn.gelu(z[:, ATT_WIDTH + 2 * KV_WIDTH + GM_WIDTH:])) * sg_ref[...]
                    + sb2_ref[...])
        st["k"], st["v"] = k, v
        st["forms"] = (k.astype(BF16), v.astype(BF16),
                       _swap_halves(k).astype(BF16), _swap_halves(v).astype(BF16))

    def mixer_scores(st):
        forms = st["forms"]
        kb, vb, kb_sw, vb_sw = forms
        bexp = bexp_ref[...]
        st["scores"], st["values"], st["sgus"] = [], [], []
        for j in range(tq // WINDOW):
            sl = slice(j * WINDOW, (j + 1) * WINDOW)
            if j == 0:
                kcat, vcat, kcat_sw, vcat_sw = (
                    jnp.concatenate([p[...], c[sl]], axis=0) for p, c in zip(prev_refs, forms))
            else:
                sl2 = slice((j - 1) * WINDOW, (j + 1) * WINDOW)
                kcat, vcat, kcat_sw, vcat_sw = kb[sl2], vb[sl2], kb_sw[sl2], vb_sw[sl2]
            st["scores"].append(_attn_scores([qc[sl] for qc in st["q"]], kcat, kcat_sw))
            st["values"].append((vcat, vcat_sw))
            st["sgus"].append(_prompt_sgu(st["u"][sl], st["gv"][sl], wsp_ref, bexp))

    def mixer_finish(st):
        for j in range(tq // WINDOW):
            first_key = jnp.where(i > 0, 0, WINDOW) if j == 0 else 0
            p_a, inv_a = _softmax_weights(st["scores"][j][0], first_key, sinks_ref, HEADS_STRAIGHT)
            p_b, inv_b = _softmax_weights(st["scores"][j][1], first_key, sinks_ref, HEADS_SWAPPED)
            o_a = jnp.dot(p_a, st["values"][j][0], preferred_element_type=F32)
            o_b = jnp.dot(p_b, st["values"][j][1], preferred_element_type=F32)
            att = _attn_merge(o_a, inv_a, o_b, inv_b)
            mixcat_ref[j * WINDOW:(j + 1) * WINDOW, :] = jnp.concatenate(
                [att, st["sgus"][j]], axis=1).astype(BF16)
        for r, fm in zip(prev_refs, st["forms"]):
            r[...] = fm[tq - WINDOW:]
        kwin_ref[...] = st["k"][tq - WINDOW:].T
        vwin_ref[...] = st["v"][tq - WINDOW:].T

    def ffn_project(st):
        st["mix"] = jnp.dot(mixcat_ref[...], w_o_ref[...], preferred_element_type=F32)

    def ffn_gate_up(st):
        modp = [mod_ref[r, pl.ds(bp, 1), :] for r in range(6)]
        y1 = _norm(ALPHA * xp_ref[...] + modp[2] * st["mix"]) * g1_ref[...] + b1_ref[...]
        hf = (_norm(y1) * (1.0 + modp[4]) + modp[3]).astype(BF16)
        st["acts"] = []
        for cols in (slice(0, FF_SPLIT), slice(FF_SPLIT, D_FF)):
            g = jnp.dot(hf, wg_ref[:, cols], preferred_element_type=F32)
            up = jnp.dot(hf, wu_ref[:, cols], preferred_element_type=F32)
            st["acts"].append((g * jax.nn.sigmoid(g) * up).astype(BF16))
        st["y1"], st["gate2"] = y1, modp[5]

    def ffn_finish(st):
        f = (jnp.dot(st["acts"][0], wd_ref[0:FF_SPLIT, :], preferred_element_type=F32)
             + jnp.dot(st["acts"][1], wd_ref[FF_SPLIT:D_FF, :], preferred_element_type=F32))
        y_ref[...] = _norm(ALPHA * st["y1"] + st["gate2"] * f) * g2_ref[...] + b2_ref[...]

    def run(phases):
        st = {}
        for phase in phases:
            phase(st)

    @pl.when(s == 0)
    def _():
        run((mixer_project, mixer_scores, mixer_finish))

    @pl.when((s > 0) & (s < nblocks))
    def _():
        run((ffn_project, mixer_project, ffn_gate_up, mixer_scores, ffn_finish, mixer_finish))

    @pl.when(s == nblocks)
    def _():
        for n in range(len(exports)):
            export_copy(n).wait()
        run((ffn_project, ffn_gate_up, ffn_finish))


def _const_spec(shape):
    nd = len(shape)
    return pl.BlockSpec(shape, lambda *_: (0,) * nd)


def _prompt_layer(x, mod, sinks, tabs, weights, w_s, bexp, sgu_g, sgu_b, g1, b1, g2, b2, tq, layer):
    bsz, seq, _ = x.shape
    nb = seq // tq
    nblocks = bsz * nb
    carry = pltpu.VMEM((WINDOW, KV_WIDTH), BF16)
    wshapes = [w.shape[1:] for w in weights]
    widest = max(c for _, c in wshapes)
    assert all(r % PREP_ROWS == 0 for r, _ in wshapes)

    def this_block(s):
        sm = jnp.minimum(s, nblocks - 1)
        return sm // nb, sm % nb, 0

    def prev_block(s):
        sp = jnp.maximum(s - 1, 0)
        return sp // nb, sp % nb, 0

    consts = (bexp, sgu_g, sgu_b, g1, b1, g2, b2)
    win_spec = pl.BlockSpec((None, KV_WIDTH, WINDOW),
                            lambda s: (jnp.minimum(s, nblocks - 1) // nb, 0, 0))
    hbm = pl.BlockSpec(memory_space=pl.ANY)
    return pl.pallas_call(
        functools.partial(_prompt_kernel, tq=tq, nb=nb, nblocks=nblocks, layer=layer),
        grid=(nblocks + 1,),
        in_specs=[
            pl.BlockSpec(memory_space=pltpu.SMEM),
            pl.BlockSpec((None, tq, D_MODEL), this_block),
            pl.BlockSpec((None, tq, D_MODEL), prev_block),
            _const_spec(mod.shape),
            *[_const_spec(c.shape) for c in tabs],
            *[hbm for _ in weights],
            pl.BlockSpec((None,) + w_s.shape[1:], lambda s: (layer, 0, 0, 0)),
            *[_const_spec(c.shape) for c in consts],
        ],
        out_specs=[pl.BlockSpec((None, tq, D_MODEL), prev_block), win_spec, win_spec,
                   *[hbm for _ in weights]],
        out_shape=[
            jax.ShapeDtypeStruct((bsz, seq, D_MODEL), F32),
            jax.ShapeDtypeStruct((bsz, KV_WIDTH, WINDOW), F32),
            jax.ShapeDtypeStruct((bsz, KV_WIDTH, WINDOW), F32),
            *[jax.ShapeDtypeStruct(shp, BF16) for shp in wshapes],
        ],
        scratch_shapes=[
            *[pltpu.VMEM(shp, BF16) for shp in wshapes],
            pltpu.VMEM((GM_HEADS // 2, CHUNK, 2 * CHUNK), BF16),
            pltpu.VMEM((PREP_SLOTS, PREP_ROWS, widest), F32),
            pltpu.SemaphoreType.DMA((PREP_SLOTS,)),
            pltpu.SemaphoreType.DMA((len(weights),)),
            pltpu.VMEM((tq, D_MODEL), BF16), carry, carry, carry, carry,
        ],
        compiler_params=pltpu.CompilerParams(
            dimension_semantics=("arbitrary",), vmem_limit_bytes=VMEM_LIMIT_BYTES),
        name="prompt_layer",
    )(sinks, x, x, mod, *tabs, *weights, w_s, *consts)


def _shifted_window(old, new2, bb, t):
    wb = old.shape[2]
    lane = lax.broadcasted_iota(jnp.int32, (1, wb), 1)
    shifted = pltpu.roll(old.reshape(bb * KV_WIDTH, wb), wb - t, 1).reshape(bb, KV_WIDTH, wb)
    new_t = new2.T
    outs = []
    for n in range(bb):
        tile = new_t[:, (n * t // LANES) * LANES:(n * t // LANES + 1) * LANES]
        tail = pltpu.roll(tile, (wb - t - (n * t) % LANES) % LANES, 1)
        outs.append(jnp.where(lane >= wb - t, tail, shifted[n]))
    return outs


def _sample_kernel(sinks_ref, x_ref, mod_ref, cos_ref, sin_ref, kt_ref, vt_ref,
                   w_in_ref, wsx_ref, bexp_ref, sg_ref, sb_ref, w_o_ref, g1_ref, b1_ref,
                   wg_ref, wu_ref, wd_ref, g2_ref, b2_ref,
                   y_ref, ktn_ref, vtn_ref, gv_ref, *, bb, t):
    x3 = x_ref[...]
    mod = [mod_ref[r][:, None, :] for r in range(6)]
    h = (_norm(x3) * (1.0 + mod[1]) + mod[0]).reshape(bb * t, D_MODEL).astype(BF16)
    z = jnp.dot(h, w_in_ref[...], preferred_element_type=F32)

    cos, sin = cos_ref[...], sin_ref[...]
    sgn_up, sgn_dn = _rotary_lane_signs()
    sup, sdn = sin * sgn_up, sin * sgn_dn
    qcols = [_rotary(z[:, c * LANES:(c + 1) * LANES], cos, sup, sdn).reshape(bb, t, LANES)
             for c in range(4)]
    k2 = _rotary(z[:, ATT_WIDTH:ATT_WIDTH + KV_WIDTH], cos, sup, sdn)
    v2 = z[:, ATT_WIDTH + KV_WIDTH:ATT_WIDTH + 2 * KV_WIDTH]
    u3 = jax.nn.gelu(z[:, ATT_WIDTH + 2 * KV_WIDTH:ATT_WIDTH + 2 * KV_WIDTH + GM_WIDTH]
                     ).reshape(bb, t, GM_WIDTH)
    gv = _norm(jax.nn.gelu(z[:, ATT_WIDTH + 2 * KV_WIDTH + GM_WIDTH:])) * sg_ref[...] + sb_ref[...]
    gv3 = gv.reshape(bb, t, GM_WIDTH)
    gv_ref[...] = gv3

    kt, vt = kt_ref[...], vt_ref[...]
    wb = kt.shape[2]
    for n, (ko, vo) in enumerate(zip(_shifted_window(kt, k2, bb, t), _shifted_window(vt, v2, bb, t))):
        ktn_ref[n] = ko
        vtn_ref[n] = vo

    nq = Q_PER_KV * t
    tok = lax.broadcasted_iota(jnp.int32, (nq, 1), 0) % t
    mask_c = (lax.broadcasted_iota(jnp.int32, (nq, wb), 1) + (WINDOW - wb) > tok)[None]
    mask_n = (lax.broadcasted_iota(jnp.int32, (nq, t), 1) <= tok)[None]
    kb, vb = kt.astype(BF16), vt.astype(BF16)
    swap_rows = lambda a: jnp.concatenate([a[:, HEAD_DIM:, :], a[:, :HEAD_DIM, :]], axis=1)
    k3, v3 = k2.reshape(bb, t, KV_WIDTH), v2.reshape(bb, t, KV_WIDTH)
    head_row = lax.broadcasted_iota(jnp.int32, (nq, 1), 0) // t
    o = {}
    for heads, km, vm, kn, vn in (
            (HEADS_STRAIGHT, kb, vb, k3, v3),
            (HEADS_SWAPPED, swap_rows(kb), swap_rows(vb), _swap_halves(k3), _swap_halves(v3))):
        lhs = _masked_heads(qcols, heads)
        s_c = jnp.einsum("bqd,bdw->bqw", lhs, km, preferred_element_type=F32)
        s_n = jnp.einsum("bqd,bkd->bqk", lhs, kn.astype(BF16), preferred_element_type=F32)
        s_c = jnp.where(mask_c, s_c, -jnp.inf)
        s_n = jnp.where(mask_n, s_n, -jnp.inf)
        sink = jnp.full((nq, 1), sinks_ref[heads[0]], F32)
        for n in range(1, len(heads)):
            sink = jnp.where(head_row == n, sinks_ref[heads[n]], sink)
        sink = sink[None]
        m = jnp.maximum(jnp.maximum(jnp.max(s_c, axis=-1, keepdims=True),
                                    jnp.max(s_n, axis=-1, keepdims=True)), sink)
        e_c, e_n = jnp.exp(s_c - m), jnp.exp(s_n - m)
        inv = 1.0 / (jnp.sum(e_c, axis=-1, keepdims=True) + jnp.sum(e_n, axis=-1, keepdims=True)
                     + jnp.exp(sink - m))
        oh = (jnp.einsum("bqw,bdw->bqd", e_c.astype(BF16), vm, preferred_element_type=F32)
              + jnp.einsum("bqk,bkd->bqd", e_n.astype(BF16), vn.astype(BF16),
                           preferred_element_type=F32)) * inv
        for n, hd in enumerate(heads):
            o[hd] = oh[:, n * t:(n + 1) * t, :]
    lo = _lo_half()[None]
    att = jnp.concatenate([jnp.where(lo, o[2 * c], o[2 * c + 1]) for c in range(4)], axis=-1)

    sv = jnp.zeros((bb, t, GM_WIDTH), F32)
    for si in range(t):
        sv = sv + wsx_ref[si][None] * gv3[:, si:si + 1, :]
    sgu = u3 * (sv + bexp_ref[...][None])

    mixcat = jnp.concatenate([att, sgu], axis=-1).reshape(bb * t, D_MODEL).astype(BF16)
    mix = jnp.dot(mixcat, w_o_ref[...], preferred_element_type=F32).reshape(bb, t, D_MODEL)
    y1 = _norm(ALPHA * x3 + mod[2] * mix) * g1_ref[...] + b1_ref[...]

    hf = (_norm(y1) * (1.0 + mod[4]) + mod[3]).reshape(bb * t, D_MODEL).astype(BF16)
    f = _swiglu(hf, wg_ref, wu_ref, wd_ref).reshape(bb, t, D_MODEL)
    y_ref[...] = _norm(ALPHA * y1 + mod[5] * f) * g2_ref[...] + b2_ref[...]


def _sample_layer(x, mod, sinks, tabs, kt, vt, w_in_b, wsx, bexp_t, sgu_g, sgu_b, w_o_b,
                  g1, b1, wg, wu, wd, g2, b2, bb):
    n, t, _ = x.shape
    wb = kt.shape[2]
    row3 = lambda r, c: pl.BlockSpec((bb, r, c), lambda b: (b, 0, 0))
    consts = (*tabs, )
    consts2 = (w_in_b, wsx, bexp_t, sgu_g, sgu_b, w_o_b, g1, b1, wg, wu, wd, g2, b2)
    return pl.pallas_call(
        functools.partial(_sample_kernel, bb=bb, t=t),
        grid=(n // bb,),
        in_specs=[
            pl.BlockSpec(memory_space=pltpu.SMEM),
            row3(t, D_MODEL), pl.BlockSpec((6, bb, D_MODEL), lambda b: (0, b, 0)),
            *[_const_spec(a.shape) for a in consts],
            row3(KV_WIDTH, wb), row3(KV_WIDTH, wb),
            *[_const_spec(a.shape) for a in consts2],
        ],
        out_specs=[row3(t, D_MODEL), row3(KV_WIDTH, wb), row3(KV_WIDTH, wb), row3(t, GM_WIDTH)],
        out_shape=[
            jax.ShapeDtypeStruct((n, t, D_MODEL), F32),
            jax.ShapeDtypeStruct((n, KV_WIDTH, wb), F32),
            jax.ShapeDtypeStruct((n, KV_WIDTH, wb), F32),
            jax.ShapeDtypeStruct((n, t, GM_WIDTH), F32),
        ],
        compiler_params=pltpu.CompilerParams(
            dimension_semantics=("arbitrary",), vmem_limit_bytes=VMEM_LIMIT_BYTES),
        name="sample_layer",
    )(sinks, x, mod, *consts, kt, vt, *consts2)


def _lane_freqs():
    half = ROT_DIM // 2
    inv_freq = np.power(np.float32(ROPE_THETA), -np.arange(half, dtype=np.float32) * np.float32(2.0 / ROT_DIM))
    per_head = np.concatenate([inv_freq, inv_freq, np.zeros((HEAD_DIM - ROT_DIM,), np.float32)])
    return np.concatenate([per_head, per_head]).astype(np.float32)


def _angle_tables(pos):
    ang = np.asarray(pos, np.float32)[:, None] * _lane_freqs()[None, :]
    return jnp.asarray(np.cos(ang), F32), jnp.asarray(np.sin(ang), F32)


def _to_feature_major(win):
    n, w = win.shape[0], win.shape[1]
    return jnp.transpose(win, (0, 2, 3, 1)).reshape(n, KV_WIDTH, w)


def _from_feature_major(win_t):
    n, _, w = win_t.shape
    return jnp.transpose(win_t.reshape(n, KV_HEADS, HEAD_DIM, w), (0, 3, 1, 2))


def kernel(x_prompt, x_sample, cache_k_win, cache_v_win, c_prompt, c_sample, w_ada, b_ada, w_in, attn_sinks, sgu_ln_g, sgu_ln_b, w_s, b_s, w_o, ln1_g, ln1_b, w_gate, w_up, w_down, ln2_g, ln2_b):
    bsz, seq, _ = x_prompt.shape
    nsm, t, _ = x_sample.shape
    tq, bb = 256, 32
    tabs_p = (*_angle_tables(np.arange(tq)), *_angle_tables(np.arange(seq // tq) * tq))
    tabs_s = _angle_tables(np.tile(PAST_LEN + np.arange(t), bb))

    yp, ys = x_prompt, x_sample
    kwp, vwp, kws, vws, sgv = [], [], [], [], []
    for l in range(DEPTH):
        bexp = jnp.repeat(b_s[l].T, HEAD_DIM, axis=1)
        wsx = jnp.repeat(jnp.transpose(jnp.tril(w_s[l, :, :t, :t]), (2, 1, 0)), HEAD_DIM, axis=-1)
        sinks = attn_sinks[l]
        sg, sb = sgu_ln_g[l][None, :], sgu_ln_b[l][None, :]
        g1, b1 = ln1_g[l][None, :], ln1_b[l][None, :]
        g2, b2 = ln2_g[l][None, :], ln2_b[l][None, :]

        mod_s, mod_p = _ada(c_sample, c_prompt, w_ada, b_ada, l)

        yp, kw, vw, w_in_b, w_o_b, wg, wu, wd = _prompt_layer(
            yp, mod_p, sinks, tabs_p, (w_in, w_o, w_gate, w_up, w_down), w_s, bexp, sg, sb,
            g1, b1, g2, b2, tq, l)
        kwp.append(_from_feature_major(kw))
        vwp.append(_from_feature_major(vw))

        ys, kn, vn, gvs = _sample_layer(
            ys, mod_s, sinks, tabs_s, _to_feature_major(cache_k_win[l]),
            _to_feature_major(cache_v_win[l]), w_in_b, wsx, bexp[:t], sg, sb, w_o_b, g1, b1, wg, wu, wd, g2, b2, bb)
        kws.append(_from_feature_major(kn))
        vws.append(_from_feature_major(vn))
        sgv.append(gvs)
    return (yp, ys, jnp.stack(kwp, axis=0), jnp.stack(vwp, axis=0), jnp.stack(kws, axis=0),
            jnp.stack(vws, axis=0), jnp.stack(sgv, axis=0))
```

```python
import functools

import jax
import jax.numpy as jnp
import numpy as np
from jax import lax
from jax.experimental import pallas as pl
from jax.experimental.pallas import tpu as pltpu

D_MODEL = 1024
HEAD_DIM = 64
ATT_HEADS = 8
KV_HEADS = 2
Q_PER_KV = ATT_HEADS // KV_HEADS
ATT_WIDTH = ATT_HEADS * HEAD_DIM
KV_WIDTH = KV_HEADS * HEAD_DIM
GM_HEADS = 8
GM_WIDTH = GM_HEADS * HEAD_DIM
IN_WIDTH = ATT_WIDTH + 2 * KV_WIDTH + 2 * GM_WIDTH
D_FF = 2816
WINDOW = 128
CHUNK = 128
PAST_LEN = 16384
ROPE_THETA = 500000.0
ROT_DIM = HEAD_DIM // 4
DEPTH = 1
ALPHA = (2 * DEPTH) ** 0.25
LN_EPS = 1e-5
ATT_SCALE = HEAD_DIM ** -0.5

LANES = 128
VMEM_LIMIT_BYTES = 60 * 1024 * 1024
FF_SPLIT = 1280

BF16 = jnp.bfloat16
F32 = jnp.float32

HEADS_STRAIGHT = tuple(h for h in range(ATT_HEADS) if h % 2 == h // Q_PER_KV)
HEADS_SWAPPED = tuple(h for h in range(ATT_HEADS) if h % 2 != h // Q_PER_KV)


def _norm(x):
    mu = jnp.mean(x, axis=-1, keepdims=True)
    xc = x - mu
    var = jnp.mean(xc * xc, axis=-1, keepdims=True)
    return xc * lax.rsqrt(var + LN_EPS)


def _lo_half():
    return lax.broadcasted_iota(jnp.int32, (1, LANES), 1) < HEAD_DIM


def _rotary_lane_signs():
    half = ROT_DIM // 2
    l = lax.broadcasted_iota(jnp.int32, (1, LANES), 1) % HEAD_DIM
    up = jnp.where(l < half, -1.0, 0.0)
    dn = jnp.where((l >= half) & (l < ROT_DIM), 1.0, 0.0)
    return up, dn


def _rotary(x, cos, sin_up, sin_dn):
    half = ROT_DIM // 2
    return (x * cos + pltpu.roll(x, LANES - half, 1) * sin_up
            + pltpu.roll(x, half, 1) * sin_dn)


def _swap_halves(x):
    return pltpu.roll(x, HEAD_DIM, x.ndim - 1)


def _masked_heads(qcols, heads):
    lo = _lo_half()
    parts = []
    for h in heads:
        qc = qcols[h // 2]
        parts.append(jnp.where(lo, qc, 0.0) if h % 2 == 0 else jnp.where(lo, 0.0, qc))
    return jnp.concatenate(parts, axis=-2).astype(BF16)


PREP_ROWS = 256
PREP_SLOTS = 4


def _load_weights_bf16(pairs, stage_ref, sem_ref, scaled_cols):
    chunks = [(n, r0) for n, (_, dst) in enumerate(pairs) for r0 in range(0, dst.shape[0], PREP_ROWS)]

    def chunk_copy(idx):
        n, r0 = chunks[idx]
        hbm, dst = pairs[n]
        slot = idx % PREP_SLOTS
        return pltpu.make_async_copy(
            hbm.at[pl.ds(r0, PREP_ROWS), :],
            stage_ref.at[slot, :, pl.ds(0, dst.shape[1])], sem_ref.at[slot])

    for idx in range(min(PREP_SLOTS, len(chunks))):
        chunk_copy(idx).start()
    for idx, (n, r0) in enumerate(chunks):
        dst = pairs[n][1]
        cols = dst.shape[1]
        chunk_copy(idx).wait()
        v = stage_ref[idx % PREP_SLOTS, :, pl.ds(0, cols)]
        if scaled_cols[n]:
            col = lax.broadcasted_iota(jnp.int32, (1, cols), 1)
            v = v * jnp.where(col < scaled_cols[n], ATT_SCALE, 1.0)
        dst[pl.ds(r0, PREP_ROWS), :] = v.astype(BF16)
        if idx + PREP_SLOTS < len(chunks):
            chunk_copy(idx + PREP_SLOTS).start()


ADA_K_CHUNK = 256


def _ada_kernel(cs_ref, cp_ref, w_ref, b_ref, os_ref, op_ref):
    nvec = os_ref.shape[0]
    vec_cols = [slice(r * D_MODEL, (r + 1) * D_MODEL) for r in range(nvec)]

    @pl.when(pl.program_id(0) == 0)
    def _():
        for o_ref in (os_ref, op_ref):
            for r, cols in enumerate(vec_cols):
                o_ref[r] = jnp.broadcast_to(b_ref[:, cols], o_ref.shape[1:])

    w = w_ref[...].astype(BF16)
    for c_ref, o_ref in ((cs_ref, os_ref), (cp_ref, op_ref)):
        c = c_ref[...]
        a = (c * jax.nn.sigmoid(c)).astype(BF16)
        part = jnp.dot(a, w, preferred_element_type=F32)
        for r, cols in enumerate(vec_cols):
            o_ref[r] += part[:, cols]


def _ada(c_sample, c_prompt, w_ada, b_ada, layer):
    n = w_ada.shape[2]
    nvec = n // D_MODEL
    kchunk = lambda c: pl.BlockSpec((c.shape[0], ADA_K_CHUNK), lambda k: (0, k))
    acc = lambda c: pl.BlockSpec((nvec, c.shape[0], D_MODEL), lambda k: (0, 0, 0))
    return pl.pallas_call(
        _ada_kernel,
        grid=(D_MODEL // ADA_K_CHUNK,),
        in_specs=[
            kchunk(c_sample), kchunk(c_prompt),
            pl.BlockSpec((None, ADA_K_CHUNK, n), lambda k: (layer, k, 0)),
            pl.BlockSpec((1, n), lambda k: (layer, 0)),
        ],
        out_specs=[acc(c_sample), acc(c_prompt)],
        out_shape=[jax.ShapeDtypeStruct((nvec, c.shape[0], D_MODEL), F32) for c in (c_sample, c_prompt)],
        compiler_params=pltpu.CompilerParams(dimension_semantics=("arbitrary",)),
        name="ada",
    )(c_sample, c_prompt, w_ada, b_ada)


def _attn_scores(qcols, kcat, kcat_sw):
    dn = (((1,), (1,)), ((), ()))
    s_a = lax.dot_general(_masked_heads(qcols, HEADS_STRAIGHT), kcat, dn, preferred_element_type=F32)
    s_b = lax.dot_general(_masked_heads(qcols, HEADS_SWAPPED), kcat_sw, dn, preferred_element_type=F32)
    return s_a, s_b


def _softmax_weights(s, first_key, sinks_ref, heads):
    row = lax.broadcasted_iota(jnp.int32, (WINDOW, 2 * WINDOW), 0)
    col = lax.broadcasted_iota(jnp.int32, (WINDOW, 2 * WINDOW), 1)
    mask = (col > row) & (col <= row + WINDOW) & (col >= first_key)
    es, invs = [], []
    for n, h in enumerate(heads):
        sink = sinks_ref[h]
        sh = jnp.where(mask, s[n * WINDOW:(n + 1) * WINDOW], -jnp.inf)
        m = jnp.maximum(jnp.max(sh, axis=-1, keepdims=True), sink)
        e = jnp.exp(sh - m)
        den = jnp.sum(e, axis=-1, keepdims=True) + jnp.exp(sink - m)
        es.append(e.astype(BF16))
        invs.append(1.0 / den)
    return jnp.concatenate(es, axis=0), invs


def _attn_merge(o_a, inv_a, o_b, inv_b):
    o = {}
    for heads, oo, inv in ((HEADS_STRAIGHT, o_a, inv_a), (HEADS_SWAPPED, o_b, inv_b)):
        for n, h in enumerate(heads):
            o[h] = oo[n * WINDOW:(n + 1) * WINDOW] * inv[n]
    lo = _lo_half()
    return jnp.concatenate([jnp.where(lo, o[2 * c], o[2 * c + 1]) for c in range(4)], axis=1)


def _prompt_sgu(uj, gvj, wsp_ref, bexp):
    lo = _lo_half()
    svs = []
    for p in range(4):
        g = gvj[:, p * LANES:(p + 1) * LANES]
        rhs = jnp.concatenate([jnp.where(lo, g, 0.0), jnp.where(lo, 0.0, g)], axis=0).astype(BF16)
        svs.append(jnp.dot(wsp_ref[p], rhs, preferred_element_type=F32))
    sv = jnp.concatenate(svs, axis=1)
    return uj * (sv + bexp)


def _swiglu(hf, wg_ref, wu_ref, wd_ref):
    f = None
    for lo, hi in ((0, FF_SPLIT), (FF_SPLIT, D_FF)):
        g = jnp.dot(hf, wg_ref[:, lo:hi], preferred_element_type=F32)
        up = jnp.dot(hf, wu_ref[:, lo:hi], preferred_element_type=F32)
        a = (g * jax.nn.sigmoid(g) * up).astype(BF16)
        d = jnp.dot(a, wd_ref[lo:hi, :], preferred_element_type=F32)
        f = d if f is None else f + d
    return f


def _prompt_kernel(sinks_ref, x_ref, xp_ref, mod_ref, cr_ref, sr_ref, cb_ref, sb_ref,
                   w_in_hbm, w_o_hbm, wg_hbm, wu_hbm, wd_hbm, ws_ref,
                   bexp_ref, sg_ref, sb2_ref, g1_ref, b1_ref, g2_ref, b2_ref,
                   y_ref, kwin_ref, vwin_ref, w_in_out, w_o_out, wg_out, wu_out, wd_out,
                   w_in_ref, w_o_ref, wg_ref, wu_ref, wd_ref, wsp_ref, stage_ref, load_sem, out_sem,
                   mixcat_ref, kprev_ref, vprev_ref, kprev_sw_ref, vprev_sw_ref,
                   *, tq, nb, nblocks, layer):
    s = pl.program_id(0)
    sm = jnp.minimum(s, nblocks - 1)
    b, i = sm // nb, sm % nb
    bp = jnp.maximum(s - 1, 0) // nb
    prev_refs = (kprev_ref, vprev_ref, kprev_sw_ref, vprev_sw_ref)
    exports = ((w_in_ref, w_in_out), (w_o_ref, w_o_out), (wg_ref, wg_out), (wu_ref, wu_out),
               (wd_ref, wd_out))

    def export_copy(n):
        return pltpu.make_async_copy(exports[n][0], exports[n][1], out_sem.at[n])

    @pl.when(s == 0)
    def _():
        for r in prev_refs:
            r[...] = jnp.zeros_like(r)
        sources = (w_in_hbm, w_o_hbm, wg_hbm, wu_hbm, wd_hbm)
        _load_weights_bf16([(src.at[layer], dst) for src, (dst, _) in zip(sources, exports)],
                           stage_ref, load_sem, scaled_cols=(ATT_WIDTH, 0, 0, 0, 0))
        for n in range(len(exports)):
            export_copy(n).start()
        row = lax.broadcasted_iota(jnp.int32, (CHUNK, CHUNK), 0)
        col = lax.broadcasted_iota(jnp.int32, (CHUNK, CHUNK), 1)
        for hd in range(GM_HEADS):
            wsp_ref[hd // 2, :, (hd % 2) * CHUNK:(hd % 2 + 1) * CHUNK] = jnp.where(
                col <= row, ws_ref[hd], 0.0).astype(BF16)

    def mixer_project(st):
        modm = [mod_ref[r, pl.ds(b, 1), :] for r in range(2)]
        h = (_norm(x_ref[...]) * (1.0 + modm[1]) + modm[0]).astype(BF16)
        z = jnp.dot(h, w_in_ref[...], preferred_element_type=F32)
        cb, sb = cb_ref[pl.ds(i, 1), :], sb_ref[pl.ds(i, 1), :]
        cr, sr = cr_ref[...], sr_ref[...]
        cos = cb * cr - sb * sr
        sin = sb * cr + cb * sr
        sgn_up, sgn_dn = _rotary_lane_signs()
        sup, sdn = sin * sgn_up, sin * sgn_dn
        st["q"] = [_rotary(z[:, c * LANES:(c + 1) * LANES], cos, sup, sdn) for c in range(4)]
        k = _rotary(z[:, ATT_WIDTH:ATT_WIDTH + KV_WIDTH], cos, sup, sdn)
        v = z[:, ATT_WIDTH + KV_WIDTH:ATT_WIDTH + 2 * KV_WIDTH]
        st["u"] = jax.nn.gelu(z[:, ATT_WIDTH + 2 * KV_WIDTH:ATT_WIDTH + 2 * KV_WIDTH + GM_WIDTH])
        st["gv"] = (_norm(jax.nn.gelu(z[:, ATT_WIDTH + 2 * KV_WIDTH + GM_WIDTH:])) * sg_ref[...]
                    + sb2_ref[...])
        st["k"], st["v"] = k, v
        st["forms"] = (k.astype(BF16), v.astype(BF16),
                       _swap_halves(k).astype(BF16), _swap_halves(v).astype(BF16))

    def mixer_scores(st):
        forms = st["forms"]
        kb, vb, kb_sw, vb_sw = forms
        bexp = bexp_ref[...]
        st["scores"], st["values"], st["sgus"] = [], [], []
        for j in range(tq // WINDOW):
            sl = slice(j * WINDOW, (j + 1) * WINDOW)
            if j == 0:
                kcat, vcat, kcat_sw, vcat_sw = (
                    jnp.concatenate([p[...], c[sl]], axis=0) for p, c in zip(prev_refs, forms))
            else:
                sl2 = slice((j - 1) * WINDOW, (j + 1) * WINDOW)
                kcat, vcat, kcat_sw, vcat_sw = kb[sl2], vb[sl2], kb_sw[sl2], vb_sw[sl2]
            st["scores"].append(_attn_scores([qc[sl] for qc in st["q"]], kcat, kcat_sw))
            st["values"].append((vcat, vcat_sw))
            st["sgus"].append(_prompt_sgu(st["u"][sl], st["gv"][sl], wsp_ref, bexp))

    def mixer_finish(st):
        for j in range(tq // WINDOW):
            first_key = jnp.where(i > 0, 0, WINDOW) if j == 0 else 0
            p_a, inv_a = _softmax_weights(st["scores"][j][0], first_key, sinks_ref, HEADS_STRAIGHT)
            p_b, inv_b = _softmax_weights(st["scores"][j][1], first_key, sinks_ref, HEADS_SWAPPED)
            o_a = jnp.dot(p_a, st["values"][j][0], preferred_element_type=F32)
            o_b = jnp.dot(p_b, st["values"][j][1], preferred_element_type=F32)
            att = _attn_merge(o_a, inv_a, o_b, inv_b)
            mixcat_ref[j * WINDOW:(j + 1) * WINDOW, :] = jnp.concatenate(
                [att, st["sgus"][j]], axis=1).astype(BF16)
        for r, fm in zip(prev_refs, st["forms"]):
            r[...] = fm[tq - WINDOW:]
        kwin_ref[...] = st["k"][tq - WINDOW:].T
        vwin_ref[...] = st["v"][tq - WINDOW:].T

    def ffn_project(st):
        st["mix"] = jnp.dot(mixcat_ref[...], w_o_ref[...], preferred_element_type=F32)

    def ffn_gate_up(st):
        modp = [mod_ref[r, pl.ds(bp, 1), :] for r in range(6)]
        y1 = _norm(ALPHA * xp_ref[...] + modp[2] * st["mix"]) * g1_ref[...] + b1_ref[...]
        hf = (_norm(y1) * (1.0 + modp[4]) + modp[3]).astype(BF16)
        st["acts"] = []
        for cols in (slice(0, FF_SPLIT), slice(FF_SPLIT, D_FF)):
            g = jnp.dot(hf, wg_ref[:, cols], preferred_element_type=F32)
            up = jnp.dot(hf, wu_ref[:, cols], preferred_element_type=F32)
            st["acts"].append((g * jax.nn.sigmoid(g) * up).astype(BF16))
        st["y1"], st["gate2"] = y1, modp[5]

    def ffn_finish(st):
        f = (jnp.dot(st["acts"][0], wd_ref[0:FF_SPLIT, :], preferred_element_type=F32)
             + jnp.dot(st["acts"][1], wd_ref[FF_SPLIT:D_FF, :], preferred_element_type=F32))
        y_ref[...] = _norm(ALPHA * st["y1"] + st["gate2"] * f) * g2_ref[...] + b2_ref[...]

    def run(phases):
        st = {}
        for phase in phases:
            phase(st)

    @pl.when(s == 0)
    def _():
        run((mixer_project, mixer_scores, mixer_finish))

    @pl.when((s > 0) & (s < nblocks))
    def _():
        run((ffn_project, mixer_project, ffn_gate_up, mixer_scores, ffn_finish, mixer_finish))

    @pl.when(s == nblocks)
    def _():
        for n in range(len(exports)):
            export_copy(n).wait()
        run((ffn_project, ffn_gate_up, ffn_finish))


def _const_spec(shape):
    nd = len(shape)
    return pl.BlockSpec(shape, lambda *_: (0,) * nd)


def _prompt_layer(x, mod, sinks, tabs, weights, w_s, bexp, sgu_g, sgu_b, g1, b1, g2, b2, tq, layer):
    bsz, seq, _ = x.shape
    nb = seq // tq
    nblocks = bsz * nb
    carry = pltpu.VMEM((WINDOW, KV_WIDTH), BF16)
    wshapes = [w.shape[1:] for w in weights]
    widest = max(c for _, c in wshapes)
    assert all(r % PREP_ROWS == 0 for r, _ in wshapes)

    def this_block(s):
        sm = jnp.minimum(s, nblocks - 1)
        return sm // nb, sm % nb, 0

    def prev_block(s):
        sp = jnp.maximum(s - 1, 0)
        return sp // nb, sp % nb, 0

    consts = (bexp, sgu_g, sgu_b, g1, b1, g2, b2)
    win_spec = pl.BlockSpec((None, KV_WIDTH, WINDOW),
                            lambda s: (jnp.minimum(s, nblocks - 1) // nb, 0, 0))
    hbm = pl.BlockSpec(memory_space=pl.ANY)
    return pl.pallas_call(
        functools.partial(_prompt_kernel, tq=tq, nb=nb, nblocks=nblocks, layer=layer),
        grid=(nblocks + 1,),
        in_specs=[
            pl.BlockSpec(memory_space=pltpu.SMEM),
            pl.BlockSpec((None, tq, D_MODEL), this_block),
            pl.BlockSpec((None, tq, D_MODEL), prev_block),
            _const_spec(mod.shape),
            *[_const_spec(c.shape) for c in tabs],
            *[hbm for _ in weights],
            pl.BlockSpec((None,) + w_s.shape[1:], lambda s: (layer, 0, 0, 0)),
            *[_const_spec(c.shape) for c in consts],
        ],
        out_specs=[pl.BlockSpec((None, tq, D_MODEL), prev_block), win_spec, win_spec,
                   *[hbm for _ in weights]],
        out_shape=[
            jax.ShapeDtypeStruct((bsz, seq, D_MODEL), F32),
            jax.ShapeDtypeStruct((bsz, KV_WIDTH, WINDOW), F32),
            jax.ShapeDtypeStruct((bsz, KV_WIDTH, WINDOW), F32),
            *[jax.ShapeDtypeStruct(shp, BF16) for shp in wshapes],
        ],
        scratch_shapes=[
            *[pltpu.VMEM(shp, BF16) for shp in wshapes],
            pltpu.VMEM((GM_HEADS // 2, CHUNK, 2 * CHUNK), BF16),
            pltpu.VMEM((PREP_SLOTS, PREP_ROWS, widest), F32),
            pltpu.SemaphoreType.DMA((PREP_SLOTS,)),
            pltpu.SemaphoreType.DMA((len(weights),)),
            pltpu.VMEM((tq, D_MODEL), BF16), carry, carry, carry, carry,
        ],
        compiler_params=pltpu.CompilerParams(
            dimension_semantics=("arbitrary",), vmem_limit_bytes=VMEM_LIMIT_BYTES),
        name="prompt_layer",
    )(sinks, x, x, mod, *tabs, *weights, w_s, *consts)


def _shifted_window(old, new2, bb, t):
    wb = old.shape[2]
    lane = lax.broadcasted_iota(jnp.int32, (1, wb), 1)
    shifted = pltpu.roll(old.reshape(bb * KV_WIDTH, wb), wb - t, 1).reshape(bb, KV_WIDTH, wb)
    new_t = new2.T
    outs = []
    for n in range(bb):
        tile = new_t[:, (n * t // LANES) * LANES:(n * t // LANES + 1) * LANES]
        tail = pltpu.roll(tile, (wb - t - (n * t) % LANES) % LANES, 1)
        outs.append(jnp.where(lane >= wb - t, tail, shifted[n]))
    return outs


def _sample_kernel(sinks_ref, x_ref, xp_ref, mod_ref, modp_ref, cos_ref, sin_ref, kt_ref, vt_ref,
                   w_in_ref, wsx_ref, bexp_ref, sg_ref, sb_ref, w_o_ref, g1_ref, b1_ref,
                   wg_ref, wu_ref, wd_ref, g2_ref, b2_ref,
                   y_ref, ktn_ref, vtn_ref, gv_ref, mixcat_ref, *, bb, t, nblocks):
    s = pl.program_id(0)
    wb = kt_ref.shape[2]
    nq = Q_PER_KV * t

    def mixer_project(st):
        x3 = x_ref[...]
        shift, scale = (mod_ref[r][:, None, :] for r in range(2))
        h = (_norm(x3) * (1.0 + scale) + shift).reshape(bb * t, D_MODEL).astype(BF16)
        z = jnp.dot(h, w_in_ref[...], preferred_element_type=F32)
        cos, sin = cos_ref[...], sin_ref[...]
        sgn_up, sgn_dn = _rotary_lane_signs()
        sup, sdn = sin * sgn_up, sin * sgn_dn
        st["qcols"] = [_rotary(z[:, c * LANES:(c + 1) * LANES], cos, sup, sdn).reshape(bb, t, LANES)
                       for c in range(4)]
        k2 = _rotary(z[:, ATT_WIDTH:ATT_WIDTH + KV_WIDTH], cos, sup, sdn)
        v2 = z[:, ATT_WIDTH + KV_WIDTH:ATT_WIDTH + 2 * KV_WIDTH]
        st["u3"] = jax.nn.gelu(z[:, ATT_WIDTH + 2 * KV_WIDTH:ATT_WIDTH + 2 * KV_WIDTH + GM_WIDTH]
                               ).reshape(bb, t, GM_WIDTH)
        gv = _norm(jax.nn.gelu(z[:, ATT_WIDTH + 2 * KV_WIDTH + GM_WIDTH:])) * sg_ref[...] + sb_ref[...]
        st["gv3"] = gv.reshape(bb, t, GM_WIDTH)
        gv_ref[...] = st["gv3"]
        st["k2"], st["v2"] = k2, v2
        st["kv"] = (kt_ref[...].astype(BF16), vt_ref[...].astype(BF16),
                    k2.reshape(bb, t, KV_WIDTH).astype(BF16), v2.reshape(bb, t, KV_WIDTH).astype(BF16))

    def update_windows(st):
        for n, (ko, vo) in enumerate(zip(_shifted_window(kt_ref[...], st["k2"], bb, t),
                                         _shifted_window(vt_ref[...], st["v2"], bb, t))):
            ktn_ref[n] = ko
            vtn_ref[n] = vo

    def mixer_scores(st):
        lo = _lo_half()[None]
        parts = []
        for hd in range(ATT_HEADS):
            qc = st["qcols"][hd // 2]
            if hd in HEADS_SWAPPED:
                qc = _swap_halves(qc)
            parts.append(jnp.where(lo, qc, 0.0) if hd // Q_PER_KV == 0 else jnp.where(lo, 0.0, qc))
        lhs = jnp.concatenate(parts, axis=1).astype(BF16)
        kb, _, kn, _ = st["kv"]
        st["scores"] = (jnp.einsum("bqd,bdw->bqw", lhs, kb, preferred_element_type=F32),
                        jnp.einsum("bqd,bkd->bqk", lhs, kn, preferred_element_type=F32))

    def mixer_finish(st):
        nrow = ATT_HEADS * t
        tok = lax.broadcasted_iota(jnp.int32, (nrow, 1), 0) % t
        head_row = lax.broadcasted_iota(jnp.int32, (nrow, 1), 0) // t
        mask_c = (lax.broadcasted_iota(jnp.int32, (nrow, wb), 1) + (WINDOW - wb) > tok)[None]
        mask_n = (lax.broadcasted_iota(jnp.int32, (nrow, t), 1) <= tok)[None]
        _, vb, _, vn = st["kv"]
        s_c = jnp.where(mask_c, st["scores"][0], -jnp.inf)
        s_n = jnp.where(mask_n, st["scores"][1], -jnp.inf)
        sink = jnp.full((nrow, 1), sinks_ref[0], F32)
        for hd in range(1, ATT_HEADS):
            sink = jnp.where(head_row == hd, sinks_ref[hd], sink)
        sink = sink[None]
        m = jnp.maximum(jnp.maximum(jnp.max(s_c, axis=-1, keepdims=True),
                                    jnp.max(s_n, axis=-1, keepdims=True)), sink)
        e_c, e_n = jnp.exp(s_c - m), jnp.exp(s_n - m)
        inv = 1.0 / (jnp.sum(e_c, axis=-1, keepdims=True) + jnp.sum(e_n, axis=-1, keepdims=True)
                     + jnp.exp(sink - m))
        oh = (jnp.einsum("bqw,bdw->bqd", e_c.astype(BF16), vb, preferred_element_type=F32)
              + jnp.einsum("bqk,bkd->bqd", e_n.astype(BF16), vn, preferred_element_type=F32)) * inv
        o = [oh[:, hd * t:(hd + 1) * t, :] for hd in range(ATT_HEADS)]
        o = [_swap_halves(o[hd]) if hd in HEADS_SWAPPED else o[hd] for hd in range(ATT_HEADS)]
        lo = _lo_half()[None]
        att = jnp.concatenate([jnp.where(lo, o[2 * c], o[2 * c + 1]) for c in range(4)], axis=-1)
        sv = jnp.zeros((bb, t, GM_WIDTH), F32)
        for si in range(t):
            sv = sv + wsx_ref[si][None] * st["gv3"][:, si:si + 1, :]
        sgu = st["u3"] * (sv + bexp_ref[...][None])
        mixcat_ref[...] = jnp.concatenate([att, sgu], axis=-1).reshape(bb * t, D_MODEL).astype(BF16)

    def ffn_project(st):
        st["mix"] = jnp.dot(mixcat_ref[...], w_o_ref[...], preferred_element_type=F32
                            ).reshape(bb, t, D_MODEL)

    def gate_up(st, cols):
        g = jnp.dot(st["hf"], wg_ref[:, cols], preferred_element_type=F32)
        up = jnp.dot(st["hf"], wu_ref[:, cols], preferred_element_type=F32)
        st["acts"].append((g * jax.nn.sigmoid(g) * up).astype(BF16))

    def ffn_gate_up_a(st):
        mod = [modp_ref[r][:, None, :] for r in range(6)]
        y1 = _norm(ALPHA * xp_ref[...] + mod[2] * st["mix"]) * g1_ref[...] + b1_ref[...]
        st["hf"] = (_norm(y1) * (1.0 + mod[4]) + mod[3]).reshape(bb * t, D_MODEL).astype(BF16)
        st["y1"], st["gate2"], st["acts"] = y1, mod[5], []
        gate_up(st, slice(0, FF_SPLIT))

    def ffn_gate_up_b(st):
        gate_up(st, slice(FF_SPLIT, D_FF))

    def ffn_finish(st):
        f = (jnp.dot(st["acts"][0], wd_ref[0:FF_SPLIT, :], preferred_element_type=F32)
             + jnp.dot(st["acts"][1], wd_ref[FF_SPLIT:D_FF, :], preferred_element_type=F32)
             ).reshape(bb, t, D_MODEL)
        y_ref[...] = _norm(ALPHA * st["y1"] + st["gate2"] * f) * g2_ref[...] + b2_ref[...]

    def run(phases):
        st = {}
        for phase in phases:
            phase(st)

    @pl.when(s == 0)
    def _():
        run((mixer_project, mixer_scores, update_windows, mixer_finish))

    @pl.when((s > 0) & (s < nblocks))
    def _():
        run((ffn_project, mixer_project, update_windows, ffn_gate_up_a, ffn_gate_up_b, mixer_scores,
             ffn_finish, mixer_finish))

    @pl.when(s == nblocks)
    def _():
        run((ffn_project, ffn_gate_up_a, ffn_gate_up_b, ffn_finish))


def _sample_layer(x, mod, sinks, tabs, kt, vt, w_in_b, wsx, bexp_t, sgu_g, sgu_b, w_o_b,
                  g1, b1, wg, wu, wd, g2, b2, bb):
    n, t, _ = x.shape
    wb = kt.shape[2]
    nblocks = n // bb
    assert wb == LANES and (bb * t) % LANES == 0 and LANES % t == 0
    this_block = lambda s: jnp.minimum(s, nblocks - 1)
    prev_block = lambda s: jnp.maximum(s - 1, 0)
    rows = lambda r, c, blk: pl.BlockSpec((bb, r, c), lambda s: (blk(s), 0, 0))
    mods = lambda blk: pl.BlockSpec((6, bb, D_MODEL), lambda s: (0, blk(s), 0))
    consts2 = (w_in_b, wsx, bexp_t, sgu_g, sgu_b, w_o_b, g1, b1, wg, wu, wd, g2, b2)
    return pl.pallas_call(
        functools.partial(_sample_kernel, bb=bb, t=t, nblocks=nblocks),
        grid=(nblocks + 1,),
        in_specs=[
            pl.BlockSpec(memory_space=pltpu.SMEM),
            rows(t, D_MODEL, this_block), rows(t, D_MODEL, prev_block),
            mods(this_block), mods(prev_block),
            *[_const_spec(a.shape) for a in tabs],
            rows(KV_WIDTH, wb, this_block), rows(KV_WIDTH, wb, this_block),
            *[_const_spec(a.shape) for a in consts2],
        ],
        out_specs=[rows(t, D_MODEL, prev_block), rows(KV_WIDTH, wb, this_block),
                   rows(KV_WIDTH, wb, this_block), rows(t, GM_WIDTH, this_block)],
        out_shape=[
            jax.ShapeDtypeStruct((n, t, D_MODEL), F32),
            jax.ShapeDtypeStruct((n, KV_WIDTH, wb), F32),
            jax.ShapeDtypeStruct((n, KV_WIDTH, wb), F32),
            jax.ShapeDtypeStruct((n, t, GM_WIDTH), F32),
        ],
        scratch_shapes=[pltpu.VMEM((bb * t, D_MODEL), BF16)],
        compiler_params=pltpu.CompilerParams(
            dimension_semantics=("arbitrary",), vmem_limit_bytes=VMEM_LIMIT_BYTES),
        name="sample_layer",
    )(sinks, x, x, mod, mod, *tabs, kt, vt, *consts2)


def _lane_freqs():
    half = ROT_DIM // 2
    inv_freq = np.power(np.float32(ROPE_THETA), -np.arange(half, dtype=np.float32) * np.float32(2.0 / ROT_DIM))
    per_head = np.concatenate([inv_freq, inv_freq, np.zeros((HEAD_DIM - ROT_DIM,), np.float32)])
    return np.concatenate([per_head, per_head]).astype(np.float32)


def _angle_tables(pos):
    ang = np.asarray(pos, np.float32)[:, None] * _lane_freqs()[None, :]
    return jnp.asarray(np.cos(ang), F32), jnp.asarray(np.sin(ang), F32)


def _to_feature_major(win):
    n, w = win.shape[0], win.shape[1]
    return jnp.transpose(win, (0, 2, 3, 1)).reshape(n, KV_WIDTH, w)


def _from_feature_major(win_t):
    n, _, w = win_t.shape
    return jnp.transpose(win_t.reshape(n, KV_HEADS, HEAD_DIM, w), (0, 3, 1, 2))


def kernel(x_prompt, x_sample, cache_k_win, cache_v_win, c_prompt, c_sample, w_ada, b_ada, w_in, attn_sinks, sgu_ln_g, sgu_ln_b, w_s, b_s, w_o, ln1_g, ln1_b, w_gate, w_up, w_down, ln2_g, ln2_b):
    bsz, seq, _ = x_prompt.shape
    nsm, t, _ = x_sample.shape
    tq, bb = 256, 32
    tabs_p = (*_angle_tables(np.arange(tq)), *_angle_tables(np.arange(seq // tq) * tq))
    tabs_s = _angle_tables(np.tile(PAST_LEN + np.arange(t), bb))

    yp, ys = x_prompt, x_sample
    kwp, vwp, kws, vws, sgv = [], [], [], [], []
    for l in range(DEPTH):
        bexp = jnp.repeat(b_s[l].T, HEAD_DIM, axis=1)
        wsx = jnp.repeat(jnp.transpose(jnp.tril(w_s[l, :, :t, :t]), (2, 1, 0)), HEAD_DIM, axis=-1)
        sinks = attn_sinks[l]
        sg, sb = sgu_ln_g[l][None, :], sgu_ln_b[l][None, :]
        g1, b1 = ln1_g[l][None, :], ln1_b[l][None, :]
        g2, b2 = ln2_g[l][None, :], ln2_b[l][None, :]

        mod_s, mod_p = _ada(c_sample, c_prompt, w_ada, b_ada, l)

        yp, kw, vw, w_in_b, w_o_b, wg, wu, wd = _prompt_layer(
            yp, mod_p, sinks, tabs_p, (w_in, w_o, w_gate, w_up, w_down), w_s, bexp, sg, sb,
            g1, b1, g2, b2, tq, l)
        kwp.append(_from_feature_major(kw))
        vwp.append(_from_feature_major(vw))

        ys, kn, vn, gvs = _sample_layer(
            ys, mod_s, sinks, tabs_s, _to_feature_major(cache_k_win[l]),
            _to_feature_major(cache_v_win[l]), w_in_b, wsx, bexp[:t], sg, sb, w_o_b, g1, b1, wg, wu, wd, g2, b2, bb)
        kws.append(_from_feature_major(kn))
        vws.append(_from_feature_major(vn))
        sgv.append(gvs)
    return (yp, ys, jnp.stack(kwp, axis=0), jnp.stack(vwp, axis=0), jnp.stack(kws, axis=0),
            jnp.stack(vws, axis=0), jnp.stack(sgv, axis=0))
```

```python
import functools

import jax
import jax.numpy as jnp
import numpy as np
from jax import lax
from jax.experimental import pallas as pl
from jax.experimental.pallas import tpu as pltpu

D_MODEL = 1024
HEAD_DIM = 64
ATT_HEADS = 8
KV_HEADS = 2
Q_PER_KV = ATT_HEADS // KV_HEADS
ATT_WIDTH = ATT_HEADS * HEAD_DIM
KV_WIDTH = KV_HEADS * HEAD_DIM
GM_HEADS = 8
GM_WIDTH = GM_HEADS * HEAD_DIM
IN_WIDTH = ATT_WIDTH + 2 * KV_WIDTH + 2 * GM_WIDTH
D_FF = 2816
WINDOW = 128
CHUNK = 128
PAST_LEN = 16384
ROPE_THETA = 500000.0
ROT_DIM = HEAD_DIM // 4
DEPTH = 1
ALPHA = (2 * DEPTH) ** 0.25
LN_EPS = 1e-5
ATT_SCALE = HEAD_DIM ** -0.5

LANES = 128
VMEM_LIMIT_BYTES = 56 * 1024 * 1024
FF_SPLIT = 1280

BF16 = jnp.bfloat16
F32 = jnp.float32

HEADS_STRAIGHT = tuple(h for h in range(ATT_HEADS) if h % 2 == h // Q_PER_KV)
HEADS_SWAPPED = tuple(h for h in range(ATT_HEADS) if h % 2 != h // Q_PER_KV)


def _norm(x):
    mu = jnp.mean(x, axis=-1, keepdims=True)
    xc = x - mu
    var = jnp.mean(xc * xc, axis=-1, keepdims=True)
    return xc * lax.rsqrt(var + LN_EPS)


def _lo_half():
    return lax.broadcasted_iota(jnp.int32, (1, LANES), 1) < HEAD_DIM


def _rotary_lane_signs():
    half = ROT_DIM // 2
    l = lax.broadcasted_iota(jnp.int32, (1, LANES), 1) % HEAD_DIM
    up = jnp.where(l < half, -1.0, 0.0)
    dn = jnp.where((l >= half) & (l < ROT_DIM), 1.0, 0.0)
    return up, dn


def _rotary(x, cos, sin_up, sin_dn):
    half = ROT_DIM // 2
    return (x * cos + pltpu.roll(x, LANES - half, 1) * sin_up
            + pltpu.roll(x, half, 1) * sin_dn)


def _swap_halves(x):
    return pltpu.roll(x, HEAD_DIM, x.ndim - 1)


def _masked_heads(qcols, heads):
    lo = _lo_half()
    parts = []
    for h in heads:
        qc = qcols[h // 2]
        parts.append(jnp.where(lo, qc, 0.0) if h % 2 == 0 else jnp.where(lo, 0.0, qc))
    return jnp.concatenate(parts, axis=-2).astype(BF16)


PREP_ROWS = 256
PREP_SLOTS = 4


def _load_weights_bf16(pairs, stage_ref, sem_ref, scaled_cols):
    chunks = [(n, r0) for n, (_, dst) in enumerate(pairs) for r0 in range(0, dst.shape[0], PREP_ROWS)]

    def chunk_copy(idx):
        n, r0 = chunks[idx]
        hbm, dst = pairs[n]
        slot = idx % PREP_SLOTS
        return pltpu.make_async_copy(
            hbm.at[pl.ds(r0, PREP_ROWS), :],
            stage_ref.at[slot, :, pl.ds(0, dst.shape[1])], sem_ref.at[slot])

    for idx in range(min(PREP_SLOTS, len(chunks))):
        chunk_copy(idx).start()
    for idx, (n, r0) in enumerate(chunks):
        dst = pairs[n][1]
        cols = dst.shape[1]
        chunk_copy(idx).wait()
        v = stage_ref[idx % PREP_SLOTS, :, pl.ds(0, cols)]
        if scaled_cols[n]:
            col = lax.broadcasted_iota(jnp.int32, (1, cols), 1)
            v = v * jnp.where(col < scaled_cols[n], ATT_SCALE, 1.0)
        dst[pl.ds(r0, PREP_ROWS), :] = v.astype(BF16)
        if idx + PREP_SLOTS < len(chunks):
            chunk_copy(idx + PREP_SLOTS).start()


ADA_K_CHUNK = 256


def _ada_kernel(cs_ref, cp_ref, w_ref, b_ref, os_ref, op_ref):
    nvec = os_ref.shape[0]
    vec_cols = [slice(r * D_MODEL, (r + 1) * D_MODEL) for r in range(nvec)]

    @pl.when(pl.program_id(0) == 0)
    def _():
        for o_ref in (os_ref, op_ref):
            for r, cols in enumerate(vec_cols):
                o_ref[r] = jnp.broadcast_to(b_ref[:, cols], o_ref.shape[1:])

    w = w_ref[...].astype(BF16)
    for c_ref, o_ref in ((cs_ref, os_ref), (cp_ref, op_ref)):
        c = c_ref[...]
        a = (c * jax.nn.sigmoid(c)).astype(BF16)
        part = jnp.dot(a, w, preferred_element_type=F32)
        for r, cols in enumerate(vec_cols):
            o_ref[r] += part[:, cols]


def _ada(c_sample, c_prompt, w_ada, b_ada, layer):
    n = w_ada.shape[2]
    nvec = n // D_MODEL
    kchunk = lambda c: pl.BlockSpec((c.shape[0], ADA_K_CHUNK), lambda k: (0, k))
    acc = lambda c: pl.BlockSpec((nvec, c.shape[0], D_MODEL), lambda k: (0, 0, 0))
    return pl.pallas_call(
        _ada_kernel,
        grid=(D_MODEL // ADA_K_CHUNK,),
        in_specs=[
            kchunk(c_sample), kchunk(c_prompt),
            pl.BlockSpec((None, ADA_K_CHUNK, n), lambda k: (layer, k, 0)),
            pl.BlockSpec((1, n), lambda k: (layer, 0)),
        ],
        out_specs=[acc(c_sample), acc(c_prompt)],
        out_shape=[jax.ShapeDtypeStruct((nvec, c.shape[0], D_MODEL), F32) for c in (c_sample, c_prompt)],
        compiler_params=pltpu.CompilerParams(dimension_semantics=("arbitrary",)),
        name="ada",
    )(c_sample, c_prompt, w_ada, b_ada)


def _attn_scores(qcols, kcat, kcat_sw):
    dn = (((1,), (1,)), ((), ()))
    s_a = lax.dot_general(_masked_heads(qcols, HEADS_STRAIGHT), kcat, dn, preferred_element_type=F32)
    s_b = lax.dot_general(_masked_heads(qcols, HEADS_SWAPPED), kcat_sw, dn, preferred_element_type=F32)
    return s_a, s_b


def _softmax_weights(s, first_key, sinks_ref, heads):
    row = lax.broadcasted_iota(jnp.int32, (WINDOW, 2 * WINDOW), 0)
    col = lax.broadcasted_iota(jnp.int32, (WINDOW, 2 * WINDOW), 1)
    mask = (col > row) & (col <= row + WINDOW) & (col >= first_key)
    es, invs = [], []
    for n, h in enumerate(heads):
        sink = sinks_ref[h]
        sh = jnp.where(mask, s[n * WINDOW:(n + 1) * WINDOW], -jnp.inf)
        m = jnp.maximum(jnp.max(sh, axis=-1, keepdims=True), sink)
        e = jnp.exp(sh - m)
        den = jnp.sum(e, axis=-1, keepdims=True) + jnp.exp(sink - m)
        es.append(e.astype(BF16))
        invs.append(1.0 / den)
    return jnp.concatenate(es, axis=0), invs


def _attn_merge(o_a, inv_a, o_b, inv_b):
    o = {}
    for heads, oo, inv in ((HEADS_STRAIGHT, o_a, inv_a), (HEADS_SWAPPED, o_b, inv_b)):
        for n, h in enumerate(heads):
            o[h] = oo[n * WINDOW:(n + 1) * WINDOW] * inv[n]
    lo = _lo_half()
    return jnp.concatenate([jnp.where(lo, o[2 * c], o[2 * c + 1]) for c in range(4)], axis=1)


def _prompt_sgu(uj, gvj, wsp_ref, bexp):
    lo = _lo_half()
    svs = []
    for p in range(4):
        g = gvj[:, p * LANES:(p + 1) * LANES]
        rhs = jnp.concatenate([jnp.where(lo, g, 0.0), jnp.where(lo, 0.0, g)], axis=0).astype(BF16)
        svs.append(jnp.dot(wsp_ref[p], rhs, preferred_element_type=F32))
    sv = jnp.concatenate(svs, axis=1)
    return uj * (sv + bexp)


def _swiglu(hf, wg_ref, wu_ref, wd_ref):
    f = None
    for lo, hi in ((0, FF_SPLIT), (FF_SPLIT, D_FF)):
        g = jnp.dot(hf, wg_ref[:, lo:hi], preferred_element_type=F32)
        up = jnp.dot(hf, wu_ref[:, lo:hi], preferred_element_type=F32)
        a = (g * jax.nn.sigmoid(g) * up).astype(BF16)
        d = jnp.dot(a, wd_ref[lo:hi, :], preferred_element_type=F32)
        f = d if f is None else f + d
    return f


def _prompt_kernel(sinks_ref, x_ref, xp_ref, mod_ref, cr_ref, sr_ref, cb_ref, sb_ref,
                   w_in_hbm, w_o_hbm, wg_hbm, wu_hbm, wd_hbm, ws_ref,
                   bexp_ref, sg_ref, sb2_ref, g1_ref, b1_ref, g2_ref, b2_ref,
                   y_ref, kwin_ref, vwin_ref, w_in_out, w_o_out, wg_out, wu_out, wd_out,
                   w_in_ref, w_o_ref, wg_ref, wu_ref, wd_ref, wsp_ref, stage_ref, load_sem, out_sem,
                   mixcat_ref, kprev_ref, vprev_ref, kprev_sw_ref, vprev_sw_ref,
                   *, tq, nb, nblocks, layer):
    s = pl.program_id(0)
    sm = jnp.minimum(s, nblocks - 1)
    b, i = sm // nb, sm % nb
    bp = jnp.maximum(s - 1, 0) // nb
    prev_refs = (kprev_ref, vprev_ref, kprev_sw_ref, vprev_sw_ref)
    exports = ((w_in_ref, w_in_out), (w_o_ref, w_o_out), (wg_ref, wg_out), (wu_ref, wu_out),
               (wd_ref, wd_out))

    def export_copy(n):
        return pltpu.make_async_copy(exports[n][0], exports[n][1], out_sem.at[n])

    @pl.when(s == 0)
    def _():
        for r in prev_refs:
            r[...] = jnp.zeros_like(r)
        sources = (w_in_hbm, w_o_hbm, wg_hbm, wu_hbm, wd_hbm)
        _load_weights_bf16([(src.at[layer], dst) for src, (dst, _) in zip(sources, exports)],
                           stage_ref, load_sem, scaled_cols=(ATT_WIDTH, 0, 0, 0, 0))
        for n in range(len(exports)):
            export_copy(n).start()
        row = lax.broadcasted_iota(jnp.int32, (CHUNK, CHUNK), 0)
        col = lax.broadcasted_iota(jnp.int32, (CHUNK, CHUNK), 1)
        for hd in range(GM_HEADS):
            wsp_ref[hd // 2, :, (hd % 2) * CHUNK:(hd % 2 + 1) * CHUNK] = jnp.where(
                col <= row, ws_ref[hd], 0.0).astype(BF16)

    def mixer_project(st):
        modm = [mod_ref[r, pl.ds(b, 1), :] for r in range(2)]
        h = (_norm(x_ref[...]) * (1.0 + modm[1]) + modm[0]).astype(BF16)
        z = jnp.dot(h, w_in_ref[...], preferred_element_type=F32)
        cb, sb = cb_ref[pl.ds(i, 1), :], sb_ref[pl.ds(i, 1), :]
        cr, sr = cr_ref[...], sr_ref[...]
        cos = cb * cr - sb * sr
        sin = sb * cr + cb * sr
        sgn_up, sgn_dn = _rotary_lane_signs()
        sup, sdn = sin * sgn_up, sin * sgn_dn
        st["q"] = [_rotary(z[:, c * LANES:(c + 1) * LANES], cos, sup, sdn) for c in range(4)]
        k = _rotary(z[:, ATT_WIDTH:ATT_WIDTH + KV_WIDTH], cos, sup, sdn)
        v = z[:, ATT_WIDTH + KV_WIDTH:ATT_WIDTH + 2 * KV_WIDTH]
        st["u"] = jax.nn.gelu(z[:, ATT_WIDTH + 2 * KV_WIDTH:ATT_WIDTH + 2 * KV_WIDTH + GM_WIDTH])
        st["gv"] = (_norm(jax.nn.gelu(z[:, ATT_WIDTH + 2 * KV_WIDTH + GM_WIDTH:])) * sg_ref[...]
                    + sb2_ref[...])
        st["k"], st["v"] = k, v
        st["forms"] = (k.astype(BF16), v.astype(BF16),
                       _swap_halves(k).astype(BF16), _swap_halves(v).astype(BF16))

    def mixer_scores(st):
        forms = st["forms"]
        kb, vb, kb_sw, vb_sw = forms
        bexp = bexp_ref[...]
        st["scores"], st["values"], st["sgus"] = [], [], []
        for j in range(tq // WINDOW):
            sl = slice(j * WINDOW, (j + 1) * WINDOW)
            if j == 0:
                kcat, vcat, kcat_sw, vcat_sw = (
                    jnp.concatenate([p[...], c[sl]], axis=0) for p, c in zip(prev_refs, forms))
            else:
                sl2 = slice((j - 1) * WINDOW, (j + 1) * WINDOW)
                kcat, vcat, kcat_sw, vcat_sw = kb[sl2], vb[sl2], kb_sw[sl2], vb_sw[sl2]
            st["scores"].append(_attn_scores([qc[sl] for qc in st["q"]], kcat, kcat_sw))
            st["values"].append((vcat, vcat_sw))
            st["sgus"].append(_prompt_sgu(st["u"][sl], st["gv"][sl], wsp_ref, bexp))

    def mixer_finish(st):
        for j in range(tq // WINDOW):
            first_key = jnp.where(i > 0, 0, WINDOW) if j == 0 else 0
            p_a, inv_a = _softmax_weights(st["scores"][j][0], first_key, sinks_ref, HEADS_STRAIGHT)
            p_b, inv_b = _softmax_weights(st["scores"][j][1], first_key, sinks_ref, HEADS_SWAPPED)
            o_a = jnp.dot(p_a, st["values"][j][0], preferred_element_type=F32)
            o_b = jnp.dot(p_b, st["values"][j][1], preferred_element_type=F32)
            att = _attn_merge(o_a, inv_a, o_b, inv_b)
            mixcat_ref[j * WINDOW:(j + 1) * WINDOW, :] = jnp.concatenate(
                [att, st["sgus"][j]], axis=1).astype(BF16)
        for r, fm in zip(prev_refs, st["forms"]):
            r[...] = fm[tq - WINDOW:]
        kwin_ref[...] = st["k"][tq - WINDOW:].T
        vwin_ref[...] = st["v"][tq - WINDOW:].T

    def ffn_project(st):
        st["mix"] = jnp.dot(mixcat_ref[...], w_o_ref[...], preferred_element_type=F32)

    def ffn_gate_up(st):
        modp = [mod_ref[r, pl.ds(bp, 1), :] for r in range(6)]
        y1 = _norm(ALPHA * xp_ref[...] + modp[2] * st["mix"]) * g1_ref[...] + b1_ref[...]
        hf = (_norm(y1) * (1.0 + modp[4]) + modp[3]).astype(BF16)
        st["acts"] = []
        for cols in (slice(0, FF_SPLIT), slice(FF_SPLIT, D_FF)):
            g = jnp.dot(hf, wg_ref[:, cols], preferred_element_type=F32)
            up = jnp.dot(hf, wu_ref[:, cols], preferred_element_type=F32)
            st["acts"].append((g * jax.nn.sigmoid(g) * up).astype(BF16))
        st["y1"], st["gate2"] = y1, modp[5]

    def ffn_finish(st):
        f = (jnp.dot(st["acts"][0], wd_ref[0:FF_SPLIT, :], preferred_element_type=F32)
             + jnp.dot(st["acts"][1], wd_ref[FF_SPLIT:D_FF, :], preferred_element_type=F32))
        y_ref[...] = _norm(ALPHA * st["y1"] + st["gate2"] * f) * g2_ref[...] + b2_ref[...]

    def run(phases):
        st = {}
        for phase in phases:
            phase(st)

    @pl.when(s == 0)
    def _():
        run((mixer_project, mixer_scores, mixer_finish))

    @pl.when((s > 0) & (s < nblocks))
    def _():
        run((ffn_project, mixer_project, ffn_gate_up, mixer_scores, ffn_finish, mixer_finish))

    @pl.when(s == nblocks)
    def _():
        for n in range(len(exports)):
            export_copy(n).wait()
        run((ffn_project, ffn_gate_up, ffn_finish))


def _const_spec(shape):
    nd = len(shape)
    return pl.BlockSpec(shape, lambda *_: (0,) * nd)


def _prompt_layer(x, mod, sinks, tabs, weights, w_s, bexp, sgu_g, sgu_b, g1, b1, g2, b2, tq, layer):
    bsz, seq, _ = x.shape
    nb = seq // tq
    nblocks = bsz * nb
    carry = pltpu.VMEM((WINDOW, KV_WIDTH), BF16)
    wshapes = [w.shape[1:] for w in weights]
    widest = max(c for _, c in wshapes)
    assert all(r % PREP_ROWS == 0 for r, _ in wshapes)

    def this_block(s):
        sm = jnp.minimum(s, nblocks - 1)
        return sm // nb, sm % nb, 0

    def prev_block(s):
        sp = jnp.maximum(s - 1, 0)
        return sp // nb, sp % nb, 0

    consts = (bexp, sgu_g, sgu_b, g1, b1, g2, b2)
    win_spec = pl.BlockSpec((None, KV_WIDTH, WINDOW),
                            lambda s: (jnp.minimum(s, nblocks - 1) // nb, 0, 0))
    hbm = pl.BlockSpec(memory_space=pl.ANY)
    return pl.pallas_call(
        functools.partial(_prompt_kernel, tq=tq, nb=nb, nblocks=nblocks, layer=layer),
        grid=(nblocks + 1,),
        in_specs=[
            pl.BlockSpec(memory_space=pltpu.SMEM),
            pl.BlockSpec((None, tq, D_MODEL), this_block),
            pl.BlockSpec((None, tq, D_MODEL), prev_block),
            _const_spec(mod.shape),
            *[_const_spec(c.shape) for c in tabs],
            *[hbm for _ in weights],
            pl.BlockSpec((None,) + w_s.shape[1:], lambda s: (layer, 0, 0, 0)),
            *[_const_spec(c.shape) for c in consts],
        ],
        out_specs=[pl.BlockSpec((None, tq, D_MODEL), prev_block), win_spec, win_spec,
                   *[hbm for _ in weights]],
        out_shape=[
            jax.ShapeDtypeStruct((bsz, seq, D_MODEL), F32),
            jax.ShapeDtypeStruct((bsz, KV_WIDTH, WINDOW), F32),
            jax.ShapeDtypeStruct((bsz, KV_WIDTH, WINDOW), F32),
            *[jax.ShapeDtypeStruct(shp, BF16) for shp in wshapes],
        ],
        scratch_shapes=[
            *[pltpu.VMEM(shp, BF16) for shp in wshapes],
            pltpu.VMEM((GM_HEADS // 2, CHUNK, 2 * CHUNK), BF16),
            pltpu.VMEM((PREP_SLOTS, PREP_ROWS, widest), F32),
            pltpu.SemaphoreType.DMA((PREP_SLOTS,)),
            pltpu.SemaphoreType.DMA((len(weights),)),
            pltpu.VMEM((tq, D_MODEL), BF16), carry, carry, carry, carry,
        ],
        compiler_params=pltpu.CompilerParams(
            dimension_semantics=("arbitrary",), vmem_limit_bytes=VMEM_LIMIT_BYTES),
        name="prompt_layer",
    )(sinks, x, x, mod, *tabs, *weights, w_s, *consts)


def _shifted_window(old, new2, bb, t):
    wb = old.shape[2]
    lane = lax.broadcasted_iota(jnp.int32, (1, wb), 1)
    shifted = pltpu.roll(old.reshape(bb * KV_WIDTH, wb), wb - t, 1).reshape(bb, KV_WIDTH, wb)
    new_t = new2.T
    outs = []
    for n in range(bb):
        tile = new_t[:, (n * t // LANES) * LANES:(n * t // LANES + 1) * LANES]
        tail = pltpu.roll(tile, (wb - t - (n * t) % LANES) % LANES, 1)
        outs.append(jnp.where(lane >= wb - t, tail, shifted[n]))
    return outs


def _sample_kernel(sinks_ref, x_ref, mod_ref, cos_ref, sin_ref, kt_ref, vt_ref,
                   w_in_ref, wsx_ref, bexp_ref, sg_ref, sb_ref, w_o_ref, g1_ref, b1_ref,
                   wg_ref, wu_ref, wd_ref, g2_ref, b2_ref,
                   y_ref, ktn_ref, vtn_ref, gv_ref, mixcat_ref, *, bb, t):
    wb = kt_ref.shape[2]
    xp_ref, modp_ref = x_ref, mod_ref

    def mixer_project(st):
        x3 = x_ref[...]
        shift, scale = (mod_ref[r][:, None, :] for r in range(2))
        h = (_norm(x3) * (1.0 + scale) + shift).reshape(bb * t, D_MODEL).astype(BF16)
        z = jnp.dot(h, w_in_ref[...], preferred_element_type=F32)
        cos, sin = cos_ref[...], sin_ref[...]
        sgn_up, sgn_dn = _rotary_lane_signs()
        sup, sdn = sin * sgn_up, sin * sgn_dn
        st["qcols"] = [_rotary(z[:, c * LANES:(c + 1) * LANES], cos, sup, sdn).reshape(bb, t, LANES)
                       for c in range(4)]
        k2 = _rotary(z[:, ATT_WIDTH:ATT_WIDTH + KV_WIDTH], cos, sup, sdn)
        v2 = z[:, ATT_WIDTH + KV_WIDTH:ATT_WIDTH + 2 * KV_WIDTH]
        st["u3"] = jax.nn.gelu(z[:, ATT_WIDTH + 2 * KV_WIDTH:ATT_WIDTH + 2 * KV_WIDTH + GM_WIDTH]
                               ).reshape(bb, t, GM_WIDTH)
        gv = _norm(jax.nn.gelu(z[:, ATT_WIDTH + 2 * KV_WIDTH + GM_WIDTH:])) * sg_ref[...] + sb_ref[...]
        st["gv3"] = gv.reshape(bb, t, GM_WIDTH)
        gv_ref[...] = st["gv3"]
        st["k2"], st["v2"] = k2, v2
        st["kv"] = (kt_ref[...].astype(BF16), vt_ref[...].astype(BF16),
                    k2.reshape(bb, t, KV_WIDTH).astype(BF16), v2.reshape(bb, t, KV_WIDTH).astype(BF16))

    def update_windows(st):
        for n, (ko, vo) in enumerate(zip(_shifted_window(kt_ref[...], st["k2"], bb, t),
                                         _shifted_window(vt_ref[...], st["v2"], bb, t))):
            ktn_ref[n] = ko
            vtn_ref[n] = vo

    def mixer_scores(st):
        lo = _lo_half()[None]
        parts = []
        for hd in range(ATT_HEADS):
            qc = st["qcols"][hd // 2]
            if hd in HEADS_SWAPPED:
                qc = _swap_halves(qc)
            parts.append(jnp.where(lo, qc, 0.0) if hd // Q_PER_KV == 0 else jnp.where(lo, 0.0, qc))
        lhs = jnp.concatenate(parts, axis=1).astype(BF16)
        kb, _, kn, _ = st["kv"]
        st["scores"] = (jnp.einsum("bqd,bdw->bqw", lhs, kb, preferred_element_type=F32),
                        jnp.einsum("bqd,bkd->bqk", lhs, kn, preferred_element_type=F32))

    def mixer_finish(st):
        nrow = ATT_HEADS * t
        tok = lax.broadcasted_iota(jnp.int32, (nrow, 1), 0) % t
        head_row = lax.broadcasted_iota(jnp.int32, (nrow, 1), 0) // t
        mask_c = (lax.broadcasted_iota(jnp.int32, (nrow, wb), 1) + (WINDOW - wb) > tok)[None]
        mask_n = (lax.broadcasted_iota(jnp.int32, (nrow, t), 1) <= tok)[None]
        _, vb, _, vn = st["kv"]
        s_c = jnp.where(mask_c, st["scores"][0], -jnp.inf)
        s_n = jnp.where(mask_n, st["scores"][1], -jnp.inf)
        sink = jnp.full((nrow, 1), sinks_ref[0], F32)
        for hd in range(1, ATT_HEADS):
            sink = jnp.where(head_row == hd, sinks_ref[hd], sink)
        sink = sink[None]
        m = jnp.maximum(jnp.maximum(jnp.max(s_c, axis=-1, keepdims=True),
                                    jnp.max(s_n, axis=-1, keepdims=True)), sink)
        e_c, e_n = jnp.exp(s_c - m), jnp.exp(s_n - m)
        inv = 1.0 / (jnp.sum(e_c, axis=-1, keepdims=True) + jnp.sum(e_n, axis=-1, keepdims=True)
                     + jnp.exp(sink - m))
        oh = (jnp.einsum("bqw,bdw->bqd", e_c.astype(BF16), vb, preferred_element_type=F32)
              + jnp.einsum("bqk,bkd->bqd", e_n.astype(BF16), vn, preferred_element_type=F32)) * inv
        o = [oh[:, hd * t:(hd + 1) * t, :] for hd in range(ATT_HEADS)]
        o = [_swap_halves(o[hd]) if hd in HEADS_SWAPPED else o[hd] for hd in range(ATT_HEADS)]
        lo = _lo_half()[None]
        att = jnp.concatenate([jnp.where(lo, o[2 * c], o[2 * c + 1]) for c in range(4)], axis=-1)
        sv = jnp.zeros((bb, t, GM_WIDTH), F32)
        for si in range(t):
            sv = sv + wsx_ref[si][None] * st["gv3"][:, si:si + 1, :]
        sgu = st["u3"] * (sv + bexp_ref[...][None])
        mixcat_ref[...] = jnp.concatenate([att, sgu], axis=-1).reshape(bb * t, D_MODEL).astype(BF16)

    def ffn_project(st):
        st["mix"] = jnp.dot(mixcat_ref[...], w_o_ref[...], preferred_element_type=F32
                            ).reshape(bb, t, D_MODEL)

    def gate_up(st, cols):
        g = jnp.dot(st["hf"], wg_ref[:, cols], preferred_element_type=F32)
        up = jnp.dot(st["hf"], wu_ref[:, cols], preferred_element_type=F32)
        st["acts"].append((g * jax.nn.sigmoid(g) * up).astype(BF16))

    def ffn_gate_up_a(st):
        mod = [modp_ref[r][:, None, :] for r in range(6)]
        y1 = _norm(ALPHA * xp_ref[...] + mod[2] * st["mix"]) * g1_ref[...] + b1_ref[...]
        st["hf"] = (_norm(y1) * (1.0 + mod[4]) + mod[3]).reshape(bb * t, D_MODEL).astype(BF16)
        st["y1"], st["gate2"], st["acts"] = y1, mod[5], []
        gate_up(st, slice(0, FF_SPLIT))

    def ffn_gate_up_b(st):
        gate_up(st, slice(FF_SPLIT, D_FF))

    def ffn_finish(st):
        f = (jnp.dot(st["acts"][0], wd_ref[0:FF_SPLIT, :], preferred_element_type=F32)
             + jnp.dot(st["acts"][1], wd_ref[FF_SPLIT:D_FF, :], preferred_element_type=F32)
             ).reshape(bb, t, D_MODEL)
        y_ref[...] = _norm(ALPHA * st["y1"] + st["gate2"] * f) * g2_ref[...] + b2_ref[...]

    def run(phases):
        st = {}
        for phase in phases:
            phase(st)

    run((mixer_project, mixer_scores, update_windows, mixer_finish,
         ffn_project, ffn_gate_up_a, ffn_gate_up_b, ffn_finish))


def _sample_layer(x, mod, sinks, tabs, kt, vt, w_in_b, wsx, bexp_t, sgu_g, sgu_b, w_o_b,
                  g1, b1, wg, wu, wd, g2, b2, bb):
    n, t, _ = x.shape
    wb = kt.shape[2]
    assert wb == LANES and (bb * t) % LANES == 0 and LANES % t == 0
    rows = lambda r, c: pl.BlockSpec((bb, r, c), lambda s: (s, 0, 0))
    consts2 = (w_in_b, wsx, bexp_t, sgu_g, sgu_b, w_o_b, g1, b1, wg, wu, wd, g2, b2)
    return pl.pallas_call(
        functools.partial(_sample_kernel, bb=bb, t=t),
        grid=(n // bb,),
        in_specs=[
            pl.BlockSpec(memory_space=pltpu.SMEM),
            rows(t, D_MODEL), pl.BlockSpec((6, bb, D_MODEL), lambda s: (0, s, 0)),
            *[_const_spec(a.shape) for a in tabs],
            rows(KV_WIDTH, wb), rows(KV_WIDTH, wb),
            *[_const_spec(a.shape) for a in consts2],
        ],
        out_specs=[rows(t, D_MODEL), rows(KV_WIDTH, wb), rows(KV_WIDTH, wb), rows(t, GM_WIDTH)],
        out_shape=[
            jax.ShapeDtypeStruct((n, t, D_MODEL), F32),
            jax.ShapeDtypeStruct((n, KV_WIDTH, wb), F32),
            jax.ShapeDtypeStruct((n, KV_WIDTH, wb), F32),
            jax.ShapeDtypeStruct((n, t, GM_WIDTH), F32),
        ],
        scratch_shapes=[pltpu.VMEM((bb * t, D_MODEL), BF16)],
        compiler_params=pltpu.CompilerParams(
            dimension_semantics=("arbitrary",), vmem_limit_bytes=VMEM_LIMIT_BYTES),
        name="sample_layer",
    )(sinks, x, mod, *tabs, kt, vt, *consts2)


def _lane_freqs():
    half = ROT_DIM // 2
    inv_freq = np.power(np.float32(ROPE_THETA), -np.arange(half, dtype=np.float32) * np.float32(2.0 / ROT_DIM))
    per_head = np.concatenate([inv_freq, inv_freq, np.zeros((HEAD_DIM - ROT_DIM,), np.float32)])
    return np.concatenate([per_head, per_head]).astype(np.float32)


def _angle_tables(pos):
    ang = np.asarray(pos, np.float32)[:, None] * _lane_freqs()[None, :]
    return jnp.asarray(np.cos(ang), F32), jnp.asarray(np.sin(ang), F32)


def _to_feature_major(win):
    n, w = win.shape[0], win.shape[1]
    return jnp.transpose(win, (0, 2, 3, 1)).reshape(n, KV_WIDTH, w)


def _from_feature_major(win_t):
    n, _, w = win_t.shape
    return jnp.transpose(win_t.reshape(n, KV_HEADS, HEAD_DIM, w), (0, 3, 1, 2))


def kernel(x_prompt, x_sample, cache_k_win, cache_v_win, c_prompt, c_sample, w_ada, b_ada, w_in, attn_sinks, sgu_ln_g, sgu_ln_b, w_s, b_s, w_o, ln1_g, ln1_b, w_gate, w_up, w_down, ln2_g, ln2_b):
    bsz, seq, _ = x_prompt.shape
    nsm, t, _ = x_sample.shape
    tq, bb = 256, 32
    tabs_p = (*_angle_tables(np.arange(tq)), *_angle_tables(np.arange(seq // tq) * tq))
    tabs_s = _angle_tables(np.tile(PAST_LEN + np.arange(t), bb))

    yp, ys = x_prompt, x_sample
    kwp, vwp, kws, vws, sgv = [], [], [], [], []
    for l in range(DEPTH):
        bexp = jnp.repeat(b_s[l].T, HEAD_DIM, axis=1)
        wsx = jnp.repeat(jnp.transpose(jnp.tril(w_s[l, :, :t, :t]), (2, 1, 0)), HEAD_DIM, axis=-1)
        sinks = attn_sinks[l]
        sg, sb = sgu_ln_g[l][None, :], sgu_ln_b[l][None, :]
        g1, b1 = ln1_g[l][None, :], ln1_b[l][None, :]
        g2, b2 = ln2_g[l][None, :], ln2_b[l][None, :]

        mod_s, mod_p = _ada(c_sample, c_prompt, w_ada, b_ada, l)

        yp, kw, vw, w_in_b, w_o_b, wg, wu, wd = _prompt_layer(
            yp, mod_p, sinks, tabs_p, (w_in, w_o, w_gate, w_up, w_down), w_s, bexp, sg, sb,
            g1, b1, g2, b2, tq, l)
        kwp.append(_from_feature_major(kw))
        vwp.append(_from_feature_major(vw))

        ys, kn, vn, gvs = _sample_layer(
            ys, mod_s, sinks, tabs_s, _to_feature_major(cache_k_win[l]),
            _to_feature_major(cache_v_win[l]), w_in_b, wsx, bexp[:t], sg, sb, w_o_b, g1, b1, wg, wu, wd, g2, b2, bb)
        kws.append(_from_feature_major(kn))
        vws.append(_from_feature_major(vn))
        sgv.append(gvs)
    return (yp, ys, jnp.stack(kwp, axis=0), jnp.stack(vwp, axis=0), jnp.stack(kws, axis=0),
            jnp.stack(vws, axis=0), jnp.stack(sgv, axis=0))
```

```python
import functools

import jax
import jax.numpy as jnp
import numpy as np
from jax import lax
from jax.experimental import pallas as pl
from jax.experimental.pallas import tpu as pltpu

D_MODEL = 1024
HEAD_DIM = 64
ATT_HEADS = 8
KV_HEADS = 2
Q_PER_KV = ATT_HEADS // KV_HEADS
ATT_WIDTH = ATT_HEADS * HEAD_DIM
KV_WIDTH = KV_HEADS * HEAD_DIM
GM_HEADS = 8
GM_WIDTH = GM_HEADS * HEAD_DIM
IN_WIDTH = ATT_WIDTH + 2 * KV_WIDTH + 2 * GM_WIDTH
D_FF = 2816
WINDOW = 128
CHUNK = 128
PAST_LEN = 16384
ROPE_THETA = 500000.0
ROT_DIM = HEAD_DIM // 4
DEPTH = 1
ALPHA = (2 * DEPTH) ** 0.25
LN_EPS = 1e-5
ATT_SCALE = HEAD_DIM ** -0.5

LANES = 128
VMEM_LIMIT_BYTES = 56 * 1024 * 1024
FF_SPLIT = 1280

BF16 = jnp.bfloat16
F32 = jnp.float32

HEADS_STRAIGHT = tuple(h for h in range(ATT_HEADS) if h % 2 == h // Q_PER_KV)
HEADS_SWAPPED = tuple(h for h in range(ATT_HEADS) if h % 2 != h // Q_PER_KV)


def _norm(x):
    mu = jnp.mean(x, axis=-1, keepdims=True)
    xc = x - mu
    var = jnp.mean(xc * xc, axis=-1, keepdims=True)
    return xc * lax.rsqrt(var + LN_EPS)


def _lo_half():
    return lax.broadcasted_iota(jnp.int32, (1, LANES), 1) < HEAD_DIM


def _rotary_lane_signs():
    half = ROT_DIM // 2
    l = lax.broadcasted_iota(jnp.int32, (1, LANES), 1) % HEAD_DIM
    up = jnp.where(l < half, -1.0, 0.0)
    dn = jnp.where((l >= half) & (l < ROT_DIM), 1.0, 0.0)
    return up, dn


def _rotary(x, cos, sin_up, sin_dn):
    half = ROT_DIM // 2
    return (x * cos + pltpu.roll(x, LANES - half, 1) * sin_up
            + pltpu.roll(x, half, 1) * sin_dn)


def _swap_halves(x):
    return pltpu.roll(x, HEAD_DIM, x.ndim - 1)


def _masked_heads(qcols, heads):
    lo = _lo_half()
    parts = []
    for h in heads:
        qc = qcols[h // 2]
        parts.append(jnp.where(lo, qc, 0.0) if h % 2 == 0 else jnp.where(lo, 0.0, qc))
    return jnp.concatenate(parts, axis=-2).astype(BF16)


PREP_ROWS = 256
PREP_SLOTS = 4


def _load_weights_bf16(pairs, stage_ref, sem_ref, scaled_cols):
    chunks = [(n, r0) for n, (_, dst) in enumerate(pairs) for r0 in range(0, dst.shape[0], PREP_ROWS)]

    def chunk_copy(idx):
        n, r0 = chunks[idx]
        hbm, dst = pairs[n]
        slot = idx % PREP_SLOTS
        return pltpu.make_async_copy(
            hbm.at[pl.ds(r0, PREP_ROWS), :],
            stage_ref.at[slot, :, pl.ds(0, dst.shape[1])], sem_ref.at[slot])

    for idx in range(min(PREP_SLOTS, len(chunks))):
        chunk_copy(idx).start()
    for idx, (n, r0) in enumerate(chunks):
        dst = pairs[n][1]
        cols = dst.shape[1]
        chunk_copy(idx).wait()
        v = stage_ref[idx % PREP_SLOTS, :, pl.ds(0, cols)]
        if scaled_cols[n]:
            col = lax.broadcasted_iota(jnp.int32, (1, cols), 1)
            v = v * jnp.where(col < scaled_cols[n], ATT_SCALE, 1.0)
        dst[pl.ds(r0, PREP_ROWS), :] = v.astype(BF16)
        if idx + PREP_SLOTS < len(chunks):
            chunk_copy(idx + PREP_SLOTS).start()


ADA_K_CHUNK = 256


def _ada_kernel(cs_ref, cp_ref, w_ref, b_ref, os_ref, op_ref):
    nvec = os_ref.shape[0]
    vec_cols = [slice(r * D_MODEL, (r + 1) * D_MODEL) for r in range(nvec)]

    @pl.when(pl.program_id(0) == 0)
    def _():
        for o_ref in (os_ref, op_ref):
            for r, cols in enumerate(vec_cols):
                o_ref[r] = jnp.broadcast_to(b_ref[:, cols], o_ref.shape[1:])

    w = w_ref[...].astype(BF16)
    for c_ref, o_ref in ((cs_ref, os_ref), (cp_ref, op_ref)):
        c = c_ref[...]
        a = (c * jax.nn.sigmoid(c)).astype(BF16)
        part = jnp.dot(a, w, preferred_element_type=F32)
        for r, cols in enumerate(vec_cols):
            o_ref[r] += part[:, cols]


def _ada(c_sample, c_prompt, w_ada, b_ada, layer):
    n = w_ada.shape[2]
    nvec = n // D_MODEL
    kchunk = lambda c: pl.BlockSpec((c.shape[0], ADA_K_CHUNK), lambda k: (0, k))
    acc = lambda c: pl.BlockSpec((nvec, c.shape[0], D_MODEL), lambda k: (0, 0, 0))
    return pl.pallas_call(
        _ada_kernel,
        grid=(D_MODEL // ADA_K_CHUNK,),
        in_specs=[
            kchunk(c_sample), kchunk(c_prompt),
            pl.BlockSpec((None, ADA_K_CHUNK, n), lambda k: (layer, k, 0)),
            pl.BlockSpec((1, n), lambda k: (layer, 0)),
        ],
        out_specs=[acc(c_sample), acc(c_prompt)],
        out_shape=[jax.ShapeDtypeStruct((nvec, c.shape[0], D_MODEL), F32) for c in (c_sample, c_prompt)],
        compiler_params=pltpu.CompilerParams(dimension_semantics=("arbitrary",)),
        name="ada",
    )(c_sample, c_prompt, w_ada, b_ada)


def _attn_scores(qcols, kcat, kcat_sw):
    dn = (((1,), (1,)), ((), ()))
    s_a = lax.dot_general(_masked_heads(qcols, HEADS_STRAIGHT), kcat, dn, preferred_element_type=F32)
    s_b = lax.dot_general(_masked_heads(qcols, HEADS_SWAPPED), kcat_sw, dn, preferred_element_type=F32)
    return s_a, s_b


def _softmax_weights(s, first_key, sinks_ref, heads):
    row = lax.broadcasted_iota(jnp.int32, (WINDOW, 2 * WINDOW), 0)
    col = lax.broadcasted_iota(jnp.int32, (WINDOW, 2 * WINDOW), 1)
    mask = (col > row) & (col <= row + WINDOW) & (col >= first_key)
    es, invs = [], []
    for n, h in enumerate(heads):
        sink = sinks_ref[h]
        sh = jnp.where(mask, s[n * WINDOW:(n + 1) * WINDOW], -jnp.inf)
        m = jnp.maximum(jnp.max(sh, axis=-1, keepdims=True), sink)
        e = jnp.exp(sh - m)
        den = jnp.sum(e, axis=-1, keepdims=True) + jnp.exp(sink - m)
        es.append(e.astype(BF16))
        invs.append(1.0 / den)
    return jnp.concatenate(es, axis=0), invs


def _attn_merge(o_a, inv_a, o_b, inv_b):
    o = {}
    for heads, oo, inv in ((HEADS_STRAIGHT, o_a, inv_a), (HEADS_SWAPPED, o_b, inv_b)):
        for n, h in enumerate(heads):
            o[h] = oo[n * WINDOW:(n + 1) * WINDOW] * inv[n]
    lo = _lo_half()
    return jnp.concatenate([jnp.where(lo, o[2 * c], o[2 * c + 1]) for c in range(4)], axis=1)


def _prompt_sgu(uj, gvj, wsp_ref, bexp):
    lo = _lo_half()
    svs = []
    for p in range(4):
        g = gvj[:, p * LANES:(p + 1) * LANES]
        rhs = jnp.concatenate([jnp.where(lo, g, 0.0), jnp.where(lo, 0.0, g)], axis=0).astype(BF16)
        svs.append(jnp.dot(wsp_ref[p], rhs, preferred_element_type=F32))
    sv = jnp.concatenate(svs, axis=1)
    return uj * (sv + bexp)


def _swiglu(hf, wg_ref, wu_ref, wd_ref):
    f = None
    for lo, hi in ((0, FF_SPLIT), (FF_SPLIT, D_FF)):
        g = jnp.dot(hf, wg_ref[:, lo:hi], preferred_element_type=F32)
        up = jnp.dot(hf, wu_ref[:, lo:hi], preferred_element_type=F32)
        a = (g * jax.nn.sigmoid(g) * up).astype(BF16)
        d = jnp.dot(a, wd_ref[lo:hi, :], preferred_element_type=F32)
        f = d if f is None else f + d
    return f


def _prompt_kernel(sinks_ref, x_ref, xp_ref, mod_ref, cr_ref, sr_ref, cb_ref, sb_ref,
                   w_in_hbm, w_o_hbm, wg_hbm, wu_hbm, wd_hbm, ws_ref,
                   bexp_ref, sg_ref, sb2_ref, g1_ref, b1_ref, g2_ref, b2_ref,
                   y_ref, kwin_ref, vwin_ref, w_in_out, w_o_out, wg_out, wu_out, wd_out,
                   w_in_ref, w_o_ref, wg_ref, wu_ref, wd_ref, wsp_ref, stage_ref, load_sem, out_sem,
                   mixcat_ref, kprev_ref, vprev_ref, kprev_sw_ref, vprev_sw_ref,
                   *, tq, nb, nblocks, layer):
    s = pl.program_id(0)
    sm = jnp.minimum(s, nblocks - 1)
    b, i = sm // nb, sm % nb
    bp = jnp.maximum(s - 1, 0) // nb
    prev_refs = (kprev_ref, vprev_ref, kprev_sw_ref, vprev_sw_ref)
    exports = ((w_in_ref, w_in_out), (w_o_ref, w_o_out), (wg_ref, wg_out), (wu_ref, wu_out),
               (wd_ref, wd_out))

    def export_copy(n):
        return pltpu.make_async_copy(exports[n][0], exports[n][1], out_sem.at[n])

    @pl.when(s == 0)
    def _():
        for r in prev_refs:
            r[...] = jnp.zeros_like(r)
        sources = (w_in_hbm, w_o_hbm, wg_hbm, wu_hbm, wd_hbm)
        _load_weights_bf16([(src.at[layer], dst) for src, (dst, _) in zip(sources, exports)],
                           stage_ref, load_sem, scaled_cols=(ATT_WIDTH, 0, 0, 0, 0))
        for n in range(len(exports)):
            export_copy(n).start()
        row = lax.broadcasted_iota(jnp.int32, (CHUNK, CHUNK), 0)
        col = lax.broadcasted_iota(jnp.int32, (CHUNK, CHUNK), 1)
        for hd in range(GM_HEADS):
            wsp_ref[hd // 2, :, (hd % 2) * CHUNK:(hd % 2 + 1) * CHUNK] = jnp.where(
                col <= row, ws_ref[hd], 0.0).astype(BF16)

    def mixer_project(st):
        modm = [mod_ref[r, pl.ds(b, 1), :] for r in range(2)]
        h = (_norm(x_ref[...]) * (1.0 + modm[1]) + modm[0]).astype(BF16)
        z = jnp.dot(h, w_in_ref[...], preferred_element_type=F32)
        cb, sb = cb_ref[pl.ds(i, 1), :], sb_ref[pl.ds(i, 1), :]
        cr, sr = cr_ref[...], sr_ref[...]
        cos = cb * cr - sb * sr
        sin = sb * cr + cb * sr
        sgn_up, sgn_dn = _rotary_lane_signs()
        sup, sdn = sin * sgn_up, sin * sgn_dn
        st["q"] = [_rotary(z[:, c * LANES:(c + 1) * LANES], cos, sup, sdn) for c in range(4)]
        k = _rotary(z[:, ATT_WIDTH:ATT_WIDTH + KV_WIDTH], cos, sup, sdn)
        v = z[:, ATT_WIDTH + KV_WIDTH:ATT_WIDTH + 2 * KV_WIDTH]
        st["u"] = jax.nn.gelu(z[:, ATT_WIDTH + 2 * KV_WIDTH:ATT_WIDTH + 2 * KV_WIDTH + GM_WIDTH])
        st["gv"] = (_norm(jax.nn.gelu(z[:, ATT_WIDTH + 2 * KV_WIDTH + GM_WIDTH:])) * sg_ref[...]
                    + sb2_ref[...])
        st["k"], st["v"] = k, v
        st["forms"] = (k.astype(BF16), v.astype(BF16),
                       _swap_halves(k).astype(BF16), _swap_halves(v).astype(BF16))

    def mixer_scores(st):
        forms = st["forms"]
        kb, vb, kb_sw, vb_sw = forms
        bexp = bexp_ref[...]
        st["scores"], st["values"], st["sgus"] = [], [], []
        for j in range(tq // WINDOW):
            sl = slice(j * WINDOW, (j + 1) * WINDOW)
            if j == 0:
                kcat, vcat, kcat_sw, vcat_sw = (
                    jnp.concatenate([p[...], c[sl]], axis=0) for p, c in zip(prev_refs, forms))
            else:
                sl2 = slice((j - 1) * WINDOW, (j + 1) * WINDOW)
                kcat, vcat, kcat_sw, vcat_sw = kb[sl2], vb[sl2], kb_sw[sl2], vb_sw[sl2]
            st["scores"].append(_attn_scores([qc[sl] for qc in st["q"]], kcat, kcat_sw))
            st["values"].append((vcat, vcat_sw))
            st["sgus"].append(_prompt_sgu(st["u"][sl], st["gv"][sl], wsp_ref, bexp))

    def mixer_finish(st):
        for j in range(tq // WINDOW):
            first_key = jnp.where(i > 0, 0, WINDOW) if j == 0 else 0
            p_a, inv_a = _softmax_weights(st["scores"][j][0], first_key, sinks_ref, HEADS_STRAIGHT)
            p_b, inv_b = _softmax_weights(st["scores"][j][1], first_key, sinks_ref, HEADS_SWAPPED)
            o_a = jnp.dot(p_a, st["values"][j][0], preferred_element_type=F32)
            o_b = jnp.dot(p_b, st["values"][j][1], preferred_element_type=F32)
            att = _attn_merge(o_a, inv_a, o_b, inv_b)
            mixcat_ref[j * WINDOW:(j + 1) * WINDOW, :] = jnp.concatenate(
                [att, st["sgus"][j]], axis=1).astype(BF16)
        for r, fm in zip(prev_refs, st["forms"]):
            r[...] = fm[tq - WINDOW:]
        kwin_ref[...] = st["k"][tq - WINDOW:].T
        vwin_ref[...] = st["v"][tq - WINDOW:].T

    def ffn_project(st):
        st["mix"] = jnp.dot(mixcat_ref[...], w_o_ref[...], preferred_element_type=F32)

    def ffn_gate_up(st):
        modp = [mod_ref[r, pl.ds(bp, 1), :] for r in range(6)]
        y1 = _norm(ALPHA * xp_ref[...] + modp[2] * st["mix"]) * g1_ref[...] + b1_ref[...]
        hf = (_norm(y1) * (1.0 + modp[4]) + modp[3]).astype(BF16)
        st["acts"] = []
        for cols in (slice(0, FF_SPLIT), slice(FF_SPLIT, D_FF)):
            g = jnp.dot(hf, wg_ref[:, cols], preferred_element_type=F32)
            up = jnp.dot(hf, wu_ref[:, cols], preferred_element_type=F32)
            st["acts"].append((g * jax.nn.sigmoid(g) * up).astype(BF16))
        st["y1"], st["gate2"] = y1, modp[5]

    def ffn_finish(st):
        f = (jnp.dot(st["acts"][0], wd_ref[0:FF_SPLIT, :], preferred_element_type=F32)
             + jnp.dot(st["acts"][1], wd_ref[FF_SPLIT:D_FF, :], preferred_element_type=F32))
        y_ref[...] = _norm(ALPHA * st["y1"] + st["gate2"] * f) * g2_ref[...] + b2_ref[...]

    def run(phases):
        st = {}
        for phase in phases:
            phase(st)

    @pl.when(s == 0)
    def _():
        run((mixer_project, mixer_scores, mixer_finish))

    @pl.when((s > 0) & (s < nblocks))
    def _():
        run((ffn_project, mixer_project, ffn_gate_up, mixer_scores, ffn_finish, mixer_finish))

    @pl.when(s == nblocks)
    def _():
        for n in range(len(exports)):
            export_copy(n).wait()
        run((ffn_project, ffn_gate_up, ffn_finish))


def _const_spec(shape):
    nd = len(shape)
    return pl.BlockSpec(shape, lambda *_: (0,) * nd)


def _prompt_layer(x, mod, sinks, tabs, weights, w_s, bexp, sgu_g, sgu_b, g1, b1, g2, b2, tq, layer):
    bsz, seq, _ = x.shape
    nb = seq // tq
    nblocks = bsz * nb
    carry = pltpu.VMEM((WINDOW, KV_WIDTH), BF16)
    wshapes = [w.shape[1:] for w in weights]
    widest = max(c for _, c in wshapes)
    assert all(r % PREP_ROWS == 0 for r, _ in wshapes)

    def this_block(s):
        sm = jnp.minimum(s, nblocks - 1)
        return sm // nb, sm % nb, 0

    def prev_block(s):
        sp = jnp.maximum(s - 1, 0)
        return sp // nb, sp % nb, 0

    consts = (bexp, sgu_g, sgu_b, g1, b1, g2, b2)
    win_spec = pl.BlockSpec((None, KV_WIDTH, WINDOW),
                            lambda s: (jnp.minimum(s, nblocks - 1) // nb, 0, 0))
    hbm = pl.BlockSpec(memory_space=pl.ANY)
    return pl.pallas_call(
        functools.partial(_prompt_kernel, tq=tq, nb=nb, nblocks=nblocks, layer=layer),
        grid=(nblocks + 1,),
        in_specs=[
            pl.BlockSpec(memory_space=pltpu.SMEM),
            pl.BlockSpec((None, tq, D_MODEL), this_block),
            pl.BlockSpec((None, tq, D_MODEL), prev_block),
            _const_spec(mod.shape),
            *[_const_spec(c.shape) for c in tabs],
            *[hbm for _ in weights],
            pl.BlockSpec((None,) + w_s.shape[1:], lambda s: (layer, 0, 0, 0)),
            *[_const_spec(c.shape) for c in consts],
        ],
        out_specs=[pl.BlockSpec((None, tq, D_MODEL), prev_block), win_spec, win_spec,
                   *[hbm for _ in weights]],
        out_shape=[
            jax.ShapeDtypeStruct((bsz, seq, D_MODEL), F32),
            jax.ShapeDtypeStruct((bsz, KV_WIDTH, WINDOW), F32),
            jax.ShapeDtypeStruct((bsz, KV_WIDTH, WINDOW), F32),
            *[jax.ShapeDtypeStruct(shp, BF16) for shp in wshapes],
        ],
        scratch_shapes=[
            *[pltpu.VMEM(shp, BF16) for shp in wshapes],
            pltpu.VMEM((GM_HEADS // 2, CHUNK, 2 * CHUNK), BF16),
            pltpu.VMEM((PREP_SLOTS, PREP_ROWS, widest), F32),
            pltpu.SemaphoreType.DMA((PREP_SLOTS,)),
            pltpu.SemaphoreType.DMA((len(weights),)),
            pltpu.VMEM((tq, D_MODEL), BF16), carry, carry, carry, carry,
        ],
        compiler_params=pltpu.CompilerParams(
            dimension_semantics=("arbitrary",), vmem_limit_bytes=VMEM_LIMIT_BYTES),
        name="prompt_layer",
    )(sinks, x, x, mod, *tabs, *weights, w_s, *consts)


def _shifted_window(old, new2, bb, t):
    wb = old.shape[2]
    lane = lax.broadcasted_iota(jnp.int32, (1, wb), 1)
    shifted = pltpu.roll(old.reshape(bb * KV_WIDTH, wb), wb - t, 1).reshape(bb, KV_WIDTH, wb)
    new_t = new2.T
    outs = []
    for n in range(bb):
        tile = new_t[:, (n * t // LANES) * LANES:(n * t // LANES + 1) * LANES]
        tail = pltpu.roll(tile, (wb - t - (n * t) % LANES) % LANES, 1)
        outs.append(jnp.where(lane >= wb - t, tail, shifted[n]))
    return outs


def _sample_kernel(sinks_ref, x_ref, mod_ref, cos_ref, sin_ref, kt_ref, vt_ref,
                   w_in_ref, wsx_ref, bexp_ref, sg_ref, sb_ref, w_o_ref, g1_ref, b1_ref,
                   wg_ref, wu_ref, wd_ref, g2_ref, b2_ref,
                   y_ref, ktn_ref, vtn_ref, gv_ref, mixcat_ref, *, bb, t):
    wb = kt_ref.shape[2]
    hb = bb // 2

    def mixer_project(st, hs):
        x3 = x_ref[hs]
        shift, scale = (mod_ref[r, hs][:, None, :] for r in range(2))
        h = (_norm(x3) * (1.0 + scale) + shift).reshape(hb * t, D_MODEL).astype(BF16)
        z = jnp.dot(h, w_in_ref[...], preferred_element_type=F32)
        cos, sin = cos_ref[0:hb * t, :], sin_ref[0:hb * t, :]
        sgn_up, sgn_dn = _rotary_lane_signs()
        sup, sdn = sin * sgn_up, sin * sgn_dn
        st["qcols"] = [_rotary(z[:, c * LANES:(c + 1) * LANES], cos, sup, sdn).reshape(hb, t, LANES)
                       for c in range(4)]
        k2 = _rotary(z[:, ATT_WIDTH:ATT_WIDTH + KV_WIDTH], cos, sup, sdn)
        v2 = z[:, ATT_WIDTH + KV_WIDTH:ATT_WIDTH + 2 * KV_WIDTH]
        st["u3"] = jax.nn.gelu(z[:, ATT_WIDTH + 2 * KV_WIDTH:ATT_WIDTH + 2 * KV_WIDTH + GM_WIDTH]
                               ).reshape(hb, t, GM_WIDTH)
        gv = _norm(jax.nn.gelu(z[:, ATT_WIDTH + 2 * KV_WIDTH + GM_WIDTH:])) * sg_ref[...] + sb_ref[...]
        st["gv3"] = gv.reshape(hb, t, GM_WIDTH)
        gv_ref[hs] = st["gv3"]
        st["k2"], st["v2"] = k2, v2
        st["kv"] = (kt_ref[hs].astype(BF16), vt_ref[hs].astype(BF16),
                    k2.reshape(hb, t, KV_WIDTH).astype(BF16), v2.reshape(hb, t, KV_WIDTH).astype(BF16))

    def update_windows(st, hs):
        for n, (ko, vo) in enumerate(zip(_shifted_window(kt_ref[hs], st["k2"], hb, t),
                                         _shifted_window(vt_ref[hs], st["v2"], hb, t))):
            ktn_ref[hs.start + n] = ko
            vtn_ref[hs.start + n] = vo

    def mixer_scores(st, hs):
        lo = _lo_half()[None]
        parts = []
        for hd in range(ATT_HEADS):
            qc = st["qcols"][hd // 2]
            if hd in HEADS_SWAPPED:
                qc = _swap_halves(qc)
            parts.append(jnp.where(lo, qc, 0.0) if hd // Q_PER_KV == 0 else jnp.where(lo, 0.0, qc))
        lhs = jnp.concatenate(parts, axis=1).astype(BF16)
        kb, _, kn, _ = st["kv"]
        st["scores"] = (jnp.einsum("bqd,bdw->bqw", lhs, kb, preferred_element_type=F32),
                        jnp.einsum("bqd,bkd->bqk", lhs, kn, preferred_element_type=F32))

    def mixer_finish(st, hs):
        nrow = ATT_HEADS * t
        tok = lax.broadcasted_iota(jnp.int32, (nrow, 1), 0) % t
        head_row = lax.broadcasted_iota(jnp.int32, (nrow, 1), 0) // t
        mask_c = (lax.broadcasted_iota(jnp.int32, (nrow, wb), 1) + (WINDOW - wb) > tok)[None]
        mask_n = (lax.broadcasted_iota(jnp.int32, (nrow, t), 1) <= tok)[None]
        _, vb, _, vn = st["kv"]
        s_c = jnp.where(mask_c, st["scores"][0], -jnp.inf)
        s_n = jnp.where(mask_n, st["scores"][1], -jnp.inf)
        sink = jnp.full((nrow, 1), sinks_ref[0], F32)
        for hd in range(1, ATT_HEADS):
            sink = jnp.where(head_row == hd, sinks_ref[hd], sink)
        sink = sink[None]
        m = jnp.maximum(jnp.maximum(jnp.max(s_c, axis=-1, keepdims=True),
                                    jnp.max(s_n, axis=-1, keepdims=True)), sink)
        e_c, e_n = jnp.exp(s_c - m), jnp.exp(s_n - m)
        inv = 1.0 / (jnp.sum(e_c, axis=-1, keepdims=True) + jnp.sum(e_n, axis=-1, keepdims=True)
                     + jnp.exp(sink - m))
        oh = (jnp.einsum("bqw,bdw->bqd", e_c.astype(BF16), vb, preferred_element_type=F32)
              + jnp.einsum("bqk,bkd->bqd", e_n.astype(BF16), vn, preferred_element_type=F32)) * inv
        o = [oh[:, hd * t:(hd + 1) * t, :] for hd in range(ATT_HEADS)]
        o = [_swap_halves(o[hd]) if hd in HEADS_SWAPPED else o[hd] for hd in range(ATT_HEADS)]
        lo = _lo_half()[None]
        att = jnp.concatenate([jnp.where(lo, o[2 * c], o[2 * c + 1]) for c in range(4)], axis=-1)
        sv = jnp.zeros((hb, t, GM_WIDTH), F32)
        for si in range(t):
            sv = sv + wsx_ref[si][None] * st["gv3"][:, si:si + 1, :]
        sgu = st["u3"] * (sv + bexp_ref[...][None])
        mixcat_ref[hs.start * t:hs.stop * t, :] = jnp.concatenate(
            [att, sgu], axis=-1).reshape(hb * t, D_MODEL).astype(BF16)

    def ffn_project(st, hs):
        st["mix"] = jnp.dot(mixcat_ref[hs.start * t:hs.stop * t, :], w_o_ref[...],
                            preferred_element_type=F32).reshape(hb, t, D_MODEL)

    def gate_up(st, cols):
        g = jnp.dot(st["hf"], wg_ref[:, cols], preferred_element_type=F32)
        up = jnp.dot(st["hf"], wu_ref[:, cols], preferred_element_type=F32)
        st["acts"].append((g * jax.nn.sigmoid(g) * up).astype(BF16))

    def ffn_gate_up_a(st, hs):
        mod = [mod_ref[r, hs][:, None, :] for r in range(6)]
        y1 = _norm(ALPHA * x_ref[hs] + mod[2] * st["mix"]) * g1_ref[...] + b1_ref[...]
        st["hf"] = (_norm(y1) * (1.0 + mod[4]) + mod[3]).reshape(hb * t, D_MODEL).astype(BF16)
        st["y1"], st["gate2"], st["acts"] = y1, mod[5], []
        gate_up(st, slice(0, FF_SPLIT))

    def ffn_gate_up_b(st, hs):
        gate_up(st, slice(FF_SPLIT, D_FF))

    def ffn_finish(st, hs):
        f = (jnp.dot(st["acts"][0], wd_ref[0:FF_SPLIT, :], preferred_element_type=F32)
             + jnp.dot(st["acts"][1], wd_ref[FF_SPLIT:D_FF, :], preferred_element_type=F32)
             ).reshape(hb, t, D_MODEL)
        y_ref[hs] = _norm(ALPHA * st["y1"] + st["gate2"] * f) * g2_ref[...] + b2_ref[...]

    first, second = slice(0, hb), slice(hb, bb)
    st_a, st_b = {}, {}
    for phase, st, hs in (
            (mixer_project, st_a, first), (mixer_project, st_b, second),
            (mixer_scores, st_a, first), (update_windows, st_a, first),
            (mixer_scores, st_b, second), (update_windows, st_b, second),
            (mixer_finish, st_a, first), (ffn_project, st_a, first),
            (mixer_finish, st_b, second), (ffn_gate_up_a, st_a, first),
            (ffn_project, st_b, second), (ffn_gate_up_b, st_a, first),
            (ffn_gate_up_a, st_b, second), (ffn_finish, st_a, first),
            (ffn_gate_up_b, st_b, second), (ffn_finish, st_b, second)):
        phase(st, hs)


def _sample_layer(x, mod, sinks, tabs, kt, vt, w_in_b, wsx, bexp_t, sgu_g, sgu_b, w_o_b,
                  g1, b1, wg, wu, wd, g2, b2, bb):
    n, t, _ = x.shape
    wb = kt.shape[2]
    assert wb == LANES and (bb // 2 * t) % LANES == 0 and LANES % t == 0
    rows = lambda r, c: pl.BlockSpec((bb, r, c), lambda s: (s, 0, 0))
    consts2 = (w_in_b, wsx, bexp_t, sgu_g, sgu_b, w_o_b, g1, b1, wg, wu, wd, g2, b2)
    return pl.pallas_call(
        functools.partial(_sample_kernel, bb=bb, t=t),
        grid=(n // bb,),
        in_specs=[
            pl.BlockSpec(memory_space=pltpu.SMEM),
            rows(t, D_MODEL), pl.BlockSpec((6, bb, D_MODEL), lambda s: (0, s, 0)),
            *[_const_spec(a.shape) for a in tabs],
            rows(KV_WIDTH, wb), rows(KV_WIDTH, wb),
            *[_const_spec(a.shape) for a in consts2],
        ],
        out_specs=[rows(t, D_MODEL), rows(KV_WIDTH, wb), rows(KV_WIDTH, wb), rows(t, GM_WIDTH)],
        out_shape=[
            jax.ShapeDtypeStruct((n, t, D_MODEL), F32),
            jax.ShapeDtypeStruct((n, KV_WIDTH, wb), F32),
            jax.ShapeDtypeStruct((n, KV_WIDTH, wb), F32),
            jax.ShapeDtypeStruct((n, t, GM_WIDTH), F32),
        ],
        scratch_shapes=[pltpu.VMEM((bb * t, D_MODEL), BF16)],
        compiler_params=pltpu.CompilerParams(
            dimension_semantics=("arbitrary",), vmem_limit_bytes=VMEM_LIMIT_BYTES),
        name="sample_layer",
    )(sinks, x, mod, *tabs, kt, vt, *consts2)


def _lane_freqs():
    half = ROT_DIM // 2
    inv_freq = np.power(np.float32(ROPE_THETA), -np.arange(half, dtype=np.float32) * np.float32(2.0 / ROT_DIM))
    per_head = np.concatenate([inv_freq, inv_freq, np.zeros((HEAD_DIM - ROT_DIM,), np.float32)])
    return np.concatenate([per_head, per_head]).astype(np.float32)


def _angle_tables(pos):
    ang = np.asarray(pos, np.float32)[:, None] * _lane_freqs()[None, :]
    return jnp.asarray(np.cos(ang), F32), jnp.asarray(np.sin(ang), F32)


def _to_feature_major(win):
    n, w = win.shape[0], win.shape[1]
    return jnp.transpose(win, (0, 2, 3, 1)).reshape(n, KV_WIDTH, w)


def _from_feature_major(win_t):
    n, _, w = win_t.shape
    return jnp.transpose(win_t.reshape(n, KV_HEADS, HEAD_DIM, w), (0, 3, 1, 2))


def kernel(x_prompt, x_sample, cache_k_win, cache_v_win, c_prompt, c_sample, w_ada, b_ada, w_in, attn_sinks, sgu_ln_g, sgu_ln_b, w_s, b_s, w_o, ln1_g, ln1_b, w_gate, w_up, w_down, ln2_g, ln2_b):
    bsz, seq, _ = x_prompt.shape
    nsm, t, _ = x_sample.shape
    tq, bb = 256, 32
    tabs_p = (*_angle_tables(np.arange(tq)), *_angle_tables(np.arange(seq // tq) * tq))
    tabs_s = _angle_tables(np.tile(PAST_LEN + np.arange(t), bb))

    yp, ys = x_prompt, x_sample
    kwp, vwp, kws, vws, sgv = [], [], [], [], []
    for l in range(DEPTH):
        bexp = jnp.repeat(b_s[l].T, HEAD_DIM, axis=1)
        wsx = jnp.repeat(jnp.transpose(jnp.tril(w_s[l, :, :t, :t]), (2, 1, 0)), HEAD_DIM, axis=-1)
        sinks = attn_sinks[l]
        sg, sb = sgu_ln_g[l][None, :], sgu_ln_b[l][None, :]
        g1, b1 = ln1_g[l][None, :], ln1_b[l][None, :]
        g2, b2 = ln2_g[l][None, :], ln2_b[l][None, :]

        mod_s, mod_p = _ada(c_sample, c_prompt, w_ada, b_ada, l)

        yp, kw, vw, w_in_b, w_o_b, wg, wu, wd = _prompt_layer(
            yp, mod_p, sinks, tabs_p, (w_in, w_o, w_gate, w_up, w_down), w_s, bexp, sg, sb,
            g1, b1, g2, b2, tq, l)
        kwp.append(_from_feature_major(kw))
        vwp.append(_from_feature_major(vw))

        ys, kn, vn, gvs = _sample_layer(
            ys, mod_s, sinks, tabs_s, _to_feature_major(cache_k_win[l]),
            _to_feature_major(cache_v_win[l]), w_in_b, wsx, bexp[:t], sg, sb, w_o_b, g1, b1, wg, wu, wd, g2, b2, bb)
        kws.append(_from_feature_major(kn))
        vws.append(_from_feature_major(vn))
        sgv.append(gvs)
    return (yp, ys, jnp.stack(kwp, axis=0), jnp.stack(vwp, axis=0), jnp.stack(kws, axis=0),
            jnp.stack(vws, axis=0), jnp.stack(sgv, axis=0))
```
